```python
import math
import jax, jax.numpy as jnp
from jax import lax
import numpy as np

D_MODEL = 1024
BATCH = 8
SEQ = 4096
DEPTH = 1

ATT_HEADS = 8
ATT_HEAD_DIM = D_MODEL // ATT_HEADS
ATT_WIDTH = ATT_HEADS * ATT_HEAD_DIM
MOBA_BLOCK = 256
MOBA_TOPK = 3
Q_CHUNK = 16
REL_BUCKETS = 32
REL_MAX_DIST = 128
ML_HEADS = 4
ML_HEAD_DIM = D_MODEL // ML_HEADS
ML_WIDTH = ML_HEADS * ML_HEAD_DIM
ML_CHUNK = 64
CONV_WIDTH = 4
EPS = 1e-6
NEG_INF = -1e30

SPLIT = [ATT_WIDTH] * 4 + [ML_WIDTH] * 5 + [ML_HEADS, ML_HEADS] + [D_MODEL, D_MODEL]
D_IN = sum(SPLIT)
SPLIT_IDX = tuple(int(i) for i in np.cumsum(SPLIT)[:-1])

kernel_name = 'moba_mlstm_gated_hybrid'


def _rmsnorm(x, w):
    xf = x.astype(jnp.float32)
    y = xf * lax.rsqrt(jnp.mean(xf * xf, axis=-1, keepdims=True) + EPS)
    return (y * w.astype(jnp.float32)).astype(x.dtype)


def _rel_bucket(dist):
    max_exact = REL_BUCKETS // 2
    n = jnp.maximum(dist, 0)
    nf = jnp.maximum(n, 1).astype(jnp.float32)
    large = max_exact + (jnp.log(nf / max_exact) / math.log(REL_MAX_DIST / max_exact)
                         * (REL_BUCKETS - max_exact)).astype(jnp.int32)
    large = jnp.minimum(large, REL_BUCKETS - 1)
    return jnp.where(n < max_exact, n, large)


def _moba_attention(q, k, v, rel_bias):
    B, H, S, d = q.shape
    nb = -(-S // MOBA_BLOCK)
    s_pad = nb * MOBA_BLOCK
    topk = min(MOBA_TOPK, nb)
    pad = ((0, 0), (0, 0), (0, s_pad - S), (0, 0))
    kp = jnp.pad(k, pad)
    vp = jnp.pad(v, pad)
    k_blocks = kp.reshape(B, H, nb, MOBA_BLOCK, d)
    v_blocks = vp.reshape(B, H, nb, MOBA_BLOCK, d)
    k_mean = jnp.mean(k_blocks.astype(jnp.float32), axis=3)
    bias_hb = rel_bias.T.astype(jnp.float32)
    scale = d ** -0.5
    bi = jnp.arange(B)[:, None, None, None]
    hi = jnp.arange(H)[None, :, None, None]
    hi5 = jnp.arange(H)[None, :, None, None, None]
    blk_ids = jnp.arange(nb)
    offs = jnp.arange(MOBA_BLOCK)

    def chunk(start):
        qc = lax.dynamic_slice_in_dim(q, start, Q_CHUNK, axis=2).astype(jnp.float32)
        qpos = start + jnp.arange(Q_CHUNK)
        qblk = start // MOBA_BLOCK
        gate = jnp.einsum('bhqd,bhnd->bhqn', qc, k_mean)
        gate = jnp.where(blk_ids < qblk, gate, NEG_INF)
        _, idx = lax.top_k(gate, topk)
        sel_ok = idx < qblk
        k_sel = k_blocks[bi, hi, idx]
        v_sel = v_blocks[bi, hi, idx]
        kpos_sel = idx[..., None] * MOBA_BLOCK + offs
        bias_sel = bias_hb[hi5, _rel_bucket(qpos[:, None, None] - kpos_sel)]
        logit_sel = jnp.einsum('bhqd,bhqkjd->bhqkj', qc, k_sel) * scale + bias_sel
        logit_sel = jnp.where(sel_ok[..., None], logit_sel, NEG_INF)
        logit_sel = logit_sel.reshape(B, H, Q_CHUNK, topk * MOBA_BLOCK)
        own_start = qblk * MOBA_BLOCK
        k_own = lax.dynamic_slice_in_dim(kp, own_start, MOBA_BLOCK, axis=2)
        v_own = lax.dynamic_slice_in_dim(vp, own_start, MOBA_BLOCK, axis=2)
        dist_own = qpos[:, None] - (own_start + offs)[None, :]
        bias_own = bias_hb[:, _rel_bucket(dist_own)]
        logit_own = jnp.einsum('bhqd,bhjd->bhqj', qc, k_own) * scale + bias_own
        logit_own = jnp.where(dist_own >= 0, logit_own, NEG_INF)
        p = jax.nn.softmax(jnp.concatenate([logit_sel, logit_own], axis=-1), axis=-1)
        p_sel = p[..., :topk * MOBA_BLOCK].reshape(B, H, Q_CHUNK, topk, MOBA_BLOCK)
        p_own = p[..., topk * MOBA_BLOCK:]
        o = (jnp.einsum('bhqkj,bhqkjd->bhqd', p_sel, v_sel)
             + jnp.einsum('bhqj,bhjd->bhqd', p_own, v_own))
        return o.astype(q.dtype)

    starts = jnp.arange(S // Q_CHUNK) * Q_CHUNK
    out = lax.map(chunk, starts)
    return jnp.moveaxis(out, 0, 2).reshape(B, H, S, d)


def _causal_dwconv(u, w, b):
    y = lax.conv_general_dilated(u, w[:, None, :].astype(u.dtype), (1,), [(CONV_WIDTH - 1, 0)],
                                 dimension_numbers=('NWC', 'WIO', 'NWC'),
                                 feature_group_count=u.shape[-1])
    return y + b.astype(u.dtype)


def _mlstm_chunkwise(q, k, v, ig, fg):
    B, NH, S, dh = q.shape
    L = ML_CHUNK
    nc = S // L

    def to_chunks(t):
        return jnp.moveaxis(t.reshape((B, NH, nc, L) + t.shape[3:]), 2, 0)

    logf = jax.nn.log_sigmoid(fg)
    causal = jnp.tril(jnp.ones((L, L), dtype=bool))

    def step(carry, xs):
        C, n, m = carry
        qc, kc, vc, igc, lfc = xs
        b = jnp.cumsum(lfc, axis=-1)
        log_d = jnp.where(causal, b[..., :, None] - b[..., None, :] + igc[..., None, :], -jnp.inf)
        m_t = jnp.maximum(b + m[..., None], jnp.max(log_d, axis=-1))
        d_mat = jnp.exp(log_d - m_t[..., None])
        inter = jnp.exp(b + m[..., None] - m_t)
        s = jnp.einsum('bhtd,bhsd->bhts', qc, kc) * d_mat
        num = (inter[..., None] * jnp.einsum('bhtd,bhde->bhte', qc, C)
               + jnp.einsum('bhts,bhse->bhte', s, vc))
        den = inter * jnp.einsum('bhtd,bhd->bht', qc, n) + jnp.sum(s, axis=-1)
        h = num / jnp.maximum(jnp.abs(den), jnp.exp(-m_t))[..., None]
        b_last = b[..., -1]
        w_log = b_last[..., None] - b + igc
        m_new = jnp.maximum(b_last + m, jnp.max(w_log, axis=-1))
        decay = jnp.exp(b_last + m - m_new)
        w_s = jnp.exp(w_log - m_new[..., None])
        C_new = decay[..., None, None] * C + jnp.einsum('bhs,bhsd,bhse->bhde', w_s, kc, vc)
        n_new = decay[..., None] * n + jnp.einsum('bhs,bhsd->bhd', w_s, kc)
        return (C_new, n_new, m_new), h

    init = (jnp.zeros((B, NH, dh, dh), jnp.float32),
            jnp.zeros((B, NH, dh), jnp.float32),
            jnp.zeros((B, NH), jnp.float32))
    _, hs = lax.scan(step, init, (to_chunks(q), to_chunks(k), to_chunks(v),
                                  to_chunks(ig), to_chunks(logf)))
    return jnp.moveaxis(hs, 0, 2).reshape(B, NH, S, dh)


def _hybrid_layer(x, c, w_ada, b_ada, norm_w, w_in, q_norm_w, k_norm_w, rel_bias, conv_w, conv_b,
                  b_igate, b_fgate, ml_norm_w, w_att_proj, w_ml_proj, w_out):
    B, S, _ = x.shape
    f32 = jnp.float32
    ada = c @ w_ada + b_ada
    shift, scale, gate = jnp.split(ada[:, None, :], 3, axis=-1)
    h = _rmsnorm(x, norm_w) * (1 + scale) + shift
    (qa, ka, va, za, qm, km, vm, zm, om, ig, fg, ga, gm) = jnp.split(h @ w_in, SPLIT_IDX, axis=-1)

    def heads(t, nh):
        return t.reshape(B, S, nh, -1).transpose(0, 2, 1, 3)

    def merge(t):
        return t.transpose(0, 2, 1, 3).reshape(B, S, -1)

    qa = _rmsnorm(heads(qa, ATT_HEADS), q_norm_w)
    ka = _rmsnorm(heads(ka, ATT_HEADS), k_norm_w)
    ya = merge(_moba_attention(qa, ka, heads(va, ATT_HEADS), rel_bias))
    ya = (ya * jax.nn.silu(za)) @ w_att_proj

    qk = jax.nn.silu(_causal_dwconv(jnp.concatenate([qm, km], axis=-1), conv_w, conv_b))
    qm, km = jnp.split(qk, 2, axis=-1)
    hm = _mlstm_chunkwise(heads(qm, ML_HEADS).astype(f32),
                          heads(km, ML_HEADS).astype(f32) * (ML_HEAD_DIM ** -0.5),
                          heads(vm, ML_HEADS).astype(f32),
                          (ig + b_igate).astype(f32).transpose(0, 2, 1),
                          (fg + b_fgate).astype(f32).transpose(0, 2, 1))
    hm = jax.nn.sigmoid(heads(om, ML_HEADS).astype(f32)) * hm
    hm = _rmsnorm(hm, ml_norm_w.reshape(ML_HEADS, 1, ML_HEAD_DIM))
    ym = (merge(hm).astype(x.dtype) * jax.nn.silu(zm)) @ w_ml_proj

    y = jax.nn.sigmoid(ga) * ya + jax.nn.sigmoid(gm) * ym
    return x + gate * (y @ w_out)


def setup_inputs(seed: int = 0) -> dict:
    key = jax.random.key(seed)
    ks = jax.random.split(key, 18)
    f32 = jnp.float32

    def nrm(k, shape, s):
        return jax.random.normal(k, shape, f32) * s

    D = D_MODEL
    return {
        'x': nrm(ks[0], (BATCH, SEQ, D), 1.0),
        'c': nrm(ks[1], (BATCH, D), 1.0),
        'w_ada': nrm(ks[2], (DEPTH, D, 3 * D), 0.5 * D ** -0.5),
        'b_ada': nrm(ks[3], (DEPTH, 3 * D), 0.02),
        'norm_w': 1.0 + nrm(ks[4], (DEPTH, D), 0.02),
        'w_in': nrm(ks[5], (DEPTH, D, D_IN), D ** -0.5),
        'q_norm_w': 1.0 + nrm(ks[6], (DEPTH, ATT_HEAD_DIM), 0.02),
        'k_norm_w': 1.0 + nrm(ks[7], (DEPTH, ATT_HEAD_DIM), 0.02),
        'rel_bias': nrm(ks[8], (REL_BUCKETS, ATT_HEADS), 0.5),
        'conv_w': nrm(ks[9], (DEPTH, CONV_WIDTH, 2 * ML_WIDTH), CONV_WIDTH ** -0.5),
        'conv_b': nrm(ks[10], (DEPTH, 2 * ML_WIDTH), 0.02),
        'b_igate': nrm(ks[11], (DEPTH, ML_HEADS), 0.1),
        'b_fgate': jnp.linspace(3.0, 6.0, ML_HEADS, dtype=f32)[None, :] + nrm(ks[12], (DEPTH, ML_HEADS), 0.1),
        'ml_norm_w': 1.0 + nrm(ks[13], (DEPTH, ML_WIDTH), 0.02),
        'w_att_proj': nrm(ks[14], (DEPTH, ATT_WIDTH, D), ATT_WIDTH ** -0.5),
        'w_ml_proj': nrm(ks[15], (DEPTH, ML_WIDTH, D), ML_WIDTH ** -0.5),
        'w_out': nrm(ks[16], (DEPTH, D, D), D ** -0.5),
    }


def reference(x, c, w_ada, b_ada, norm_w, w_in, q_norm_w, k_norm_w, rel_bias, conv_w, conv_b,
              b_igate, b_fgate, ml_norm_w, w_att_proj, w_ml_proj, w_out):
    for l in range(DEPTH):
        x = _hybrid_layer(x, c, w_ada[l], b_ada[l], norm_w[l], w_in[l], q_norm_w[l], k_norm_w[l],
                          rel_bias, conv_w[l], conv_b[l], b_igate[l], b_fgate[l], ml_norm_w[l],
                          w_att_proj[l], w_ml_proj[l], w_out[l])
    return x
```

```python
import functools
import math

import jax
import jax.numpy as jnp
import numpy as np
from jax import lax
from jax.experimental import pallas as pl
from jax.experimental.pallas import tpu as pltpu

F32 = jnp.float32
BF16 = jnp.bfloat16

D_MODEL = 1024
ATT_HEADS = 8
ATT_HEAD_DIM = D_MODEL // ATT_HEADS
MOBA_BLOCK = 256
MOBA_TOPK = 3
REL_BUCKETS = 32
REL_MAX_DIST = 128
ML_HEADS = 4
ML_HEAD_DIM = D_MODEL // ML_HEADS
CONV_WIDTH = 4
EPS = 1e-6
NEG_INF = -1e30
LOG2E = math.log2(math.e)

LANES = 128
SUBLANES = 8
GATE_COLS = 2 * ML_HEADS
ML_CHUNK = 256
PROJ_TM = 1024
PROJ_TN = 1024
OUT_TM = 512
VMEM_LIMIT = 52 * 1024 * 1024

COL_QA, COL_KA, COL_VA, COL_ZA, COL_QM, COL_KM, COL_VM, COL_ZM, COL_OM, COL_GA, COL_GM = range(11)
N_COL_TILES = 11

_NT = (((1,), (1,)), ((), ()))
_TN = (((0,), (0,)), ((), ()))


def _sigmoid(x):
    return 1.0 / (1.0 + jnp.exp(-x))


def _log_sigmoid(x):
    return jnp.minimum(x, 0.0) - jnp.log1p(jnp.exp(-jnp.abs(x)))


def _ada_kernel(c_ref, w_ref, b_ref, o_ref):
    o_ref[...] = jnp.dot(c_ref[...], w_ref[...], precision=lax.Precision.HIGHEST,
                         preferred_element_type=F32) + b_ref[...]


def _ada_call(c, w_ada, b_ada):
    B, D = c.shape
    return pl.pallas_call(
        _ada_kernel,
        grid=(3,),
        in_specs=[pl.BlockSpec((B, D), lambda j: (0, 0)),
                  pl.BlockSpec((D, D), lambda j: (0, j)),
                  pl.BlockSpec((1, D), lambda j: (0, j))],
        out_specs=pl.BlockSpec((B, D), lambda j: (0, j)),
        out_shape=jax.ShapeDtypeStruct((B, 3 * D), F32),
        compiler_params=pltpu.CompilerParams(dimension_semantics=("arbitrary",),
                                             vmem_limit_bytes=VMEM_LIMIT),
        name="ada",
    )(c, w_ada, b_ada.reshape(1, 3 * D))


def _inproj_kernel(x_ref, ada_ref, nw_ref, w_ref, wg_ref, gb_ref, hnw_ref,
                   p_ref, g_ref, gt_ref, h_scr):
    j = pl.program_id(1)

    @pl.when(j == 0)
    def _():
        x = x_ref[...]
        ada = ada_ref[0]
        y = x * lax.rsqrt(jnp.mean(x * x, axis=-1, keepdims=True) + EPS) * nw_ref[...]
        h = y * (1.0 + ada[1:2, :]) + ada[0:1, :]
        h_scr[...] = h.astype(BF16)
        g = jnp.dot(h, wg_ref[...], precision=lax.Precision.HIGHEST,
                    preferred_element_type=F32) + gb_ref[...]
        g_ref[...] = g
        gt_ref[...] = g.T[:GATE_COLS, :]

    acc = jnp.dot(h_scr[...], w_ref[...], preferred_element_type=F32)

    @pl.when(j <= COL_KA)
    def _():
        w = hnw_ref[pl.ds(j, 1), :]
        for hd in range(ATT_HEADS):
            sl = slice(hd * ATT_HEAD_DIM, (hd + 1) * ATT_HEAD_DIM)
            a = acc[:, sl]
            r = lax.rsqrt(jnp.mean(a * a, axis=-1, keepdims=True) + EPS)
            p_ref[:, sl] = (a * r * w[:, sl]).astype(BF16)

    @pl.when(j > COL_KA)
    def _():
        p_ref[...] = acc.astype(BF16)


def _inproj_call(x2, ada3, norm_w, w_main, wg, gb, hnw, seq):
    T, D = x2.shape
    tm, tn = PROJ_TM, PROJ_TN
    tiles_per_seq = seq // tm
    return pl.pallas_call(
        _inproj_kernel,
        grid=(T // tm, N_COL_TILES),
        in_specs=[pl.BlockSpec((tm, D), lambda i, j: (i, 0)),
                  pl.BlockSpec((1, 3, D), lambda i, j: (i // tiles_per_seq, 0, 0)),
                  pl.BlockSpec((1, D), lambda i, j: (0, 0)),
                  pl.BlockSpec((D, tn), lambda i, j: (0, j)),
                  pl.BlockSpec((D, LANES), lambda i, j: (0, 0)),
                  pl.BlockSpec((1, LANES), lambda i, j: (0, 0)),
                  pl.BlockSpec((2, tn), lambda i, j: (0, 0))],
        out_specs=[pl.BlockSpec((tm, tn), lambda i, j: (i, j)),
                   pl.BlockSpec((tm, LANES), lambda i, j: (i, 0)),
                   pl.BlockSpec((GATE_COLS, tm), lambda i, j: (0, i))],
        out_shape=[jax.ShapeDtypeStruct((T, N_COL_TILES * tn), BF16),
                   jax.ShapeDtypeStruct((T, LANES), F32),
                   jax.ShapeDtypeStruct((GATE_COLS, T), F32)],
        scratch_shapes=[pltpu.VMEM((tm, D), BF16)],
        compiler_params=pltpu.CompilerParams(dimension_semantics=("arbitrary", "arbitrary"),
                                             vmem_limit_bytes=VMEM_LIMIT),
        name="inproj",
    )(x2, ada3, norm_w, w_main, wg, gb, hnw)


def _moba_kernel(q_ref, k_ref, v_ref, za_ref, bown_ref, bprev_ref, o_ref, kmean_scr, selb_scr):
    qi = pl.program_id(2)
    blk = MOBA_BLOCK
    nb = k_ref.shape[0] // blk

    @pl.when(qi == 0)
    def _():
        for n in range(nb):
            kb = k_ref[n * blk:(n + 1) * blk, :].astype(F32)
            kmean_scr[n:n + 1, :] = jnp.mean(kb, axis=0, keepdims=True)

    q = q_ref[...]
    gate = lax.dot_general(kmean_scr[...], q.astype(F32), _NT, precision=lax.Precision.HIGHEST,
                           preferred_element_type=F32)
    rows = lax.broadcasted_iota(jnp.int32, gate.shape, 0)
    valid = rows < qi
    g = jnp.where(valid, gate, -jnp.inf)
    selb = jnp.full(gate.shape, NEG_INF, F32)
    for _ in range(MOBA_TOPK):
        mx = jnp.max(g, axis=0, keepdims=True)
        first = jnp.min(jnp.where(g == mx, rows, nb), axis=0, keepdims=True)
        pick = rows == first
        selb = jnp.where(pick, 0.0, selb)
        g = jnp.where(pick, -jnp.inf, g)
    selb_scr[...] = jnp.where(valid, selb, NEG_INF)

    def scores(j):
        kj = k_ref[pl.ds(pl.multiple_of(j * blk, blk), blk), :]
        return lax.dot_general(kj, q, _NT, preferred_element_type=F32)

    def pv(j, p):
        vj = v_ref[pl.ds(pl.multiple_of(j * blk, blk), blk), :]
        return lax.dot_general(vj, p.astype(BF16), _TN, preferred_element_type=F32)

    def update(carry, j, s):
        m, l, acc = carry
        m_new = jnp.maximum(m, jnp.max(s, axis=0, keepdims=True))
        alpha = jnp.exp2(m - m_new)
        p = jnp.exp2(s - m_new)
        return (m_new, alpha * l + jnp.sum(p, axis=0, keepdims=True), alpha * acc + pv(j, p))

    s = scores(qi) + bown_ref[0]
    m = jnp.max(s, axis=0, keepdims=True)
    p = jnp.exp2(s - m)
    carry = (m, jnp.sum(p, axis=0, keepdims=True), pv(qi, p))

    def prev_block(c):
        j = qi - 1
        return update(c, j, scores(j) + bprev_ref[0] + selb_scr[pl.ds(j, 1), :])

    carry = lax.cond(qi >= 1, prev_block, lambda c: c, carry)

    def far_block(j, c):
        return update(c, j, scores(j) + selb_scr[pl.ds(j, 1), :])

    m, l, acc = lax.fori_loop(0, jnp.maximum(qi - 1, 0), far_block, carry)

    y = (acc * (1.0 / l)).T
    za = za_ref[...].astype(F32)
    o_ref[...] = (y * (za * _sigmoid(za))).astype(BF16)


def _moba_call(p, bias_own, bias_prev, batch, seq):
    T = p.shape[0]
    blk, d = MOBA_BLOCK, ATT_HEAD_DIM
    nq = seq // blk
    hpt = PROJ_TN // d
    return pl.pallas_call(
        _moba_kernel,
        grid=(batch, ATT_HEADS, nq),
        in_specs=[pl.BlockSpec((blk, d), lambda b, h, i: (b * nq + i, COL_QA * hpt + h)),
                  pl.BlockSpec((seq, d), lambda b, h, i: (b, COL_KA * hpt + h)),
                  pl.BlockSpec((seq, d), lambda b, h, i: (b, COL_VA * hpt + h)),
                  pl.BlockSpec((blk, d), lambda b, h, i: (b * nq + i, COL_ZA * hpt + h)),
                  pl.BlockSpec((1, blk, blk), lambda b, h, i: (h, 0, 0)),
                  pl.BlockSpec((1, blk, blk), lambda b, h, i: (h, 0, 0))],
        out_specs=pl.BlockSpec((blk, d), lambda b, h, i: (b * nq + i, h)),
        out_shape=jax.ShapeDtypeStruct((T, ATT_HEADS * d), BF16),
        scratch_shapes=[pltpu.VMEM((seq // blk, d), F32),
                        pltpu.VMEM((seq // blk, blk), F32)],
        compiler_params=pltpu.CompilerParams(
            dimension_semantics=("arbitrary", "arbitrary", "arbitrary"),
            vmem_limit_bytes=VMEM_LIMIT),
        name="moba",
    )(p, p, p, p, bias_own, bias_prev)


def _mlstm_kernel(qm_ref, km_ref, vm_ref, zm_ref, om_ref, g_ref, gt_ref, cw_ref, cb_ref, mlw_ref,
                  o_ref, c_scr, n_scr, m_scr, extq_scr, extk_scr):
    c_idx = pl.program_id(1)
    L = qm_ref.shape[0]
    W = ML_HEADS * ML_HEAD_DIM
    pad = SUBLANES

    @pl.when(c_idx == 0)
    def _():
        c_scr[...] = jnp.zeros_like(c_scr)
        n_scr[...] = jnp.zeros_like(n_scr)
        m_scr[...] = jnp.zeros_like(m_scr)
        extq_scr[0:pad, :] = jnp.zeros((pad, W), F32)
        extk_scr[0:pad, :] = jnp.zeros((pad, W), F32)

    def conv_silu(src_ref, ext_scr, col0):
        ext_scr[pad:pad + L, :] = src_ref[...].astype(F32)
        y = cb_ref[:, col0:col0 + W]
        for w in range(CONV_WIDTH):
            off = pad - (CONV_WIDTH - 1) + w
            y = y + cw_ref[w:w + 1, col0:col0 + W] * ext_scr[off:off + L, :]
        ext_scr[0:pad, :] = ext_scr[L:L + pad, :]
        return y * _sigmoid(y)

    qc = conv_silu(qm_ref, extq_scr, 0)
    kc = conv_silu(km_ref, extk_scr, W) * (ML_HEAD_DIM ** -0.5)

    r_i = lax.broadcasted_iota(jnp.int32, (L, L), 0)
    c_i = lax.broadcasted_iota(jnp.int32, (L, L), 1)
    causal = c_i <= r_i
    ltri = causal.astype(F32)
    utri = (r_i <= c_i).astype(F32)
    g_col = g_ref[...]
    g_row = gt_ref[...]
    b_col_all = jnp.dot(ltri, _log_sigmoid(g_col), precision=lax.Precision.HIGHEST,
                        preferred_element_type=F32)
    b_row_all = jnp.dot(_log_sigmoid(g_row), utri, precision=lax.Precision.HIGHEST,
                        preferred_element_type=F32)

    for hh in range(ML_HEADS):
        sl = slice(hh * ML_HEAD_DIM, (hh + 1) * ML_HEAD_DIM)
        q = qc[:, sl]
        k = kc[:, sl]
        v = vm_ref[:, sl]
        b_col = b_col_all[:, ML_HEADS + hh:ML_HEADS + hh + 1]
        a_col = g_col[:, hh:hh + 1] - b_col
        a_row = g_row[hh:hh + 1, :] - b_row_all[ML_HEADS + hh:ML_HEADS + hh + 1, :]
        m_prev = m_scr[hh][0:1, 0:1]

        a_mask = jnp.where(causal, a_row, -jnp.inf)
        gcol = jnp.maximum(m_prev, jnp.max(a_mask, axis=1, keepdims=True))
        d_mat = jnp.exp(a_mask - gcol)
        inter = jnp.exp(m_prev - gcol)
        g_last = jnp.max(gcol, axis=0, keepdims=True)

        qb = q.astype(BF16)
        s = lax.dot_general(qb, k.astype(BF16), _NT, preferred_element_type=F32) * d_mat
        c_old = c_scr[hh]
        n_old = n_scr[hh]
        num = (inter * jnp.dot(qb, c_old.astype(BF16), preferred_element_type=F32)
               + jnp.dot(s.astype(BF16), v, preferred_element_type=F32))
        den = inter * jnp.sum(q * n_old, axis=1, keepdims=True) + jnp.sum(s, axis=1, keepdims=True)
        h = num / jnp.maximum(jnp.abs(den), jnp.exp(-(b_col + gcol)))

        decay = jnp.exp(m_prev - g_last)
        kw = k * jnp.exp(a_col - g_last)
        c_scr[hh] = decay * c_old + lax.dot_general(kw.astype(BF16), v, _TN,
                                                    preferred_element_type=F32)
        n_scr[hh] = decay * n_old + jnp.sum(kw, axis=0, keepdims=True)
        m_scr[hh] = jnp.broadcast_to(b_col[L - 1:L, :] + g_last, m_scr.shape[1:])

        hg = _sigmoid(om_ref[:, sl].astype(F32)) * h
        hn = hg * lax.rsqrt(jnp.mean(hg * hg, axis=-1, keepdims=True) + EPS) * mlw_ref[:, sl]
        z = zm_ref[:, sl].astype(F32)
        o_ref[:, sl] = (hn * (z * _sigmoid(z))).astype(BF16)


def _mlstm_call(p, g, gt, conv_w, conv_b, ml_norm_w, batch, seq):
    T = p.shape[0]
    L = ML_CHUNK
    nc = seq // L
    W = ML_HEADS * ML_HEAD_DIM
    row = lambda b, c: b * nc + c
    pspec = lambda col: pl.BlockSpec((L, W), lambda b, c: (row(b, c), col))
    return pl.pallas_call(
        _mlstm_kernel,
        grid=(batch, nc),
        in_specs=[pspec(COL_QM), pspec(COL_KM), pspec(COL_VM), pspec(COL_ZM), pspec(COL_OM),
                  pl.BlockSpec((L, LANES), lambda b, c: (row(b, c), 0)),
                  pl.BlockSpec((GATE_COLS, L), lambda b, c: (0, row(b, c))),
                  pl.BlockSpec((CONV_WIDTH, 2 * W), lambda b, c: (0, 0)),
                  pl.BlockSpec((1, 2 * W), lambda b, c: (0, 0)),
                  pl.BlockSpec((1, W), lambda b, c: (0, 0))],
        out_specs=pl.BlockSpec((L, W), lambda b, c: (row(b, c), 0)),
        out_shape=jax.ShapeDtypeStruct((T, W), BF16),
        scratch_shapes=[pltpu.VMEM((ML_HEADS, ML_HEAD_DIM, ML_HEAD_DIM), F32),
                        pltpu.VMEM((ML_HEADS, 1, ML_HEAD_DIM), F32),
                        pltpu.VMEM((ML_HEADS, SUBLANES, LANES), F32),
                        pltpu.VMEM((L + SUBLANES, W), F32),
                        pltpu.VMEM((L + SUBLANES, W), F32)],
        compiler_params=pltpu.CompilerParams(dimension_semantics=("arbitrary", "arbitrary"),
                                             vmem_limit_bytes=VMEM_LIMIT),
        name="mlstm",
    )(p, p, p, p, p, g, gt, conv_w, conv_b, ml_norm_w)


def _outp_kernel(ua_ref, um_ref, ga_ref, gm_ref, x_ref, ada_ref, wa_ref, wm_ref, wo_ref, o_ref):
    ya = jnp.dot(ua_ref[...], wa_ref[...], preferred_element_type=F32)
    ym = jnp.dot(um_ref[...], wm_ref[...], preferred_element_type=F32)
    y = _sigmoid(ga_ref[...].astype(F32)) * ya + _sigmoid(gm_ref[...].astype(F32)) * ym
    gate = ada_ref[0][2:3, :]
    o_ref[...] = x_ref[...] + gate * jnp.dot(y.astype(BF16), wo_ref[...], preferred_element_type=F32)


def _outp_call(ua, um, p, x2, ada3, wa, wm, wo, seq):
    T, D = x2.shape
    tm = OUT_TM
    tiles_per_seq = seq // tm
    act = lambda col: pl.BlockSpec((tm, D), lambda i: (i, col))
    wspec = pl.BlockSpec((D, D), lambda i: (0, 0))
    return pl.pallas_call(
        _outp_kernel,
        grid=(T // tm,),
        in_specs=[act(0), act(0), act(COL_GA), act(COL_GM), act(0),
                  pl.BlockSpec((1, 3, D), lambda i: (i // tiles_per_seq, 0, 0)),
                  wspec, wspec, wspec],
        out_specs=pl.BlockSpec((tm, D), lambda i: (i, 0)),
        out_shape=jax.ShapeDtypeStruct((T, D), F32),
        compiler_params=pltpu.CompilerParams(dimension_semantics=("arbitrary",),
                                             vmem_limit_bytes=VMEM_LIMIT),
        name="outp",
    )(ua, um, p, p, x2, ada3, wa, wm, wo)


def _rel_bias_tables(rel_bias):
    max_exact = REL_BUCKETS // 2
    dist = jnp.arange(2 * MOBA_BLOCK)
    nf = jnp.maximum(dist, 1).astype(F32)
    large = max_exact + (jnp.log(nf / max_exact) / math.log(REL_MAX_DIST / max_exact)
                         * (REL_BUCKETS - max_exact)).astype(jnp.int32)
    bucket = jnp.where(dist < max_exact, dist, jnp.minimum(large, REL_BUCKETS - 1))
    rb = rel_bias.astype(F32)
    table = (rb[bucket, :] - rb[REL_BUCKETS - 1, :]).T * LOG2E
    key = jnp.arange(MOBA_BLOCK)[:, None]
    qry = jnp.arange(MOBA_BLOCK)[None, :]
    d_own = qry - key
    own = jnp.where(d_own >= 0, table[:, jnp.maximum(d_own, 0)], NEG_INF)
    prev = table[:, d_own + MOBA_BLOCK]
    return own, prev


def _layer(x2, c, w_ada, b_ada, norm_w, w_in, q_norm_w, k_norm_w, rel_bias, conv_w, conv_b,
           b_igate, b_fgate, ml_norm_w, w_att_proj, w_ml_proj, w_out, batch, seq):
    D = D_MODEL
    g0 = 4 * D + 5 * D
    ada3 = _ada_call(c, w_ada, b_ada).reshape(batch, 3, D)

    w_main = jnp.concatenate([w_in[:, :g0], w_in[:, g0 + GATE_COLS:]], axis=1).astype(BF16)
    wg = jnp.pad(w_in[:, g0:g0 + GATE_COLS], ((0, 0), (0, LANES - GATE_COLS)))
    gb = jnp.pad(jnp.concatenate([b_igate, b_fgate]), (0, LANES - GATE_COLS)).reshape(1, LANES)
    q_scale = ATT_HEAD_DIM ** -0.5 * LOG2E
    hnw = jnp.stack([jnp.tile(q_norm_w.astype(F32), ATT_HEADS) * q_scale,
                     jnp.tile(k_norm_w.astype(F32), ATT_HEADS)])

    p, g, gt = _inproj_call(x2, ada3, norm_w.reshape(1, D), w_main, wg, gb, hnw, seq)
    bias_own, bias_prev = _rel_bias_tables(rel_bias)
    ua = _moba_call(p, bias_own, bias_prev, batch, seq)
    um = _mlstm_call(p, g, gt, conv_w, conv_b.reshape(1, -1), ml_norm_w.reshape(1, -1), batch, seq)
    return _outp_call(ua, um, p, x2, ada3, w_att_proj.astype(BF16), w_ml_proj.astype(BF16),
                      w_out.astype(BF16), seq)


def kernel(x, c, w_ada, b_ada, norm_w, w_in, q_norm_w, k_norm_w, rel_bias, conv_w, conv_b, b_igate,
           b_fgate, ml_norm_w, w_att_proj, w_ml_proj, w_out):
    batch, seq, D = x.shape
    assert D == D_MODEL and seq % PROJ_TM == 0 and seq % MOBA_BLOCK == 0 and seq % ML_CHUNK == 0
    x2 = x.reshape(batch * seq, D)
    for l in range(w_in.shape[0]):
        x2 = _layer(x2, c, w_ada[l], b_ada[l], norm_w[l], w_in[l], q_norm_w[l], k_norm_w[l],
                    rel_bias, conv_w[l], conv_b[l], b_igate[l], b_fgate[l], ml_norm_w[l],
                    w_att_proj[l], w_ml_proj[l], w_out[l], batch, seq)
    return x2.reshape(batch, seq, D)
```

```python
import functools
import math

import jax
import jax.numpy as jnp
import numpy as np
from jax import lax
from jax.experimental import pallas as pl
from jax.experimental.pallas import tpu as pltpu

F32 = jnp.float32
BF16 = jnp.bfloat16

D_MODEL = 1024
ATT_HEADS = 8
ATT_HEAD_DIM = D_MODEL // ATT_HEADS
MOBA_BLOCK = 256
MOBA_TOPK = 3
REL_BUCKETS = 32
REL_MAX_DIST = 128
ML_HEADS = 4
ML_HEAD_DIM = D_MODEL // ML_HEADS
CONV_WIDTH = 4
EPS = 1e-6
NEG_INF = -1e30
LOG2E = math.log2(math.e)

LANES = 128
SUBLANES = 8
GATE_COLS = 2 * ML_HEADS
ML_CHUNK = 256
PROJ_TM = 1024
PROJ_TN = 1024
OUT_TM = 512
MOBA_HEAD_GROUP = 4
VMEM_LIMIT = 52 * 1024 * 1024

COL_QA, COL_KA, COL_VA, COL_ZA, COL_QM, COL_KM, COL_VM, COL_ZM, COL_OM, COL_GA, COL_GM = range(11)
N_COL_TILES = 11

_NT = (((1,), (1,)), ((), ()))
_TN = (((0,), (0,)), ((), ()))


def _sigmoid(x):
    return 1.0 / (1.0 + jnp.exp(-x))


def _log_sigmoid(x):
    return jnp.minimum(x, 0.0) - jnp.log1p(jnp.exp(-jnp.abs(x)))


def _ada_kernel(c_ref, w_ref, b_ref, o_ref):
    o_ref[...] = jnp.dot(c_ref[...], w_ref[...], precision=lax.Precision.HIGHEST,
                         preferred_element_type=F32) + b_ref[...]


def _ada_call(c, w_ada, b_ada):
    B, D = c.shape
    return pl.pallas_call(
        _ada_kernel,
        grid=(3,),
        in_specs=[pl.BlockSpec((B, D), lambda j: (0, 0)),
                  pl.BlockSpec((D, D), lambda j: (0, j)),
                  pl.BlockSpec((1, D), lambda j: (0, j))],
        out_specs=pl.BlockSpec((B, D), lambda j: (0, j)),
        out_shape=jax.ShapeDtypeStruct((B, 3 * D), F32),
        compiler_params=pltpu.CompilerParams(dimension_semantics=("arbitrary",),
                                             vmem_limit_bytes=VMEM_LIMIT),
        name="ada",
    )(c, w_ada, b_ada.reshape(1, 3 * D))


def _inproj_kernel(x_ref, ada_ref, nw_ref, w_ref, wg_ref, gb_ref, hnw_ref,
                   p_ref, g_ref, gt_ref, h_scr):
    j = pl.program_id(1)

    @pl.when(j == 0)
    def _():
        x = x_ref[...]
        ada = ada_ref[0]
        y = x * lax.rsqrt(jnp.mean(x * x, axis=-1, keepdims=True) + EPS) * nw_ref[...]
        h = y * (1.0 + ada[1:2, :]) + ada[0:1, :]
        h_scr[...] = h.astype(BF16)
        g = jnp.dot(h, wg_ref[...], precision=lax.Precision.HIGHEST,
                    preferred_element_type=F32) + gb_ref[...]
        g_ref[...] = g
        gt_ref[...] = g.T[:GATE_COLS, :]

    acc = jnp.dot(h_scr[...], w_ref[...], preferred_element_type=F32)

    @pl.when(j <= COL_KA)
    def _():
        w = hnw_ref[pl.ds(j, 1), :]
        for hd in range(ATT_HEADS):
            sl = slice(hd * ATT_HEAD_DIM, (hd + 1) * ATT_HEAD_DIM)
            a = acc[:, sl]
            r = lax.rsqrt(jnp.mean(a * a, axis=-1, keepdims=True) + EPS)
            p_ref[:, sl] = (a * r * w[:, sl]).astype(BF16)

    @pl.when(j > COL_KA)
    def _():
        p_ref[...] = acc.astype(BF16)


def _inproj_call(x2, ada3, norm_w, w_main, wg, gb, hnw, seq):
    T, D = x2.shape
    tm, tn = PROJ_TM, PROJ_TN
    tiles_per_seq = seq // tm
    return pl.pallas_call(
        _inproj_kernel,
        grid=(T // tm, N_COL_TILES),
        in_specs=[pl.BlockSpec((tm, D), lambda i, j: (i, 0)),
                  pl.BlockSpec((1, 3, D), lambda i, j: (i // tiles_per_seq, 0, 0)),
                  pl.BlockSpec((1, D), lambda i, j: (0, 0)),
                  pl.BlockSpec((D, tn), lambda i, j: (0, j)),
                  pl.BlockSpec((D, LANES), lambda i, j: (0, 0)),
                  pl.BlockSpec((1, LANES), lambda i, j: (0, 0)),
                  pl.BlockSpec((2, tn), lambda i, j: (0, 0))],
        out_specs=[pl.BlockSpec((tm, tn), lambda i, j: (i, j)),
                   pl.BlockSpec((tm, LANES), lambda i, j: (i, 0)),
                   pl.BlockSpec((GATE_COLS, tm), lambda i, j: (0, i))],
        out_shape=[jax.ShapeDtypeStruct((T, N_COL_TILES * tn), BF16),
                   jax.ShapeDtypeStruct((T, LANES), F32),
                   jax.ShapeDtypeStruct((GATE_COLS, T), F32)],
        scratch_shapes=[pltpu.VMEM((tm, D), BF16)],
        compiler_params=pltpu.CompilerParams(dimension_semantics=("arbitrary", "arbitrary"),
                                             vmem_limit_bytes=VMEM_LIMIT),
        name="inproj",
    )(x2, ada3, norm_w, w_main, wg, gb, hnw)


def _moba_kernel(q_ref, k_ref, v_ref, za_ref, bown_ref, bprev_ref, o_ref,
                 kmean_scr, vt_scr, selb_scr, s_scr, m_scr, l_scr, acc_scr):
    qi = pl.program_id(2)
    blk, d = MOBA_BLOCK, ATT_HEAD_DIM
    nb = k_ref.shape[0] // blk
    heads = q_ref.shape[1] // d

    def hsl(hd):
        return slice(hd * d, (hd + 1) * d)

    @pl.when(qi == 0)
    def _():
        for n in range(nb):
            kb = k_ref[n * blk:(n + 1) * blk, :].astype(F32)
            kmean_scr[n:n + 1, :] = jnp.mean(kb, axis=0, keepdims=True)
            for hd in range(heads):
                vt_scr[n, hsl(hd), :] = v_ref[n * blk:(n + 1) * blk, hsl(hd)].T

    def scores(hd, j):
        kj = k_ref[pl.ds(pl.multiple_of(j * blk, blk), blk), hsl(hd)]
        return lax.dot_general(kj, q_ref[:, hsl(hd)], _NT, preferred_element_type=F32)

    def fold(x):
        return x.reshape(blk // SUBLANES, SUBLANES, blk)

    rows = lax.broadcasted_iota(jnp.int32, (nb, blk), 0)
    valid = rows < qi
    jp = jnp.maximum(qi - 1, 0)
    for hd in range(heads):
        gate = lax.dot_general(kmean_scr[:, hsl(hd)], q_ref[:, hsl(hd)].astype(F32), _NT,
                               precision=lax.Precision.HIGHEST, preferred_element_type=F32)
        g = jnp.where(valid, gate, -jnp.inf)
        selb = jnp.full(gate.shape, NEG_INF, F32)
        for _ in range(MOBA_TOPK):
            mx = jnp.max(g, axis=0, keepdims=True)
            first = jnp.min(jnp.where(g == mx, rows, nb), axis=0, keepdims=True)
            pick = rows == first
            selb = jnp.where(pick, 0.0, selb)
            g = jnp.where(pick, -jnp.inf, g)
        selb_scr[hd] = jnp.where(valid, selb, NEG_INF)

        s_own = scores(hd, qi) + bown_ref[hd]
        s_prev = scores(hd, jp) + bprev_ref[hd] + selb_scr[hd, pl.ds(jp, 1), :]
        s_scr[hd, 0] = s_own
        s_scr[hd, 1] = s_prev
        m_scr[hd] = jnp.maximum(jnp.max(fold(s_own), axis=0), jnp.max(fold(s_prev), axis=0))

    n_far = jnp.maximum(qi - 1, 0)

    def far_scores(j, carry):
        for hd in range(heads):
            s = scores(hd, j) + selb_scr[hd, pl.ds(j, 1), :]
            s_scr[hd, j + 2] = s
            m_scr[hd] = jnp.maximum(m_scr[hd], jnp.max(fold(s), axis=0))
        return carry

    lax.fori_loop(0, n_far, far_scores, 0)

    for hd in range(heads):
        m = jnp.max(m_scr[hd], axis=0, keepdims=True)
        m_scr[hd] = jnp.broadcast_to(m, (SUBLANES, blk))
        l_scr[hd] = jnp.zeros((SUBLANES, blk), F32)
        acc_scr[hd] = jnp.zeros((d, blk), F32)

    def accumulate(t, carry):
        kb = jnp.where(t == 0, qi, jnp.where(t == 1, jp, t - 2))
        for hd in range(heads):
            p = jnp.exp2(fold(s_scr[hd, t]) - m_scr[hd])
            l_scr[hd] += jnp.sum(p, axis=0)
            acc_scr[hd] += jnp.dot(vt_scr[kb, hsl(hd), :], p.reshape(blk, blk).astype(BF16),
                                   preferred_element_type=F32)
        return carry

    lax.fori_loop(0, n_far + 2, accumulate, 0)

    for hd in range(heads):
        l = jnp.sum(l_scr[hd], axis=0, keepdims=True)
        y = (acc_scr[hd] * (1.0 / l)).T
        za = za_ref[:, hsl(hd)].astype(F32)
        o_ref[:, hsl(hd)] = (y * (za * _sigmoid(za))).astype(BF16)


def _moba_call(p, bias_own, bias_prev, batch, seq):
    T = p.shape[0]
    blk, d = MOBA_BLOCK, ATT_HEAD_DIM
    nq = seq // blk
    hg = MOBA_HEAD_GROUP
    w = hg * d
    gpt = PROJ_TN // w
    return pl.pallas_call(
        _moba_kernel,
        grid=(batch, ATT_HEADS // hg, nq),
        in_specs=[pl.BlockSpec((blk, w), lambda b, h, i: (b * nq + i, COL_QA * gpt + h)),
                  pl.BlockSpec((seq, w), lambda b, h, i: (b, COL_KA * gpt + h)),
                  pl.BlockSpec((seq, w), lambda b, h, i: (b, COL_VA * gpt + h)),
                  pl.BlockSpec((blk, w), lambda b, h, i: (b * nq + i, COL_ZA * gpt + h)),
                  pl.BlockSpec((hg, blk, blk), lambda b, h, i: (h, 0, 0)),
                  pl.BlockSpec((hg, blk, blk), lambda b, h, i: (h, 0, 0))],
        out_specs=pl.BlockSpec((blk, w), lambda b, h, i: (b * nq + i, h)),
        out_shape=jax.ShapeDtypeStruct((T, ATT_HEADS * d), BF16),
        scratch_shapes=[pltpu.VMEM((nq, w), F32),
                        pltpu.VMEM((nq, w, blk), BF16),
                        pltpu.VMEM((hg, nq, blk), F32),
                        pltpu.VMEM((hg, nq, blk, blk), F32),
                        pltpu.VMEM((hg, SUBLANES, blk), F32),
                        pltpu.VMEM((hg, SUBLANES, blk), F32),
                        pltpu.VMEM((hg, d, blk), F32)],
        compiler_params=pltpu.CompilerParams(
            dimension_semantics=("arbitrary", "arbitrary", "arbitrary"),
            vmem_limit_bytes=VMEM_LIMIT),
        name="moba",
    )(p, p, p, p, bias_own, bias_prev)


def _mlstm_kernel(qm_ref, km_ref, vm_ref, zm_ref, om_ref, g_ref, gt_ref, cw_ref, cb_ref, mlw_ref,
                  o_ref, c_scr, n_scr, m_scr, extq_scr, extk_scr):
    c_idx = pl.program_id(1)
    L = qm_ref.shape[0]
    W = ML_HEADS * ML_HEAD_DIM
    pad = SUBLANES

    @pl.when(c_idx == 0)
    def _():
        c_scr[...] = jnp.zeros_like(c_scr)
        n_scr[...] = jnp.zeros_like(n_scr)
        m_scr[...] = jnp.zeros_like(m_scr)
        extq_scr[0:pad, :] = jnp.zeros((pad, W), F32)
        extk_scr[0:pad, :] = jnp.zeros((pad, W), F32)

    def conv_silu(src_ref, ext_scr, col0):
        ext_scr[pad:pad + L, :] = src_ref[...].astype(F32)
        y = cb_ref[:, col0:col0 + W]
        for w in range(CONV_WIDTH):
            off = pad - (CONV_WIDTH - 1) + w
            y = y + cw_ref[w:w + 1, col0:col0 + W] * ext_scr[off:off + L, :]
        ext_scr[0:pad, :] = ext_scr[L:L + pad, :]
        return y * _sigmoid(y)

    qc = conv_silu(qm_ref, extq_scr, 0)
    kc = conv_silu(km_ref, extk_scr, W) * (ML_HEAD_DIM ** -0.5)

    r_i = lax.broadcasted_iota(jnp.int32, (L, L), 0)
    c_i = lax.broadcasted_iota(jnp.int32, (L, L), 1)
    causal = c_i <= r_i
    ltri = causal.astype(F32)
    utri = (r_i <= c_i).astype(F32)
    g_col = g_ref[...]
    g_row = gt_ref[...]
    b_col_all = jnp.dot(ltri, _log_sigmoid(g_col), precision=lax.Precision.HIGHEST,
                        preferred_element_type=F32)
    b_row_all = jnp.dot(_log_sigmoid(g_row), utri, precision=lax.Precision.HIGHEST,
                        preferred_element_type=F32)

    for hh in range(ML_HEADS):
        sl = slice(hh * ML_HEAD_DIM, (hh + 1) * ML_HEAD_DIM)
        q = qc[:, sl]
        k = kc[:, sl]
        v = vm_ref[:, sl]
        b_col = b_col_all[:, ML_HEADS + hh:ML_HEADS + hh + 1]
        a_col = g_col[:, hh:hh + 1] - b_col
        a_row = g_row[hh:hh + 1, :] - b_row_all[ML_HEADS + hh:ML_HEADS + hh + 1, :]
        m_prev = m_scr[hh][0:1, 0:1]

        a_mask = jnp.where(causal, a_row, -jnp.inf)
        gcol = jnp.maximum(m_prev, jnp.max(a_mask, axis=1, keepdims=True))
        d_mat = jnp.exp(a_mask - gcol)
        inter = jnp.exp(m_prev - gcol)
        g_last = jnp.max(gcol, axis=0, keepdims=True)

        qb = q.astype(BF16)
        s = lax.dot_general(qb, k.astype(BF16), _NT, preferred_element_type=F32) * d_mat
        c_old = c_scr[hh]
        n_old = n_scr[hh]
        num = (inter * jnp.dot(qb, c_old.astype(BF16), preferred_element_type=F32)
               + jnp.dot(s.astype(BF16), v, preferred_element_type=F32))
        den = inter * jnp.sum(q * n_old, axis=1, keepdims=True) + jnp.sum(s, axis=1, keepdims=True)
        h = num / jnp.maximum(jnp.abs(den), jnp.exp(-(b_col + gcol)))

        decay = jnp.exp(m_prev - g_last)
        kw = k * jnp.exp(a_col - g_last)
        c_scr[hh] = decay * c_old + lax.dot_general(kw.astype(BF16), v, _TN,
                                                    preferred_element_type=F32)
        n_scr[hh] = decay * n_old + jnp.sum(kw, axis=0, keepdims=True)
        m_scr[hh] = jnp.broadcast_to(b_col[L - 1:L, :] + g_last, m_scr.shape[1:])

        hg = _sigmoid(om_ref[:, sl].astype(F32)) * h
        hn = hg * lax.rsqrt(jnp.mean(hg * hg, axis=-1, keepdims=True) + EPS) * mlw_ref[:, sl]
        z = zm_ref[:, sl].astype(F32)
        o_ref[:, sl] = (hn * (z * _sigmoid(z))).astype(BF16)


def _mlstm_call(p, g, gt, conv_w, conv_b, ml_norm_w, batch, seq):
    T = p.shape[0]
    L = ML_CHUNK
    nc = seq // L
    W = ML_HEADS * ML_HEAD_DIM
    row = lambda b, c: b * nc + c
    pspec = lambda col: pl.BlockSpec((L, W), lambda b, c: (row(b, c), col))
    return pl.pallas_call(
        _mlstm_kernel,
        grid=(batch, nc),
        in_specs=[pspec(COL_QM), pspec(COL_KM), pspec(COL_VM), pspec(COL_ZM), pspec(COL_OM),
                  pl.BlockSpec((L, LANES), lambda b, c: (row(b, c), 0)),
                  pl.BlockSpec((GATE_COLS, L), lambda b, c: (0, row(b, c))),
                  pl.BlockSpec((CONV_WIDTH, 2 * W), lambda b, c: (0, 0)),
                  pl.BlockSpec((1, 2 * W), lambda b, c: (0, 0)),
                  pl.BlockSpec((1, W), lambda b, c: (0, 0))],
        out_specs=pl.BlockSpec((L, W), lambda b, c: (row(b, c), 0)),
        out_shape=jax.ShapeDtypeStruct((T, W), BF16),
        scratch_shapes=[pltpu.VMEM((ML_HEADS, ML_HEAD_DIM, ML_HEAD_DIM), F32),
                        pltpu.VMEM((ML_HEADS, 1, ML_HEAD_DIM), F32),
                        pltpu.VMEM((ML_HEADS, SUBLANES, LANES), F32),
                        pltpu.VMEM((L + SUBLANES, W), F32),
                        pltpu.VMEM((L + SUBLANES, W), F32)],
        compiler_params=pltpu.CompilerParams(dimension_semantics=("arbitrary", "arbitrary"),
                                             vmem_limit_bytes=VMEM_LIMIT),
        name="mlstm",
    )(p, p, p, p, p, g, gt, conv_w, conv_b, ml_norm_w)


def _outp_kernel(ua_ref, um_ref, ga_ref, gm_ref, x_ref, ada_ref, wa_ref, wm_ref, wo_ref, o_ref):
    ya = jnp.dot(ua_ref[...], wa_ref[...], preferred_element_type=F32)
    ym = jnp.dot(um_ref[...], wm_ref[...], preferred_element_type=F32)
    y = _sigmoid(ga_ref[...].astype(F32)) * ya + _sigmoid(gm_ref[...].astype(F32)) * ym
    gate = ada_ref[0][2:3, :]
    o_ref[...] = x_ref[...] + gate * jnp.dot(y.astype(BF16), wo_ref[...], preferred_element_type=F32)


def _outp_call(ua, um, p, x2, ada3, wa, wm, wo, seq):
    T, D = x2.shape
    tm = OUT_TM
    tiles_per_seq = seq // tm
    act = lambda col: pl.BlockSpec((tm, D), lambda i: (i, col))
    wspec = pl.BlockSpec((D, D), lambda i: (0, 0))
    return pl.pallas_call(
        _outp_kernel,
        grid=(T // tm,),
        in_specs=[act(0), act(0), act(COL_GA), act(COL_GM), act(0),
                  pl.BlockSpec((1, 3, D), lambda i: (i // tiles_per_seq, 0, 0)),
                  wspec, wspec, wspec],
        out_specs=pl.BlockSpec((tm, D), lambda i: (i, 0)),
        out_shape=jax.ShapeDtypeStruct((T, D), F32),
        compiler_params=pltpu.CompilerParams(dimension_semantics=("arbitrary",),
                                             vmem_limit_bytes=VMEM_LIMIT),
        name="outp",
    )(ua, um, p, p, x2, ada3, wa, wm, wo)


def _rel_bias_tables(rel_bias):
    max_exact = REL_BUCKETS // 2
    rb = rel_bias.astype(F32)
    rb = (rb - rb[REL_BUCKETS - 1:, :]) * LOG2E
    key = jnp.arange(MOBA_BLOCK)[:, None]
    qry = jnp.arange(MOBA_BLOCK)[None, :]
    d_own = qry - key

    def tile(dist):
        nf = jnp.maximum(dist, 1).astype(F32)
        large = max_exact + (jnp.log(nf / max_exact) / math.log(REL_MAX_DIST / max_exact)
                             * (REL_BUCKETS - max_exact)).astype(jnp.int32)
        bucket = jnp.where(dist < max_exact, dist, jnp.minimum(large, REL_BUCKETS - 1))
        onehot = (bucket[..., None] == jnp.arange(REL_BUCKETS)).astype(F32)
        return jnp.einsum('kqb,bh->hkq', onehot, rb, precision=lax.Precision.HIGHEST)

    own = jnp.where(d_own >= 0, tile(jnp.maximum(d_own, 0)), NEG_INF)
    prev = tile(d_own + MOBA_BLOCK)
    return own, prev


def _layer(x2, c, w_ada, b_ada, norm_w, w_in, q_norm_w, k_norm_w, rel_bias, conv_w, conv_b,
           b_igate, b_fgate, ml_norm_w, w_att_proj, w_ml_proj, w_out, batch, seq):
    D = D_MODEL
    g0 = 4 * D + 5 * D
    ada3 = _ada_call(c, w_ada, b_ada).reshape(batch, 3, D)

    w_main = jnp.concatenate([w_in[:, :g0], w_in[:, g0 + GATE_COLS:]], axis=1).astype(BF16)
    wg = jnp.pad(w_in[:, g0:g0 + GATE_COLS], ((0, 0), (0, LANES - GATE_COLS)))
    gb = jnp.pad(jnp.concatenate([b_igate, b_fgate]), (0, LANES - GATE_COLS)).reshape(1, LANES)
    q_scale = ATT_HEAD_DIM ** -0.5 * LOG2E
    hnw = jnp.stack([jnp.tile(q_norm_w.astype(F32), ATT_HEADS) * q_scale,
                     jnp.tile(k_norm_w.astype(F32), ATT_HEADS)])

    p, g, gt = _inproj_call(x2, ada3, norm_w.reshape(1, D), w_main, wg, gb, hnw, seq)
    bias_own, bias_prev = _rel_bias_tables(rel_bias)
    ua = _moba_call(p, bias_own, bias_prev, batch, seq)
    um = _mlstm_call(p, g, gt, conv_w, conv_b.reshape(1, -1), ml_norm_w.reshape(1, -1), batch, seq)
    return _outp_call(ua, um, p, x2, ada3, w_att_proj.astype(BF16), w_ml_proj.astype(BF16),
                      w_out.astype(BF16), seq)


def kernel(x, c, w_ada, b_ada, norm_w, w_in, q_norm_w, k_norm_w, rel_bias, conv_w, conv_b, b_igate,
           b_fgate, ml_norm_w, w_att_proj, w_ml_proj, w_out):
    batch, seq, D = x.shape
    assert D == D_MODEL and seq % PROJ_TM == 0 and seq % MOBA_BLOCK == 0 and seq % ML_CHUNK == 0
    x2 = x.reshape(batch * seq, D)
    for l in range(w_in.shape[0]):
        x2 = _layer(x2, c, w_ada[l], b_ada[l], norm_w[l], w_in[l], q_norm_w[l], k_norm_w[l],
                    rel_bias, conv_w[l], conv_b[l], b_igate[l], b_fgate[l], ml_norm_w[l],
                    w_att_proj[l], w_ml_proj[l], w_out[l], batch, seq)
    return x2.reshape(batch, seq, D)
```

```python
import functools
import math

import jax
import jax.numpy as jnp
import numpy as np
from jax import lax
from jax.experimental import pallas as pl
from jax.experimental.pallas import tpu as pltpu

F32 = jnp.float32
BF16 = jnp.bfloat16

D_MODEL = 1024
ATT_HEADS = 8
ATT_HEAD_DIM = D_MODEL // ATT_HEADS
MOBA_BLOCK = 256
MOBA_TOPK = 3
REL_BUCKETS = 32
REL_MAX_DIST = 128
ML_HEADS = 4
ML_HEAD_DIM = D_MODEL // ML_HEADS
CONV_WIDTH = 4
EPS = 1e-6
NEG_INF = -1e30
LOG2E = math.log2(math.e)

LANES = 128
SUBLANES = 8
GATE_COLS = 2 * ML_HEADS
ML_CHUNK = 256
PROJ_TM = 1024
PROJ_TN = 1024
OUT_TM = 512
MOBA_HEAD_GROUP = 4
VMEM_LIMIT = 52 * 1024 * 1024

COL_QA, COL_KA, COL_VA, COL_ZA, COL_QM, COL_KM, COL_VM, COL_ZM, COL_OM, COL_GA, COL_GM = range(11)
N_COL_TILES = 11

_NT = (((1,), (1,)), ((), ()))
_TN = (((0,), (0,)), ((), ()))


def _sigmoid(x):
    return 1.0 / (1.0 + jnp.exp(-x))


def _log_sigmoid(x):
    return jnp.minimum(x, 0.0) - jnp.log1p(jnp.exp(-jnp.abs(x)))


def _ada_kernel(c_ref, w_ref, b_ref, o_ref):
    o_ref[...] = jnp.dot(c_ref[...], w_ref[...], precision=lax.Precision.HIGHEST,
                         preferred_element_type=F32) + b_ref[...]


def _ada_call(c, w_ada, b_ada):
    B, D = c.shape
    return pl.pallas_call(
        _ada_kernel,
        grid=(3,),
        in_specs=[pl.BlockSpec((B, D), lambda j: (0, 0)),
                  pl.BlockSpec((D, D), lambda j: (0, j)),
                  pl.BlockSpec((1, D), lambda j: (0, j))],
        out_specs=pl.BlockSpec((B, D), lambda j: (0, j)),
        out_shape=jax.ShapeDtypeStruct((B, 3 * D), F32),
        compiler_params=pltpu.CompilerParams(dimension_semantics=("arbitrary",),
                                             vmem_limit_bytes=VMEM_LIMIT),
        name="ada",
    )(c, w_ada, b_ada.reshape(1, 3 * D))


def _inproj_kernel(x_ref, ada_ref, nw_ref, w_ref, wg_ref, gb_ref, hnw_ref,
                   p_ref, g_ref, gt_ref, h_scr):
    j = pl.program_id(1)

    @pl.when(j == 0)
    def _():
        x = x_ref[...]
        ada = ada_ref[0]
        y = x * lax.rsqrt(jnp.mean(x * x, axis=-1, keepdims=True) + EPS) * nw_ref[...]
        h = y * (1.0 + ada[1:2, :]) + ada[0:1, :]
        h_scr[...] = h.astype(BF16)
        g = jnp.dot(h, wg_ref[...], precision=lax.Precision.HIGHEST,
                    preferred_element_type=F32) + gb_ref[...]
        g_ref[...] = g
        gt_ref[...] = g.T[:GATE_COLS, :]

    acc = jnp.dot(h_scr[...], w_ref[...], preferred_element_type=F32)

    @pl.when(j <= COL_KA)
    def _():
        w = hnw_ref[pl.ds(j, 1), :]
        for hd in range(ATT_HEADS):
            sl = slice(hd * ATT_HEAD_DIM, (hd + 1) * ATT_HEAD_DIM)
            a = acc[:, sl]
            r = lax.rsqrt(jnp.mean(a * a, axis=-1, keepdims=True) + EPS)
            p_ref[:, sl] = (a * r * w[:, sl]).astype(BF16)

    @pl.when(j > COL_KA)
    def _():
        p_ref[...] = acc.astype(BF16)


def _inproj_call(x2, ada3, norm_w, w_main, wg, gb, hnw, seq):
    T, D = x2.shape
    tm, tn = PROJ_TM, PROJ_TN
    tiles_per_seq = seq // tm
    return pl.pallas_call(
        _inproj_kernel,
        grid=(T // tm, N_COL_TILES),
        in_specs=[pl.BlockSpec((tm, D), lambda i, j: (i, 0)),
                  pl.BlockSpec((1, 3, D), lambda i, j: (i // tiles_per_seq, 0, 0)),
                  pl.BlockSpec((1, D), lambda i, j: (0, 0)),
                  pl.BlockSpec((D, tn), lambda i, j: (0, j)),
                  pl.BlockSpec((D, LANES), lambda i, j: (0, 0)),
                  pl.BlockSpec((1, LANES), lambda i, j: (0, 0)),
                  pl.BlockSpec((2, tn), lambda i, j: (0, 0))],
        out_specs=[pl.BlockSpec((tm, tn), lambda i, j: (i, j)),
                   pl.BlockSpec((tm, LANES), lambda i, j: (i, 0)),
                   pl.BlockSpec((GATE_COLS, tm), lambda i, j: (0, i))],
        out_shape=[jax.ShapeDtypeStruct((T, N_COL_TILES * tn), BF16),
                   jax.ShapeDtypeStruct((T, LANES), F32),
                   jax.ShapeDtypeStruct((GATE_COLS, T), F32)],
        scratch_shapes=[pltpu.VMEM((tm, D), BF16)],
        compiler_params=pltpu.CompilerParams(dimension_semantics=("arbitrary", "arbitrary"),
                                             vmem_limit_bytes=VMEM_LIMIT),
        name="inproj",
    )(x2, ada3, norm_w, w_main, wg, gb, hnw)


def _moba_kernel(q_ref, k_ref, v_ref, za_ref, bown_ref, bprev_ref, o_ref,
                 kmean_scr, vt_scr, selb_scr, s_scr, m_scr, l_scr, acc_scr):
    qi = pl.program_id(2)
    blk, d = MOBA_BLOCK, ATT_HEAD_DIM
    nb = k_ref.shape[0] // blk
    heads = q_ref.shape[1] // d

    def hsl(hd):
        return slice(hd * d, (hd + 1) * d)

    @pl.when(qi == 0)
    def _():
        for n in range(nb):
            kb = k_ref[n * blk:(n + 1) * blk, :].astype(F32)
            kmean_scr[n:n + 1, :] = jnp.mean(kb, axis=0, keepdims=True)
            for hd in range(heads):
                vt_scr[n, hsl(hd), :] = v_ref[n * blk:(n + 1) * blk, hsl(hd)].T

    def scores(hd, j):
        kj = k_ref[pl.ds(pl.multiple_of(j * blk, blk), blk), hsl(hd)]
        return lax.dot_general(kj, q_ref[:, hsl(hd)], _NT, preferred_element_type=F32)

    def fold(x):
        return x.reshape(blk // SUBLANES, SUBLANES, blk)

    rows = lax.broadcasted_iota(jnp.int32, (nb, blk), 0)
    valid = rows < qi
    jp = jnp.maximum(qi - 1, 0)
    for hd in range(heads):
        gate = lax.dot_general(kmean_scr[:, hsl(hd)], q_ref[:, hsl(hd)].astype(F32), _NT,
                               precision=lax.Precision.HIGHEST, preferred_element_type=F32)
        g = jnp.where(valid, gate, -jnp.inf)
        selb = jnp.full(gate.shape, NEG_INF, F32)
        for _ in range(MOBA_TOPK):
            mx = jnp.max(g, axis=0, keepdims=True)
            first = jnp.min(jnp.where(g == mx, rows, nb), axis=0, keepdims=True)
            pick = rows == first
            selb = jnp.where(pick, 0.0, selb)
            g = jnp.where(pick, -jnp.inf, g)
        selb_scr[hd] = jnp.where(valid, selb, NEG_INF)

        s_own = scores(hd, qi) + bown_ref[hd]
        s_prev = scores(hd, jp) + bprev_ref[hd] + selb_scr[hd, pl.ds(jp, 1), :]
        s_scr[hd, 0] = s_own
        s_scr[hd, 1] = s_prev
        m_scr[hd] = jnp.maximum(jnp.max(fold(s_own), axis=0), jnp.max(fold(s_prev), axis=0))

    n_far = jnp.maximum(qi - 1, 0)

    def far_scores(j, carry):
        for hd in range(heads):
            s = scores(hd, j) + selb_scr[hd, pl.ds(j, 1), :]
            s_scr[hd, j + 2] = s
            m_scr[hd] = jnp.maximum(m_scr[hd], jnp.max(fold(s), axis=0))
        return carry

    lax.fori_loop(0, n_far, far_scores, 0)

    for hd in range(heads):
        m = jnp.max(m_scr[hd], axis=0, keepdims=True)
        m_scr[hd] = jnp.broadcast_to(m, (SUBLANES, blk))
        l_scr[hd] = jnp.zeros((SUBLANES, blk), F32)
        acc_scr[hd] = jnp.zeros((d, blk), F32)

    def accumulate(t, carry):
        kb = jnp.where(t == 0, qi, jnp.where(t == 1, jp, t - 2))
        for hd in range(heads):
            p = jnp.exp2(fold(s_scr[hd, t]) - m_scr[hd])
            l_scr[hd] += jnp.sum(p, axis=0)
            acc_scr[hd] += jnp.dot(vt_scr[kb, hsl(hd), :], p.reshape(blk, blk).astype(BF16),
                                   preferred_element_type=F32)
        return carry

    lax.fori_loop(0, n_far + 2, accumulate, 0)

    for hd in range(heads):
        l = jnp.sum(l_scr[hd], axis=0, keepdims=True)
        y = (acc_scr[hd] * (1.0 / l)).T
        za = za_ref[:, hsl(hd)].astype(F32)
        o_ref[:, hsl(hd)] = (y * (za * _sigmoid(za))).astype(BF16)


def _moba_call(p, bias_own, bias_prev, batch, seq):
    T = p.shape[0]
    blk, d = MOBA_BLOCK, ATT_HEAD_DIM
    nq = seq // blk
    hg = MOBA_HEAD_GROUP
    w = hg * d
    gpt = PROJ_TN // w
    return pl.pallas_call(
        _moba_kernel,
        grid=(batch, ATT_HEADS // hg, nq),
        in_specs=[pl.BlockSpec((blk, w), lambda b, h, i: (b * nq + i, COL_QA * gpt + h)),
                  pl.BlockSpec((seq, w), lambda b, h, i: (b, COL_KA * gpt + h)),
                  pl.BlockSpec((seq, w), lambda b, h, i: (b, COL_VA * gpt + h)),
                  pl.BlockSpec((blk, w), lambda b, h, i: (b * nq + i, COL_ZA * gpt + h)),
                  pl.BlockSpec((hg, blk, blk), lambda b, h, i: (h, 0, 0)),
                  pl.BlockSpec((hg, blk, blk), lambda b, h, i: (h, 0, 0))],
        out_specs=pl.BlockSpec((blk, w), lambda b, h, i: (b * nq + i, h)),
        out_shape=jax.ShapeDtypeStruct((T, ATT_HEADS * d), BF16),
        scratch_shapes=[pltpu.VMEM((nq, w), F32),
                        pltpu.VMEM((nq, w, blk), BF16),
                        pltpu.VMEM((hg, nq, blk), F32),
                        pltpu.VMEM((hg, nq, blk, blk), F32),
                        pltpu.VMEM((hg, SUBLANES, blk), F32),
                        pltpu.VMEM((hg, SUBLANES, blk), F32),
                        pltpu.VMEM((hg, d, blk), F32)],
        compiler_params=pltpu.CompilerParams(
            dimension_semantics=("arbitrary", "arbitrary", "arbitrary"),
            vmem_limit_bytes=VMEM_LIMIT),
        name="moba",
    )(p, p, p, p, bias_own, bias_prev)


def _mlstm_kernel(qm_ref, km_ref, vm_ref, zm_ref, om_ref, g_ref, gt_ref, cw_ref, cb_ref, mlw_ref,
                  o_ref, c_scr, n_scr, m_scr, tailq_scr, tailk_scr, shift_scr):
    c_idx = pl.program_id(1)
    L = qm_ref.shape[0]
    W = ML_HEADS * ML_HEAD_DIM
    pad = SUBLANES

    r_i = lax.broadcasted_iota(jnp.int32, (L, L), 0)
    c_i = lax.broadcasted_iota(jnp.int32, (L, L), 1)

    @pl.when(c_idx == 0)
    def _():
        c_scr[...] = jnp.zeros_like(c_scr)
        n_scr[...] = jnp.zeros_like(n_scr)
        m_scr[...] = jnp.zeros_like(m_scr)
        tailq_scr[...] = jnp.zeros_like(tailq_scr)
        tailk_scr[...] = jnp.zeros_like(tailk_scr)
        for k in range(CONV_WIDTH):
            shift_scr[k] = (c_i == r_i - k).astype(F32).astype(BF16)

    row8 = lax.broadcasted_iota(jnp.int32, (pad, W), 0)

    def conv_silu(src_ref, tail_scr, col0):
        x = src_ref[...]
        tail = tail_scr[...]
        y = cb_ref[:, col0:col0 + W]
        fix = jnp.zeros((pad, W), F32)
        for w in range(CONV_WIDTH):
            k = CONV_WIDTH - 1 - w
            cw = cw_ref[w:w + 1, col0:col0 + W]
            y = y + cw * jnp.dot(shift_scr[k], x, preferred_element_type=F32)
            if k:
                fix = fix + cw * jnp.where(row8 < k, pltpu.roll(tail, k, axis=0), 0.0)
        tail_scr[...] = x[L - 2 * pad:, :].astype(F32)[pad:, :]
        y = jnp.concatenate([y[:pad] + fix, y[pad:]], axis=0)
        return y * _sigmoid(y)

    qc = conv_silu(qm_ref, tailq_scr, 0)
    kc = conv_silu(km_ref, tailk_scr, W) * (ML_HEAD_DIM ** -0.5)

    causal = c_i <= r_i
    ltri = causal.astype(F32)
    utri = (r_i <= c_i).astype(F32)
    g_col = g_ref[...]
    g_row = gt_ref[...]
    b_col_all = jnp.dot(ltri, _log_sigmoid(g_col), precision=lax.Precision.HIGHEST,
                        preferred_element_type=F32)
    b_row_all = jnp.dot(_log_sigmoid(g_row), utri, precision=lax.Precision.HIGHEST,
                        preferred_element_type=F32)

    for hh in range(ML_HEADS):
        sl = slice(hh * ML_HEAD_DIM, (hh + 1) * ML_HEAD_DIM)
        q = qc[:, sl]
        k = kc[:, sl]
        v = vm_ref[:, sl]
        b_col = b_col_all[:, ML_HEADS + hh:ML_HEADS + hh + 1]
        a_col = g_col[:, hh:hh + 1] - b_col
        a_row = g_row[hh:hh + 1, :] - b_row_all[ML_HEADS + hh:ML_HEADS + hh + 1, :]
        m_prev = m_scr[hh][0:1, 0:1]

        a_mask = jnp.where(causal, a_row, -jnp.inf)
        gcol = jnp.maximum(m_prev, jnp.max(a_mask, axis=1, keepdims=True))
        d_mat = jnp.exp(a_mask - gcol)
        inter = jnp.exp(m_prev - gcol)
        g_last = jnp.max(gcol, axis=0, keepdims=True)

        qb = q.astype(BF16)
        s = lax.dot_general(qb, k.astype(BF16), _NT, preferred_element_type=F32) * d_mat
        c_old = c_scr[hh]
        n_old = n_scr[hh]
        num = (inter * jnp.dot(qb, c_old.astype(BF16), preferred_element_type=F32)
               + jnp.dot(s.astype(BF16), v, preferred_element_type=F32))
        den = inter * jnp.sum(q * n_old, axis=1, keepdims=True) + jnp.sum(s, axis=1, keepdims=True)
        h = num / jnp.maximum(jnp.abs(den), jnp.exp(-(b_col + gcol)))

        decay = jnp.exp(m_prev - g_last)
        kw = k * jnp.exp(a_col - g_last)
        c_scr[hh] = decay * c_old + lax.dot_general(kw.astype(BF16), v, _TN,
                                                    preferred_element_type=F32)
        n_scr[hh] = decay * n_old + jnp.sum(kw, axis=0, keepdims=True)
        m_scr[hh] = jnp.broadcast_to(b_col[L - 1:L, :] + g_last, m_scr.shape[1:])

        hg = _sigmoid(om_ref[:, sl].astype(F32)) * h
        hn = hg * lax.rsqrt(jnp.mean(hg * hg, axis=-1, keepdims=True) + EPS) * mlw_ref[:, sl]
        z = zm_ref[:, sl].astype(F32)
        o_ref[:, sl] = (hn * (z * _sigmoid(z))).astype(BF16)


def _mlstm_call(p, g, gt, conv_w, conv_b, ml_norm_w, batch, seq):
    T = p.shape[0]
    L = ML_CHUNK
    nc = seq // L
    W = ML_HEADS * ML_HEAD_DIM
    row = lambda b, c: b * nc + c
    pspec = lambda col: pl.BlockSpec((L, W), lambda b, c: (row(b, c), col))
    return pl.pallas_call(
        _mlstm_kernel,
        grid=(batch, nc),
        in_specs=[pspec(COL_QM), pspec(COL_KM), pspec(COL_VM), pspec(COL_ZM), pspec(COL_OM),
                  pl.BlockSpec((L, LANES), lambda b, c: (row(b, c), 0)),
                  pl.BlockSpec((GATE_COLS, L), lambda b, c: (0, row(b, c))),
                  pl.BlockSpec((CONV_WIDTH, 2 * W), lambda b, c: (0, 0)),
                  pl.BlockSpec((1, 2 * W), lambda b, c: (0, 0)),
                  pl.BlockSpec((1, W), lambda b, c: (0, 0))],
        out_specs=pl.BlockSpec((L, W), lambda b, c: (row(b, c), 0)),
        out_shape=jax.ShapeDtypeStruct((T, W), BF16),
        scratch_shapes=[pltpu.VMEM((ML_HEADS, ML_HEAD_DIM, ML_HEAD_DIM), F32),
                        pltpu.VMEM((ML_HEADS, 1, ML_HEAD_DIM), F32),
                        pltpu.VMEM((ML_HEADS, SUBLANES, LANES), F32),
                        pltpu.VMEM((SUBLANES, W), F32),
                        pltpu.VMEM((SUBLANES, W), F32),
                        pltpu.VMEM((CONV_WIDTH, L, L), BF16)],
        compiler_params=pltpu.CompilerParams(dimension_semantics=("arbitrary", "arbitrary"),
                                             vmem_limit_bytes=VMEM_LIMIT),
        name="mlstm",
    )(p, p, p, p, p, g, gt, conv_w, conv_b, ml_norm_w)


def _outp_kernel(ua_ref, um_ref, ga_ref, gm_ref, x_ref, ada_ref, wa_ref, wm_ref, wo_ref, o_ref):
    ya = jnp.dot(ua_ref[...], wa_ref[...], preferred_element_type=F32)
    ym = jnp.dot(um_ref[...], wm_ref[...], preferred_element_type=F32)
    y = _sigmoid(ga_ref[...].astype(F32)) * ya + _sigmoid(gm_ref[...].astype(F32)) * ym
    gate = ada_ref[0][2:3, :]
    o_ref[...] = x_ref[...] + gate * jnp.dot(y.astype(BF16), wo_ref[...], preferred_element_type=F32)


def _outp_call(ua, um, p, x2, ada3, wa, wm, wo, seq):
    T, D = x2.shape
    tm = OUT_TM
    tiles_per_seq = seq // tm
    act = lambda col: pl.BlockSpec((tm, D), lambda i: (i, col))
    wspec = pl.BlockSpec((D, D), lambda i: (0, 0))
    return pl.pallas_call(
        _outp_kernel,
        grid=(T // tm,),
        in_specs=[act(0), act(0), act(COL_GA), act(COL_GM), act(0),
                  pl.BlockSpec((1, 3, D), lambda i: (i // tiles_per_seq, 0, 0)),
                  wspec, wspec, wspec],
        out_specs=pl.BlockSpec((tm, D), lambda i: (i, 0)),
        out_shape=jax.ShapeDtypeStruct((T, D), F32),
        compiler_params=pltpu.CompilerParams(dimension_semantics=("arbitrary",),
                                             vmem_limit_bytes=VMEM_LIMIT),
        name="outp",
    )(ua, um, p, p, x2, ada3, wa, wm, wo)


def _rel_bias_tables(rel_bias):
    max_exact = REL_BUCKETS // 2
    rb = rel_bias.astype(F32)
    rb = (rb - rb[REL_BUCKETS - 1:, :]) * LOG2E
    key = jnp.arange(MOBA_BLOCK)[:, None]
    qry = jnp.arange(MOBA_BLOCK)[None, :]
    d_own = qry - key

    def tile(dist):
        nf = jnp.maximum(dist, 1).astype(F32)
        large = max_exact + (jnp.log(nf / max_exact) / math.log(REL_MAX_DIST / max_exact)
                             * (REL_BUCKETS - max_exact)).astype(jnp.int32)
        bucket = jnp.where(dist < max_exact, dist, jnp.minimum(large, REL_BUCKETS - 1))
        onehot = (bucket[..., None] == jnp.arange(REL_BUCKETS)).astype(F32)
        return jnp.einsum('kqb,bh->hkq', onehot, rb, precision=lax.Precision.HIGHEST)

    own = jnp.where(d_own >= 0, tile(jnp.maximum(d_own, 0)), NEG_INF)
    prev = tile(d_own + MOBA_BLOCK)
    return own, prev


def _layer(x2, c, w_ada, b_ada, norm_w, w_in, q_norm_w, k_norm_w, rel_bias, conv_w, conv_b,
           b_igate, b_fgate, ml_norm_w, w_att_proj, w_ml_proj, w_out, batch, seq):
    D = D_MODEL
    g0 = 4 * D + 5 * D
    ada3 = _ada_call(c, w_ada, b_ada).reshape(batch, 3, D)

    w_main = jnp.concatenate([w_in[:, :g0], w_in[:, g0 + GATE_COLS:]], axis=1).astype(BF16)
    wg = jnp.pad(w_in[:, g0:g0 + GATE_COLS], ((0, 0), (0, LANES - GATE_COLS)))
    gb = jnp.pad(jnp.concatenate([b_igate, b_fgate]), (0, LANES - GATE_COLS)).reshape(1, LANES)
    q_scale = ATT_HEAD_DIM ** -0.5 * LOG2E
    hnw = jnp.stack([jnp.tile(q_norm_w.astype(F32), ATT_HEADS) * q_scale,
                     jnp.tile(k_norm_w.astype(F32), ATT_HEADS)])

    p, g, gt = _inproj_call(x2, ada3, norm_w.reshape(1, D), w_main, wg, gb, hnw, seq)
    bias_own, bias_prev = _rel_bias_tables(rel_bias)
    ua = _moba_call(p, bias_own, bias_prev, batch, seq)
    um = _mlstm_call(p, g, gt, conv_w, conv_b.reshape(1, -1), ml_norm_w.reshape(1, -1), batch, seq)
    return _outp_call(ua, um, p, x2, ada3, w_att_proj.astype(BF16), w_ml_proj.astype(BF16),
                      w_out.astype(BF16), seq)


def kernel(x, c, w_ada, b_ada, norm_w, w_in, q_norm_w, k_norm_w, rel_bias, conv_w, conv_b, b_igate,
           b_fgate, ml_norm_w, w_att_proj, w_ml_proj, w_out):
    batch, seq, D = x.shape
    assert D == D_MODEL and seq % PROJ_TM == 0 and seq % MOBA_BLOCK == 0 and seq % ML_CHUNK == 0
    x2 = x.reshape(batch * seq, D)
    for l in range(w_in.shape[0]):
        x2 = _layer(x2, c, w_ada[l], b_ada[l], norm_w[l], w_in[l], q_norm_w[l], k_norm_w[l],
                    rel_bias, conv_w[l], conv_b[l], b_igate[l], b_fgate[l], ml_norm_w[l],
                    w_att_proj[l], w_ml_proj[l], w_out[l], batch, seq)
    return x2.reshape(batch, seq, D)
```

```python
import functools
import math

import jax
import jax.numpy as jnp
import numpy as np
from jax import lax
from jax.experimental import pallas as pl
from jax.experimental.pallas import tpu as pltpu

F32 = jnp.float32
BF16 = jnp.bfloat16

D_MODEL = 1024
ATT_HEADS = 8
ATT_HEAD_DIM = D_MODEL // ATT_HEADS
MOBA_BLOCK = 256
MOBA_TOPK = 3
REL_BUCKETS = 32
REL_MAX_DIST = 128
ML_HEADS = 4
ML_HEAD_DIM = D_MODEL // ML_HEADS
CONV_WIDTH = 4
EPS = 1e-6
NEG_INF = -1e30
LOG2E = math.log2(math.e)

LANES = 128
SUBLANES = 8
GATE_COLS = 2 * ML_HEADS
ML_CHUNK = 256
PROJ_TM = 1024
PROJ_TN = 1024
OUT_TM = 512
MOBA_HEAD_GROUP = 4
VMEM_LIMIT = 52 * 1024 * 1024

COL_QA, COL_KA, COL_VA, COL_ZA, COL_QM, COL_KM, COL_VM, COL_ZM, COL_OM, COL_GA, COL_GM = range(11)
N_COL_TILES = 11

_NT = (((1,), (1,)), ((), ()))
_TN = (((0,), (0,)), ((), ()))


def _sigmoid(x):
    return 1.0 / (1.0 + jnp.exp(-x))


def _log_sigmoid(x):
    return jnp.minimum(x, 0.0) - jnp.log1p(jnp.exp(-jnp.abs(x)))


def _ada_kernel(c_ref, w_ref, b_ref, o_ref):
    o_ref[...] = jnp.dot(c_ref[...], w_ref[...], precision=lax.Precision.HIGHEST,
                         preferred_element_type=F32) + b_ref[...]


def _ada_call(c, w_ada, b_ada):
    B, D = c.shape
    return pl.pallas_call(
        _ada_kernel,
        grid=(3,),
        in_specs=[pl.BlockSpec((B, D), lambda j: (0, 0)),
                  pl.BlockSpec((D, D), lambda j: (0, j)),
                  pl.BlockSpec((1, D), lambda j: (0, j))],
        out_specs=pl.BlockSpec((B, D), lambda j: (0, j)),
        out_shape=jax.ShapeDtypeStruct((B, 3 * D), F32),
        compiler_params=pltpu.CompilerParams(dimension_semantics=("arbitrary",),
                                             vmem_limit_bytes=VMEM_LIMIT),
        name="ada",
    )(c, w_ada, b_ada.reshape(1, 3 * D))


def _inproj_kernel(x_ref, ada_ref, nw_ref, w_ref, wg_ref, gb_ref, hnw_ref,
                   p_ref, g_ref, gt_ref, h_scr):
    j = pl.program_id(1)

    @pl.when(j == 0)
    def _():
        x = x_ref[...]
        ada = ada_ref[0]
        y = x * lax.rsqrt(jnp.mean(x * x, axis=-1, keepdims=True) + EPS) * nw_ref[...]
        h = y * (1.0 + ada[1:2, :]) + ada[0:1, :]
        h_scr[...] = h.astype(BF16)
        g = jnp.dot(h, wg_ref[...], precision=lax.Precision.HIGHEST,
                    preferred_element_type=F32) + gb_ref[...]
        g_ref[...] = g
        gt_ref[...] = g.T[:GATE_COLS, :]

    acc = jnp.dot(h_scr[...], w_ref[...], preferred_element_type=F32)

    @pl.when(j <= COL_KA)
    def _():
        w = hnw_ref[pl.ds(j, 1), :]
        for hd in range(ATT_HEADS):
            sl = slice(hd * ATT_HEAD_DIM, (hd + 1) * ATT_HEAD_DIM)
            a = acc[:, sl]
            r = lax.rsqrt(jnp.mean(a * a, axis=-1, keepdims=True) + EPS)
            p_ref[:, sl] = (a * r * w[:, sl]).astype(BF16)

    @pl.when(j > COL_KA)
    def _():
        p_ref[...] = acc.astype(BF16)


def _inproj_call(x2, ada3, norm_w, w_main, wg, gb, hnw, seq):
    T, D = x2.shape
    tm, tn = PROJ_TM, PROJ_TN
    tiles_per_seq = seq // tm
    return pl.pallas_call(
        _inproj_kernel,
        grid=(T // tm, N_COL_TILES),
        in_specs=[pl.BlockSpec((tm, D), lambda i, j: (i, 0)),
                  pl.BlockSpec((1, 3, D), lambda i, j: (i // tiles_per_seq, 0, 0)),
                  pl.BlockSpec((1, D), lambda i, j: (0, 0)),
                  pl.BlockSpec((D, tn), lambda i, j: (0, j)),
                  pl.BlockSpec((D, LANES), lambda i, j: (0, 0)),
                  pl.BlockSpec((1, LANES), lambda i, j: (0, 0)),
                  pl.BlockSpec((2, tn), lambda i, j: (0, 0))],
        out_specs=[pl.BlockSpec((tm, tn), lambda i, j: (i, j)),
                   pl.BlockSpec((tm, LANES), lambda i, j: (i, 0)),
                   pl.BlockSpec((GATE_COLS, tm), lambda i, j: (0, i))],
        out_shape=[jax.ShapeDtypeStruct((T, N_COL_TILES * tn), BF16),
                   jax.ShapeDtypeStruct((T, LANES), F32),
                   jax.ShapeDtypeStruct((GATE_COLS, T), F32)],
        scratch_shapes=[pltpu.VMEM((tm, D), BF16)],
        compiler_params=pltpu.CompilerParams(dimension_semantics=("arbitrary", "arbitrary"),
                                             vmem_limit_bytes=VMEM_LIMIT),
        name="inproj",
    )(x2, ada3, norm_w, w_main, wg, gb, hnw)


def _moba_kernel(qa_ref, qb_ref, k_ref, v_ref, zaa_ref, zab_ref, bown_ref, bprev_ref, o_ref,
                 kmean_scr, vt_scr, q2_scr, selb_scr, s_scr, m8_scr):
    step = pl.program_id(2)
    blk, d = MOBA_BLOCK, ATT_HEAD_DIM
    nb = k_ref.shape[0] // blk
    heads = qa_ref.shape[1] // d
    tile = (step, nb - 1 - step)
    n_far = (jnp.maximum(step - 1, 0), nb - 2 - step)
    far_slots = nb - 2
    near_slots = 4

    def hsl(hd):
        return slice(hd * d, (hd + 1) * d)

    @pl.when(step == 0)
    def _():
        for n in range(nb):
            kb = k_ref[n * blk:(n + 1) * blk, :].astype(F32)
            kmean_scr[n:n + 1, :] = jnp.mean(kb, axis=0, keepdims=True)
            for hd in range(heads):
                vt_scr[n, hsl(hd), :] = v_ref[n * blk:(n + 1) * blk, hsl(hd)].T

    q2_scr[0] = qa_ref[...]
    q2_scr[1] = qb_ref[...]

    def fold(x):
        return x.reshape(blk // SUBLANES, SUBLANES, blk)

    rows = lax.broadcasted_iota(jnp.int32, (nb, blk), 0)
    for x in range(2):
        valid = rows < tile[x]
        for hd in range(heads):
            gate = lax.dot_general(kmean_scr[:, hsl(hd)], q2_scr[x, :, hsl(hd)].astype(F32), _NT,
                                   precision=lax.Precision.HIGHEST, preferred_element_type=F32)
            g = jnp.where(valid, gate, -jnp.inf)
            selb = jnp.full(gate.shape, NEG_INF, F32)
            for _ in range(MOBA_TOPK):
                mx = jnp.max(g, axis=0, keepdims=True)
                first = jnp.min(jnp.where(g == mx, rows, nb), axis=0, keepdims=True)
                pick = rows == first
                selb = jnp.where(pick, 0.0, selb)
                g = jnp.where(pick, -jnp.inf, g)
            selb_scr[x, hd] = jnp.where(valid, selb, NEG_INF)

    far = []
    for t in range(far_slots):
        is_a = t < n_far[0]
        x = jnp.where(is_a, 0, 1)
        j = jnp.where(is_a, t, t - n_far[0])
        live = j < jnp.where(is_a, n_far[0], n_far[1])
        far.append((is_a, x, j, live))

    for hd in range(heads):
        buf = hd % s_scr.shape[0]

        def score_item(slot, x, j, add):
            kj = k_ref[pl.ds(pl.multiple_of(j * blk, blk), blk), hsl(hd)]
            s = lax.dot_general(kj, q2_scr[x, :, hsl(hd)], _NT, preferred_element_type=F32) + add
            s_scr[buf, slot] = s
            return jnp.max(fold(s), axis=0)

        m8 = []
        near = []
        for x in range(2):
            jp = jnp.maximum(tile[x] - 1, 0)
            near.append((x, tile[x]))
            near.append((x, jp))
            mo = score_item(2 * x, x, tile[x], bown_ref[hd])
            mp = score_item(2 * x + 1, x, jp, bprev_ref[hd] + selb_scr[x, hd, pl.ds(jp, 1), :])
            m8.append(jnp.maximum(mo, mp))
        for t, (is_a, x, j, live) in enumerate(far):
            add = jnp.where(live, selb_scr[x, hd, pl.ds(j, 1), :], NEG_INF)
            mt = score_item(near_slots + t, x, j, add)
            m8[0] = jnp.maximum(m8[0], jnp.where(is_a, mt, -jnp.inf))
            m8[1] = jnp.maximum(m8[1], jnp.where(is_a, -jnp.inf, mt))
        for x in range(2):
            m8_scr[x, hd] = jnp.broadcast_to(jnp.max(m8[x], axis=0, keepdims=True), (SUBLANES, blk))

        def prob_item(slot, m):
            p = jnp.exp2(fold(s_scr[buf, slot]) - m)
            return jnp.sum(p, axis=0), p.reshape(blk, blk).astype(BF16)

        l8 = [None, None]
        acc = [None, None]
        for slot, (x, j) in enumerate(near):
            lp, pb = prob_item(slot, m8_scr[x, hd])
            pv = jnp.dot(vt_scr[j, hsl(hd), :], pb, preferred_element_type=F32)
            l8[x] = lp if l8[x] is None else l8[x] + lp
            acc[x] = pv if acc[x] is None else acc[x] + pv
        acc_far = None
        for t, (is_a, x, j, live) in enumerate(far):
            lp, pb = prob_item(near_slots + t, m8_scr[x, hd])
            vt = vt_scr[j, hsl(hd), :]
            zero = jnp.zeros_like(vt)
            lhs = jnp.concatenate([jnp.where(is_a, vt, zero), jnp.where(is_a, zero, vt)], axis=0)
            pv = jnp.dot(lhs, pb, preferred_element_type=F32)
            acc_far = pv if acc_far is None else acc_far + pv
            l8[0] = l8[0] + jnp.where(is_a, lp, 0.0)
            l8[1] = l8[1] + jnp.where(is_a, 0.0, lp)

        for x, za_ref in enumerate((zaa_ref, zab_ref)):
            l = jnp.sum(l8[x], axis=0, keepdims=True)
            y = ((acc[x] + acc_far[x * d:(x + 1) * d]) * (1.0 / l)).T
            za = za_ref[:, hsl(hd)].astype(F32)
            o_ref[0, x, 0, :, hsl(hd)] = (y * (za * _sigmoid(za))).astype(BF16)


def _moba_call(p, bias_own, bias_prev, batch, seq):
    blk, d = MOBA_BLOCK, ATT_HEAD_DIM
    nq = seq // blk
    hg = MOBA_HEAD_GROUP
    w = hg * d
    gpt = PROJ_TN // w
    first = lambda col: (lambda b, h, i: (b * nq + i, col * gpt + h))
    last = lambda col: (lambda b, h, i: (b * nq + nq - 1 - i, col * gpt + h))
    return pl.pallas_call(
        _moba_kernel,
        grid=(batch, ATT_HEADS // hg, nq // 2),
        in_specs=[pl.BlockSpec((blk, w), first(COL_QA)),
                  pl.BlockSpec((blk, w), last(COL_QA)),
                  pl.BlockSpec((seq, w), lambda b, h, i: (b, COL_KA * gpt + h)),
                  pl.BlockSpec((seq, w), lambda b, h, i: (b, COL_VA * gpt + h)),
                  pl.BlockSpec((blk, w), first(COL_ZA)),
                  pl.BlockSpec((blk, w), last(COL_ZA)),
                  pl.BlockSpec((hg, blk, blk), lambda b, h, i: (h, 0, 0)),
                  pl.BlockSpec((hg, blk, blk), lambda b, h, i: (h, 0, 0))],
        out_specs=pl.BlockSpec((1, 2, 1, blk, w), lambda b, h, i: (b, 0, i, 0, h)),
        out_shape=jax.ShapeDtypeStruct((batch, 2, nq // 2, blk, ATT_HEADS * d), BF16),
        scratch_shapes=[pltpu.VMEM((nq, w), F32),
                        pltpu.VMEM((nq, w, blk), BF16),
                        pltpu.VMEM((2, blk, w), BF16),
                        pltpu.VMEM((2, hg, nq, blk), F32),
                        pltpu.VMEM((2, nq + 2, blk, blk), F32),
                        pltpu.VMEM((2, hg, SUBLANES, blk), F32)],
        compiler_params=pltpu.CompilerParams(
            dimension_semantics=("arbitrary", "arbitrary", "arbitrary"),
            vmem_limit_bytes=VMEM_LIMIT),
        name="moba",
    )(p, p, p, p, p, p, bias_own, bias_prev)


def _ua_tile_row(t, nq):
    half = nq // 2
    return jnp.where(t < half, t, half + (nq - 1 - t))


def _mlstm_kernel(qm_ref, km_ref, vm_ref, zm_ref, om_ref, g_ref, gt_ref, cw_ref, cb_ref, mlw_ref,
                  o_ref, c_scr, n_scr, m_scr, tailq_scr, tailk_scr, shift_scr):
    c_idx = pl.program_id(1)
    L = qm_ref.shape[0]
    W = ML_HEADS * ML_HEAD_DIM
    pad = SUBLANES

    r_i = lax.broadcasted_iota(jnp.int32, (L, L), 0)
    c_i = lax.broadcasted_iota(jnp.int32, (L, L), 1)

    @pl.when(c_idx == 0)
    def _():
        c_scr[...] = jnp.zeros_like(c_scr)
        n_scr[...] = jnp.zeros_like(n_scr)
        m_scr[...] = jnp.zeros_like(m_scr)
        tailq_scr[...] = jnp.zeros_like(tailq_scr)
        tailk_scr[...] = jnp.zeros_like(tailk_scr)
        for k in range(CONV_WIDTH):
            shift_scr[k] = (c_i == r_i - k).astype(F32).astype(BF16)

    row8 = lax.broadcasted_iota(jnp.int32, (pad, W), 0)

    def conv_silu(src_ref, tail_scr, col0):
        x = src_ref[...]
        tail = tail_scr[...]
        y = cb_ref[:, col0:col0 + W]
        fix = jnp.zeros((pad, W), F32)
        for w in range(CONV_WIDTH):
            k = CONV_WIDTH - 1 - w
            cw = cw_ref[w:w + 1, col0:col0 + W]
            y = y + cw * jnp.dot(shift_scr[k], x, preferred_element_type=F32)
            if k:
                fix = fix + cw * jnp.where(row8 < k, pltpu.roll(tail, k, axis=0), 0.0)
        tail_scr[...] = x[L - 2 * pad:, :].astype(F32)[pad:, :]
        y = jnp.concatenate([y[:pad] + fix, y[pad:]], axis=0)
        return y * _sigmoid(y)

    qc = conv_silu(qm_ref, tailq_scr, 0)
    kc = conv_silu(km_ref, tailk_scr, W) * (ML_HEAD_DIM ** -0.5)

    causal = c_i <= r_i
    ltri = causal.astype(F32)
    utri = (r_i <= c_i).astype(F32)
    g_col = g_ref[...]
    g_row = gt_ref[...]
    b_col_all = jnp.dot(ltri, _log_sigmoid(g_col), precision=lax.Precision.HIGHEST,
                        preferred_element_type=F32)
    b_row_all = jnp.dot(_log_sigmoid(g_row), utri, precision=lax.Precision.HIGHEST,
                        preferred_element_type=F32)

    for hh in range(ML_HEADS):
        sl = slice(hh * ML_HEAD_DIM, (hh + 1) * ML_HEAD_DIM)
        q = qc[:, sl]
        k = kc[:, sl]
        v = vm_ref[:, sl]
        b_col = b_col_all[:, ML_HEADS + hh:ML_HEADS + hh + 1]
        a_col = g_col[:, hh:hh + 1] - b_col
        a_row = g_row[hh:hh + 1, :] - b_row_all[ML_HEADS + hh:ML_HEADS + hh + 1, :]
        m_prev = m_scr[hh][0:1, 0:1]

        a_mask = jnp.where(causal, a_row, -jnp.inf)
        gcol = jnp.maximum(m_prev, jnp.max(a_mask, axis=1, keepdims=True))
        d_mat = jnp.exp(a_mask - gcol)
        inter = jnp.exp(m_prev - gcol)
        g_last = jnp.max(gcol, axis=0, keepdims=True)

        qb = q.astype(BF16)
        s = lax.dot_general(qb, k.astype(BF16), _NT, preferred_element_type=F32) * d_mat
        c_old = c_scr[hh]
        n_old = n_scr[hh]
        num = (inter * jnp.dot(qb, c_old.astype(BF16), preferred_element_type=F32)
               + jnp.dot(s.astype(BF16), v, preferred_element_type=F32))
        den = inter * jnp.sum(q * n_old, axis=1, keepdims=True) + jnp.sum(s, axis=1, keepdims=True)
        h = num / jnp.maximum(jnp.abs(den), jnp.exp(-(b_col + gcol)))

        decay = jnp.exp(m_prev - g_last)
        kw = k * jnp.exp(a_col - g_last)
        c_scr[hh] = decay * c_old + lax.dot_general(kw.astype(BF16), v, _TN,
                                                    preferred_element_type=F32)
        n_scr[hh] = decay * n_old + jnp.sum(kw, axis=0, keepdims=True)
        m_scr[hh] = jnp.broadcast_to(b_col[L - 1:L, :] + g_last, m_scr.shape[1:])

        hg = _sigmoid(om_ref[:, sl].astype(F32)) * h
        hn = hg * lax.rsqrt(jnp.mean(hg * hg, axis=-1, keepdims=True) + EPS) * mlw_ref[:, sl]
        z = zm_ref[:, sl].astype(F32)
        o_ref[:, sl] = (hn * (z * _sigmoid(z))).astype(BF16)


def _mlstm_call(p, g, gt, conv_w, conv_b, ml_norm_w, batch, seq):
    T = p.shape[0]
    L = ML_CHUNK
    nc = seq // L
    W = ML_HEADS * ML_HEAD_DIM
    row = lambda b, c: b * nc + c
    pspec = lambda col: pl.BlockSpec((L, W), lambda b, c: (row(b, c), col))
    return pl.pallas_call(
        _mlstm_kernel,
        grid=(batch, nc),
        in_specs=[pspec(COL_QM), pspec(COL_KM), pspec(COL_VM), pspec(COL_ZM), pspec(COL_OM),
                  pl.BlockSpec((L, LANES), lambda b, c: (row(b, c), 0)),
                  pl.BlockSpec((GATE_COLS, L), lambda b, c: (0, row(b, c))),
                  pl.BlockSpec((CONV_WIDTH, 2 * W), lambda b, c: (0, 0)),
                  pl.BlockSpec((1, 2 * W), lambda b, c: (0, 0)),
                  pl.BlockSpec((1, W), lambda b, c: (0, 0))],
        out_specs=pl.BlockSpec((L, W), lambda b, c: (row(b, c), 0)),
        out_shape=jax.ShapeDtypeStruct((T, W), BF16),
        scratch_shapes=[pltpu.VMEM((ML_HEADS, ML_HEAD_DIM, ML_HEAD_DIM), F32),
                        pltpu.VMEM((ML_HEADS, 1, ML_HEAD_DIM), F32),
                        pltpu.VMEM((ML_HEADS, SUBLANES, LANES), F32),
                        pltpu.VMEM((SUBLANES, W), F32),
                        pltpu.VMEM((SUBLANES, W), F32),
                        pltpu.VMEM((CONV_WIDTH, L, L), BF16)],
        compiler_params=pltpu.CompilerParams(dimension_semantics=("arbitrary", "arbitrary"),
                                             vmem_limit_bytes=VMEM_LIMIT),
        name="mlstm",
    )(p, p, p, p, p, g, gt, conv_w, conv_b, ml_norm_w)


def _outp_kernel(*refs):
    ua_refs, (um_ref, ga_ref, gm_ref, x_ref, ada_ref, wa_ref, wm_ref, wo_ref, o_ref) = refs[:-9], refs[-9:]
    ua = jnp.concatenate([r[...] for r in ua_refs], axis=0)
    ya = jnp.dot(ua, wa_ref[...], preferred_element_type=F32)
    ym = jnp.dot(um_ref[...], wm_ref[...], preferred_element_type=F32)
    y = _sigmoid(ga_ref[...].astype(F32)) * ya + _sigmoid(gm_ref[...].astype(F32)) * ym
    gate = ada_ref[0][2:3, :]
    o_ref[...] = x_ref[...] + gate * jnp.dot(y.astype(BF16), wo_ref[...], preferred_element_type=F32)


def _outp_call(ua, um, p, x2, ada3, wa, wm, wo, seq):
    T, D = x2.shape
    tm = OUT_TM
    steps_per_seq = seq // tm
    blk = MOBA_BLOCK
    nq = seq // blk
    tiles_per_step = tm // blk

    def ua_spec(e):
        def index(i):
            t = (i % steps_per_seq) * tiles_per_step + e
            return ((i // steps_per_seq) * nq + _ua_tile_row(t, nq), 0)
        return pl.BlockSpec((blk, D), index)

    act = lambda col: pl.BlockSpec((tm, D), lambda i: (i, col))
    wspec = pl.BlockSpec((D, D), lambda i: (0, 0))
    return pl.pallas_call(
        _outp_kernel,
        grid=(T // tm,),
        in_specs=[ua_spec(e) for e in range(tiles_per_step)]
                 + [act(0), act(COL_GA), act(COL_GM), act(0),
                    pl.BlockSpec((1, 3, D), lambda i: (i // steps_per_seq, 0, 0)),
                    wspec, wspec, wspec],
        out_specs=pl.BlockSpec((tm, D), lambda i: (i, 0)),
        out_shape=jax.ShapeDtypeStruct((T, D), F32),
        compiler_params=pltpu.CompilerParams(dimension_semantics=("arbitrary",),
                                             vmem_limit_bytes=VMEM_LIMIT),
        name="outp",
    )(*([ua] * tiles_per_step), um, p, p, x2, ada3, wa, wm, wo)


def _rel_bias_tables(rel_bias):
    max_exact = REL_BUCKETS // 2
    rb = rel_bias.astype(F32)
    rb = (rb - rb[REL_BUCKETS - 1:, :]) * LOG2E
    key = jnp.arange(MOBA_BLOCK)[:, None]
    qry = jnp.arange(MOBA_BLOCK)[None, :]
    d_own = qry - key

    def tile(dist):
        nf = jnp.maximum(dist, 1).astype(F32)
        large = max_exact + (jnp.log(nf / max_exact) / math.log(REL_MAX_DIST / max_exact)
                             * (REL_BUCKETS - max_exact)).astype(jnp.int32)
        bucket = jnp.where(dist < max_exact, dist, jnp.minimum(large, REL_BUCKETS - 1))
        onehot = (bucket[..., None] == jnp.arange(REL_BUCKETS)).astype(F32)
        return jnp.einsum('kqb,bh->hkq', onehot, rb, precision=lax.Precision.HIGHEST)

    own = jnp.where(d_own >= 0, tile(jnp.maximum(d_own, 0)), NEG_INF)
    prev = tile(d_own + MOBA_BLOCK)
    return own, prev


def _layer(x2, c, w_ada, b_ada, norm_w, w_in, q_norm_w, k_norm_w, rel_bias, conv_w, conv_b,
           b_igate, b_fgate, ml_norm_w, w_att_proj, w_ml_proj, w_out, batch, seq):
    D = D_MODEL
    g0 = 4 * D + 5 * D
    ada3 = _ada_call(c, w_ada, b_ada).reshape(batch, 3, D)

    w_main = jnp.concatenate([w_in[:, :g0], w_in[:, g0 + GATE_COLS:]], axis=1).astype(BF16)
    wg = jnp.pad(w_in[:, g0:g0 + GATE_COLS], ((0, 0), (0, LANES - GATE_COLS)))
    gb = jnp.pad(jnp.concatenate([b_igate, b_fgate]), (0, LANES - GATE_COLS)).reshape(1, LANES)
    q_scale = ATT_HEAD_DIM ** -0.5 * LOG2E
    hnw = jnp.stack([jnp.tile(q_norm_w.astype(F32), ATT_HEADS) * q_scale,
                     jnp.tile(k_norm_w.astype(F32), ATT_HEADS)])

    p, g, gt = _inproj_call(x2, ada3, norm_w.reshape(1, D), w_main, wg, gb, hnw, seq)
    bias_own, bias_prev = _rel_bias_tables(rel_bias)
    ua = _moba_call(p, bias_own, bias_prev, batch, seq).reshape(batch * seq, D)
    um = _mlstm_call(p, g, gt, conv_w, conv_b.reshape(1, -1), ml_norm_w.reshape(1, -1), batch, seq)
    return _outp_call(ua, um, p, x2, ada3, w_att_proj.astype(BF16), w_ml_proj.astype(BF16),
                      w_out.astype(BF16), seq)


def kernel(x, c, w_ada, b_ada, norm_w, w_in, q_norm_w, k_norm_w, rel_bias, conv_w, conv_b, b_igate,
           b_fgate, ml_norm_w, w_att_proj, w_ml_proj, w_out):
    batch, seq, D = x.shape
    assert D == D_MODEL and seq % PROJ_TM == 0 and seq % MOBA_BLOCK == 0 and seq % ML_CHUNK == 0
    x2 = x.reshape(batch * seq, D)
    for l in range(w_in.shape[0]):
        x2 = _layer(x2, c, w_ada[l], b_ada[l], norm_w[l], w_in[l], q_norm_w[l], k_norm_w[l],
                    rel_bias, conv_w[l], conv_b[l], b_igate[l], b_fgate[l], ml_norm_w[l],
                    w_att_proj[l], w_ml_proj[l], w_out[l], batch, seq)
    return x2.reshape(batch, seq, D)
```

```python
import functools
import math

import jax
import jax.numpy as jnp
import numpy as np
from jax import lax
from jax.experimental import pallas as pl
from jax.experimental.pallas import tpu as pltpu

F32 = jnp.float32
BF16 = jnp.bfloat16

D_MODEL = 1024
ATT_HEADS = 8
ATT_HEAD_DIM = D_MODEL // ATT_HEADS
MOBA_BLOCK = 256
MOBA_TOPK = 3
REL_BUCKETS = 32
REL_MAX_DIST = 128
ML_HEADS = 4
ML_HEAD_DIM = D_MODEL // ML_HEADS
CONV_WIDTH = 4
EPS = 1e-6
NEG_INF = -1e30
LOG2E = math.log2(math.e)

LANES = 128
SUBLANES = 8
BF16_SUBLANES = 16
GATE_COLS = 2 * ML_HEADS
ML_CHUNK = 256
PROJ_TM = 1024
PROJ_TN = 1024
OUT_TM = 512
MOBA_HEAD_GROUP = 4
VMEM_LIMIT = 52 * 1024 * 1024

COL_QA, COL_KA, COL_VA, COL_ZA, COL_QM, COL_KM, COL_VM, COL_ZM, COL_OM, COL_GA, COL_GM = range(11)
N_COL_TILES = 11

_NT = (((1,), (1,)), ((), ()))
_TN = (((0,), (0,)), ((), ()))


def _sigmoid(x):
    return 1.0 / (1.0 + jnp.exp(-x))


def _log_sigmoid(x):
    return jnp.minimum(x, 0.0) - jnp.log1p(jnp.exp(-jnp.abs(x)))


def _ada_kernel(c_ref, w_ref, b_ref, o_ref):
    o_ref[...] = jnp.dot(c_ref[...], w_ref[...], precision=lax.Precision.HIGHEST,
                         preferred_element_type=F32) + b_ref[...]


def _ada_call(c, w_ada, b_ada):
    B, D = c.shape
    return pl.pallas_call(
        _ada_kernel,
        grid=(3,),
        in_specs=[pl.BlockSpec((B, D), lambda j: (0, 0)),
                  pl.BlockSpec((D, D), lambda j: (0, j)),
                  pl.BlockSpec((1, D), lambda j: (0, j))],
        out_specs=pl.BlockSpec((B, D), lambda j: (0, j)),
        out_shape=jax.ShapeDtypeStruct((B, 3 * D), F32),
        compiler_params=pltpu.CompilerParams(dimension_semantics=("arbitrary",),
                                             vmem_limit_bytes=VMEM_LIMIT),
        name="ada",
    )(c, w_ada, b_ada.reshape(1, 3 * D))


def _inproj_kernel(x_ref, ada_ref, nw_ref, w_ref, wg_ref, gb_ref, hnw_ref,
                   p_ref, g_ref, gt_ref, h_scr):
    j = pl.program_id(1)

    @pl.when(j == 0)
    def _():
        x = x_ref[...]
        ada = ada_ref[0]
        y = x * lax.rsqrt(jnp.mean(x * x, axis=-1, keepdims=True) + EPS) * nw_ref[...]
        h = y * (1.0 + ada[1:2, :]) + ada[0:1, :]
        h_scr[...] = h.astype(BF16)
        g = jnp.dot(h, wg_ref[...], precision=lax.Precision.HIGHEST,
                    preferred_element_type=F32) + gb_ref[...]
        g_ref[...] = g
        gt_ref[...] = g.T[:GATE_COLS, :]

    acc = jnp.dot(h_scr[...], w_ref[...], preferred_element_type=F32)

    @pl.when(j <= COL_KA)
    def _():
        w = hnw_ref[pl.ds(j, 1), :]
        for hd in range(ATT_HEADS):
            sl = slice(hd * ATT_HEAD_DIM, (hd + 1) * ATT_HEAD_DIM)
            a = acc[:, sl]
            r = lax.rsqrt(jnp.mean(a * a, axis=-1, keepdims=True) + EPS)
            p_ref[:, sl] = (a * r * w[:, sl]).astype(BF16)

    @pl.when(j > COL_KA)
    def _():
        p_ref[...] = acc.astype(BF16)


def _inproj_call(x2, ada3, norm_w, w_main, wg, gb, hnw, seq):
    T, D = x2.shape
    tm, tn = PROJ_TM, PROJ_TN
    tiles_per_seq = seq // tm
    return pl.pallas_call(
        _inproj_kernel,
        grid=(T // tm, N_COL_TILES),
        in_specs=[pl.BlockSpec((tm, D), lambda i, j: (i, 0)),
                  pl.BlockSpec((1, 3, D), lambda i, j: (i // tiles_per_seq, 0, 0)),
                  pl.BlockSpec((1, D), lambda i, j: (0, 0)),
                  pl.BlockSpec((D, tn), lambda i, j: (0, j)),
                  pl.BlockSpec((D, LANES), lambda i, j: (0, 0)),
                  pl.BlockSpec((1, LANES), lambda i, j: (0, 0)),
                  pl.BlockSpec((2, tn), lambda i, j: (0, 0))],
        out_specs=[pl.BlockSpec((tm, tn), lambda i, j: (i, j)),
                   pl.BlockSpec((tm, LANES), lambda i, j: (i, 0)),
                   pl.BlockSpec((GATE_COLS, tm), lambda i, j: (0, i))],
        out_shape=[jax.ShapeDtypeStruct((T, N_COL_TILES * tn), BF16),
                   jax.ShapeDtypeStruct((T, LANES), F32),
                   jax.ShapeDtypeStruct((GATE_COLS, T), F32)],
        scratch_shapes=[pltpu.VMEM((tm, D), BF16)],
        compiler_params=pltpu.CompilerParams(dimension_semantics=("arbitrary", "arbitrary"),
                                             vmem_limit_bytes=VMEM_LIMIT),
        name="inproj",
    )(x2, ada3, norm_w, w_main, wg, gb, hnw)


def _moba_kernel(qa_ref, qb_ref, k_ref, v_ref, zaa_ref, zab_ref, bown_ref, bprev_ref, o_ref,
                 kmean_scr, vt_scr, q2_scr, selb_scr, s_scr, m8_scr):
    step = pl.program_id(2)
    blk, d = MOBA_BLOCK, ATT_HEAD_DIM
    da = d + BF16_SUBLANES
    nb = k_ref.shape[0] // blk
    heads = qa_ref.shape[1] // d
    tile = (step, nb - 1 - step)
    n_far = (jnp.maximum(step - 1, 0), nb - 2 - step)
    far_slots = nb - 2
    near_slots = 4

    def hsl(hd):
        return slice(hd * d, (hd + 1) * d)

    @pl.when(step == 0)
    def _():
        for n in range(nb):
            kb = k_ref[n * blk:(n + 1) * blk, :].astype(F32)
            kmean_scr[n:n + 1, :] = jnp.mean(kb, axis=0, keepdims=True)
            for hd in range(heads):
                vt_scr[n, hd * da:hd * da + d, :] = v_ref[n * blk:(n + 1) * blk, hsl(hd)].T
                vt_scr[n, hd * da + d:(hd + 1) * da, :] = jnp.ones((da - d, blk), BF16)

    q2_scr[0] = qa_ref[...]
    q2_scr[1] = qb_ref[...]

    def fold(x):
        return x.reshape(blk // SUBLANES, SUBLANES, blk)

    rows = lax.broadcasted_iota(jnp.int32, (nb, blk), 0)
    for x in range(2):
        valid = rows < tile[x]
        for hd in range(heads):
            gate = lax.dot_general(kmean_scr[:, hsl(hd)], q2_scr[x, :, hsl(hd)].astype(F32), _NT,
                                   precision=lax.Precision.HIGHEST, preferred_element_type=F32)
            g = jnp.where(valid, gate, -jnp.inf)
            selb = jnp.full(gate.shape, NEG_INF, F32)
            for _ in range(MOBA_TOPK):
                mx = jnp.max(g, axis=0, keepdims=True)
                first = jnp.min(jnp.where(g == mx, rows, nb), axis=0, keepdims=True)
                pick = rows == first
                selb = jnp.where(pick, 0.0, selb)
                g = jnp.where(pick, -jnp.inf, g)
            selb_scr[x, hd] = jnp.where(valid, selb, NEG_INF)

    def near_items(hd):
        out = []
        for x in range(2):
            jp = jnp.maximum(tile[x] - 1, 0)
            out.append((2 * x, x, tile[x], bown_ref[hd], None))
            out.append((2 * x + 1, x, jp, bprev_ref[hd] + selb_scr[x, hd, pl.ds(jp, 1), :], None))
        return out

    shared_slots = nb // 2 - 2

    def far_items(hd):
        out = []
        for t in range(far_slots):
            if t < shared_slots:
                is_a = t < n_far[0]
                x = jnp.where(is_a, 0, 1)
                j = jnp.where(is_a, t, t - n_far[0])
                live = j < jnp.where(is_a, n_far[0], n_far[1])
            else:
                is_a, x, j = None, 1, t - n_far[0]
                live = j < n_far[1]
            add = jnp.where(live, selb_scr[x, hd, pl.ds(j, 1), :], NEG_INF)
            out.append((near_slots + t, x, j, add, is_a))
        return out

    items = [near_items(hd) + far_items(hd) for hd in range(heads)]
    m8 = [[None, None] for _ in range(heads)]
    for i in range(near_slots + far_slots):
        for hd in range(heads):
            slot, x, j, add, is_a = items[hd][i]
            kj = k_ref[pl.ds(pl.multiple_of(j * blk, blk), blk), hsl(hd)]
            s = lax.dot_general(kj, q2_scr[x, :, hsl(hd)], _NT, preferred_element_type=F32) + add
            s_scr[hd, slot] = s
            mt = jnp.max(fold(s), axis=0)
            for y in range(2):
                if is_a is None:
                    if y != x:
                        continue
                    part = mt
                else:
                    part = jnp.where(is_a, mt, -jnp.inf) if y == 0 else jnp.where(is_a, -jnp.inf, mt)
                m8[hd][y] = part if m8[hd][y] is None else jnp.maximum(m8[hd][y], part)
    for hd in range(heads):
        for x in range(2):
            m8_scr[x, hd] = jnp.broadcast_to(jnp.max(m8[hd][x], axis=0, keepdims=True), (SUBLANES, blk))

    acc = [[None, None] for _ in range(heads)]
    acc_far = [None] * heads

    def add_to(lst, k, v):
        lst[k] = v if lst[k] is None else lst[k] + v

    for i in range(near_slots + far_slots):
        for hd in range(heads):
            slot, x, j, _, is_a = items[hd][i]
            p = jnp.exp2(fold(s_scr[hd, slot]) - m8_scr[x, hd])
            pb = p.reshape(blk, blk).astype(BF16)
            vt = vt_scr[j, hd * da:(hd + 1) * da, :]
            if is_a is None:
                add_to(acc[hd], x, jnp.dot(vt, pb, preferred_element_type=F32))
            else:
                zero = jnp.zeros_like(vt)
                lhs = jnp.concatenate([jnp.where(is_a, vt, zero), jnp.where(is_a, zero, vt)], axis=0)
                add_to(acc_far, hd, jnp.dot(lhs, pb, preferred_element_type=F32))

    for hd in range(heads):
        for x, za_ref in enumerate((zaa_ref, zab_ref)):
            tot = acc[hd][x] + acc_far[hd][x * da:(x + 1) * da]
            y = (tot[:d] * (1.0 / tot[d:d + 1])).T
            za = za_ref[:, hsl(hd)].astype(F32)
            o_ref[0, x, 0, :, hsl(hd)] = (y * (za * _sigmoid(za))).astype(BF16)


def _moba_call(p, bias_own, bias_prev, batch, seq):
    blk, d = MOBA_BLOCK, ATT_HEAD_DIM
    nq = seq // blk
    hg = MOBA_HEAD_GROUP
    w = hg * d
    gpt = PROJ_TN // w
    first = lambda col: (lambda b, h, i: (b * nq + i, col * gpt + h))
    last = lambda col: (lambda b, h, i: (b * nq + nq - 1 - i, col * gpt + h))
    return pl.pallas_call(
        _moba_kernel,
        grid=(batch, ATT_HEADS // hg, nq // 2),
        in_specs=[pl.BlockSpec((blk, w), first(COL_QA)),
                  pl.BlockSpec((blk, w), last(COL_QA)),
                  pl.BlockSpec((seq, w), lambda b, h, i: (b, COL_KA * gpt + h)),
                  pl.BlockSpec((seq, w), lambda b, h, i: (b, COL_VA * gpt + h)),
                  pl.BlockSpec((blk, w), first(COL_ZA)),
                  pl.BlockSpec((blk, w), last(COL_ZA)),
                  pl.BlockSpec((hg, blk, blk), lambda b, h, i: (h, 0, 0)),
                  pl.BlockSpec((hg, blk, blk), lambda b, h, i: (h, 0, 0))],
        out_specs=pl.BlockSpec((1, 2, 1, blk, w), lambda b, h, i: (b, 0, i, 0, h)),
        out_shape=jax.ShapeDtypeStruct((batch, 2, nq // 2, blk, ATT_HEADS * d), BF16),
        scratch_shapes=[pltpu.VMEM((nq, w), F32),
                        pltpu.VMEM((nq, hg * (d + BF16_SUBLANES), blk), BF16),
                        pltpu.VMEM((2, blk, w), BF16),
                        pltpu.VMEM((2, hg, nq, blk), F32),
                        pltpu.VMEM((hg, nq + 2, blk, blk), F32),
                        pltpu.VMEM((2, hg, SUBLANES, blk), F32)],
        compiler_params=pltpu.CompilerParams(
            dimension_semantics=("arbitrary", "arbitrary", "arbitrary"),
            vmem_limit_bytes=VMEM_LIMIT),
        name="moba",
    )(p, p, p, p, p, p, bias_own, bias_prev)


def _ua_tile_row(t, nq):
    half = nq // 2
    return jnp.where(t < half, t, half + (nq - 1 - t))


def _mlstm_kernel(qm_ref, km_ref, vm_ref, zm_ref, om_ref, g_ref, gt_ref, cw_ref, cb_ref, mlw_ref,
                  o_ref, c_scr, n_scr, m_scr, tailq_scr, tailk_scr, shift_scr):
    c_idx = pl.program_id(1)
    L = qm_ref.shape[0]
    W = ML_HEADS * ML_HEAD_DIM
    pad = SUBLANES

    r_i = lax.broadcasted_iota(jnp.int32, (L, L), 0)
    c_i = lax.broadcasted_iota(jnp.int32, (L, L), 1)

    @pl.when(c_idx == 0)
    def _():
        c_scr[...] = jnp.zeros_like(c_scr)
        n_scr[...] = jnp.zeros_like(n_scr)
        m_scr[...] = jnp.zeros_like(m_scr)
        tailq_scr[...] = jnp.zeros_like(tailq_scr)
        tailk_scr[...] = jnp.zeros_like(tailk_scr)
        for k in range(CONV_WIDTH):
            shift_scr[k] = (c_i == r_i - k).astype(F32).astype(BF16)

    row8 = lax.broadcasted_iota(jnp.int32, (pad, W), 0)

    def conv_silu(src_ref, tail_scr, col0):
        x = src_ref[...]
        tail = tail_scr[...]
        y = cb_ref[:, col0:col0 + W]
        fix = jnp.zeros((pad, W), F32)
        for w in range(CONV_WIDTH):
            k = CONV_WIDTH - 1 - w
            cw = cw_ref[w:w + 1, col0:col0 + W]
            y = y + cw * jnp.dot(shift_scr[k], x, preferred_element_type=F32)
            if k:
                fix = fix + cw * jnp.where(row8 < k, pltpu.roll(tail, k, axis=0), 0.0)
        tail_scr[...] = x[L - 2 * pad:, :].astype(F32)[pad:, :]
        y = jnp.concatenate([y[:pad] + fix, y[pad:]], axis=0)
        return y * _sigmoid(y)

    qc = conv_silu(qm_ref, tailq_scr, 0)
    kc = conv_silu(km_ref, tailk_scr, W) * (ML_HEAD_DIM ** -0.5)

    causal = c_i <= r_i
    ltri = causal.astype(F32)
    utri = (r_i <= c_i).astype(F32)
    g_col = g_ref[...]
    g_row = gt_ref[...]
    b_col_all = jnp.dot(ltri, _log_sigmoid(g_col), precision=lax.Precision.HIGHEST,
                        preferred_element_type=F32)
    b_row_all = jnp.dot(_log_sigmoid(g_row), utri, precision=lax.Precision.HIGHEST,
                        preferred_element_type=F32)

    for hh in range(ML_HEADS):
        sl = slice(hh * ML_HEAD_DIM, (hh + 1) * ML_HEAD_DIM)
        q = qc[:, sl]
        k = kc[:, sl]
        v = vm_ref[:, sl]
        b_col = b_col_all[:, ML_HEADS + hh:ML_HEADS + hh + 1]
        a_col = g_col[:, hh:hh + 1] - b_col
        a_row = g_row[hh:hh + 1, :] - b_row_all[ML_HEADS + hh:ML_HEADS + hh + 1, :]
        m_prev = m_scr[hh][0:1, 0:1]

        a_mask = jnp.where(causal, a_row, -jnp.inf)
        gcol = jnp.maximum(m_prev, jnp.max(a_mask, axis=1, keepdims=True))
        d_mat = jnp.exp(a_mask - gcol)
        inter = jnp.exp(m_prev - gcol)
        g_last = jnp.max(gcol, axis=0, keepdims=True)

        qb = q.astype(BF16)
        s = lax.dot_general(qb, k.astype(BF16), _NT, preferred_element_type=F32) * d_mat
        c_old = c_scr[hh]
        n_old = n_scr[hh]
        num = (inter * jnp.dot(qb, c_old.astype(BF16), preferred_element_type=F32)
               + jnp.dot(s.astype(BF16), v, preferred_element_type=F32))
        den = inter * jnp.sum(q * n_old, axis=1, keepdims=True) + jnp.sum(s, axis=1, keepdims=True)
        h = num / jnp.maximum(jnp.abs(den), jnp.exp(-(b_col + gcol)))

        decay = jnp.exp(m_prev - g_last)
        kw = k * jnp.exp(a_col - g_last)
        c_scr[hh] = decay * c_old + lax.dot_general(kw.astype(BF16), v, _TN,
                                                    preferred_element_type=F32)
        n_scr[hh] = decay * n_old + jnp.sum(kw, axis=0, keepdims=True)
        m_scr[hh] = jnp.broadcast_to(b_col[L - 1:L, :] + g_last, m_scr.shape[1:])

        hg = _sigmoid(om_ref[:, sl].astype(F32)) * h
        hn = hg * lax.rsqrt(jnp.mean(hg * hg, axis=-1, keepdims=True) + EPS) * mlw_ref[:, sl]
        z = zm_ref[:, sl].astype(F32)
        o_ref[:, sl] = (hn * (z * _sigmoid(z))).astype(BF16)


def _mlstm_call(p, g, gt, conv_w, conv_b, ml_norm_w, batch, seq):
    T = p.shape[0]
    L = ML_CHUNK
    nc = seq // L
    W = ML_HEADS * ML_HEAD_DIM
    row = lambda b, c: b * nc + c
    pspec = lambda col: pl.BlockSpec((L, W), lambda b, c: (row(b, c), col))
    return pl.pallas_call(
        _mlstm_kernel,
        grid=(batch, nc),
        in_specs=[pspec(COL_QM), pspec(COL_KM), pspec(COL_VM), pspec(COL_ZM), pspec(COL_OM),
                  pl.BlockSpec((L, LANES), lambda b, c: (row(b, c), 0)),
                  pl.BlockSpec((GATE_COLS, L), lambda b, c: (0, row(b, c))),
                  pl.BlockSpec((CONV_WIDTH, 2 * W), lambda b, c: (0, 0)),
                  pl.BlockSpec((1, 2 * W), lambda b, c: (0, 0)),
                  pl.BlockSpec((1, W), lambda b, c: (0, 0))],
        out_specs=pl.BlockSpec((L, W), lambda b, c: (row(b, c), 0)),
        out_shape=jax.ShapeDtypeStruct((T, W), BF16),
        scratch_shapes=[pltpu.VMEM((ML_HEADS, ML_HEAD_DIM, ML_HEAD_DIM), F32),
                        pltpu.VMEM((ML_HEADS, 1, ML_HEAD_DIM), F32),
                        pltpu.VMEM((ML_HEADS, SUBLANES, LANES), F32),
                        pltpu.VMEM((SUBLANES, W), F32),
                        pltpu.VMEM((SUBLANES, W), F32),
                        pltpu.VMEM((CONV_WIDTH, L, L), BF16)],
        compiler_params=pltpu.CompilerParams(dimension_semantics=("arbitrary", "arbitrary"),
                                             vmem_limit_bytes=VMEM_LIMIT),
        name="mlstm",
    )(p, p, p, p, p, g, gt, conv_w, conv_b, ml_norm_w)


def _outp_kernel(*refs):
    ua_refs, (um_ref, ga_ref, gm_ref, x_ref, ada_ref, wa_ref, wm_ref, wo_ref, o_ref) = refs[:-9], refs[-9:]
    ua = jnp.concatenate([r[...] for r in ua_refs], axis=0)
    ya = jnp.dot(ua, wa_ref[...], preferred_element_type=F32)
    ym = jnp.dot(um_ref[...], wm_ref[...], preferred_element_type=F32)
    y = _sigmoid(ga_ref[...].astype(F32)) * ya + _sigmoid(gm_ref[...].astype(F32)) * ym
    gate = ada_ref[0][2:3, :]
    o_ref[...] = x_ref[...] + gate * jnp.dot(y.astype(BF16), wo_ref[...], preferred_element_type=F32)


def _outp_call(ua, um, p, x2, ada3, wa, wm, wo, seq):
    T, D = x2.shape
    tm = OUT_TM
    steps_per_seq = seq // tm
    blk = MOBA_BLOCK
    nq = seq // blk
    tiles_per_step = tm // blk

    def ua_spec(e):
        def index(i):
            t = (i % steps_per_seq) * tiles_per_step + e
            return ((i // steps_per_seq) * nq + _ua_tile_row(t, nq), 0)
        return pl.BlockSpec((blk, D), index)

    act = lambda col: pl.BlockSpec((tm, D), lambda i: (i, col))
    wspec = pl.BlockSpec((D, D), lambda i: (0, 0))
    return pl.pallas_call(
        _outp_kernel,
        grid=(T // tm,),
        in_specs=[ua_spec(e) for e in range(tiles_per_step)]
                 + [act(0), act(COL_GA), act(COL_GM), act(0),
                    pl.BlockSpec((1, 3, D), lambda i: (i // steps_per_seq, 0, 0)),
                    wspec, wspec, wspec],
        out_specs=pl.BlockSpec((tm, D), lambda i: (i, 0)),
        out_shape=jax.ShapeDtypeStruct((T, D), F32),
        compiler_params=pltpu.CompilerParams(dimension_semantics=("arbitrary",),
                                             vmem_limit_bytes=VMEM_LIMIT),
        name="outp",
    )(*([ua] * tiles_per_step), um, p, p, x2, ada3, wa, wm, wo)


def _rel_bias_tables(rel_bias):
    max_exact = REL_BUCKETS // 2
    rb = rel_bias.astype(F32)
    rb = (rb - rb[REL_BUCKETS - 1:, :]) * LOG2E
    key = jnp.arange(MOBA_BLOCK)[:, None]
    qry = jnp.arange(MOBA_BLOCK)[None, :]
    d_own = qry - key

    def tile(dist):
        nf = jnp.maximum(dist, 1).astype(F32)
        large = max_exact + (jnp.log(nf / max_exact) / math.log(REL_MAX_DIST / max_exact)
                             * (REL_BUCKETS - max_exact)).astype(jnp.int32)
        bucket = jnp.where(dist < max_exact, dist, jnp.minimum(large, REL_BUCKETS - 1))
        onehot = (bucket[..., None] == jnp.arange(REL_BUCKETS)).astype(F32)
        return jnp.einsum('kqb,bh->hkq', onehot, rb, precision=lax.Precision.HIGHEST)

    own = jnp.where(d_own >= 0, tile(jnp.maximum(d_own, 0)), NEG_INF)
    prev = tile(d_own + MOBA_BLOCK)
    return own, prev


def _layer(x2, c, w_ada, b_ada, norm_w, w_in, q_norm_w, k_norm_w, rel_bias, conv_w, conv_b,
           b_igate, b_fgate, ml_norm_w, w_att_proj, w_ml_proj, w_out, batch, seq):
    D = D_MODEL
    g0 = 4 * D + 5 * D
    ada3 = _ada_call(c, w_ada, b_ada).reshape(batch, 3, D)

    w_main = jnp.concatenate([w_in[:, :g0], w_in[:, g0 + GATE_COLS:]], axis=1).astype(BF16)
    wg = jnp.pad(w_in[:, g0:g0 + GATE_COLS], ((0, 0), (0, LANES - GATE_COLS)))
    gb = jnp.pad(jnp.concatenate([b_igate, b_fgate]), (0, LANES - GATE_COLS)).reshape(1, LANES)
    q_scale = ATT_HEAD_DIM ** -0.5 * LOG2E
    hnw = jnp.stack([jnp.tile(q_norm_w.astype(F32), ATT_HEADS) * q_scale,
                     jnp.tile(k_norm_w.astype(F32), ATT_HEADS)])

    p, g, gt = _inproj_call(x2, ada3, norm_w.reshape(1, D), w_main, wg, gb, hnw, seq)
    bias_own, bias_prev = _rel_bias_tables(rel_bias)
    ua = _moba_call(p, bias_own, bias_prev, batch, seq).reshape(batch * seq, D)
    um = _mlstm_call(p, g, gt, conv_w, conv_b.reshape(1, -1), ml_norm_w.reshape(1, -1), batch, seq)
    return _outp_call(ua, um, p, x2, ada3, w_att_proj.astype(BF16), w_ml_proj.astype(BF16),
                      w_out.astype(BF16), seq)


def kernel(x, c, w_ada, b_ada, norm_w, w_in, q_norm_w, k_norm_w, rel_bias, conv_w, conv_b, b_igate,
           b_fgate, ml_norm_w, w_att_proj, w_ml_proj, w_out):
    batch, seq, D = x.shape
    assert D == D_MODEL and seq % PROJ_TM == 0 and seq % MOBA_BLOCK == 0 and seq % ML_CHUNK == 0
    x2 = x.reshape(batch * seq, D)
    for l in range(w_in.shape[0]):
        x2 = _layer(x2, c, w_ada[l], b_ada[l], norm_w[l], w_in[l], q_norm_w[l], k_norm_w[l],
                    rel_bias, conv_w[l], conv_b[l], b_igate[l], b_fgate[l], ml_norm_w[l],
                    w_att_proj[l], w_ml_proj[l], w_out[l], batch, seq)
    return x2.reshape(batch, seq, D)
```

```python
import functools
import math

import jax
import jax.numpy as jnp
import numpy as np
from jax import lax
from jax.experimental import pallas as pl
from jax.experimental.pallas import tpu as pltpu

F32 = jnp.float32
BF16 = jnp.bfloat16

D_MODEL = 1024
ATT_HEADS = 8
ATT_HEAD_DIM = D_MODEL // ATT_HEADS
MOBA_BLOCK = 256
MOBA_TOPK = 3
REL_BUCKETS = 32
REL_MAX_DIST = 128
ML_HEADS = 4
ML_HEAD_DIM = D_MODEL // ML_HEADS
CONV_WIDTH = 4
EPS = 1e-6
NEG_INF = -1e30
LOG2E = math.log2(math.e)

LANES = 128
SUBLANES = 8
BF16_SUBLANES = 16
GATE_COLS = 2 * ML_HEADS
ML_CHUNK = 256
PROJ_TM = 1024
PROJ_TN = 1024
OUT_TM = 512
MOBA_HEAD_GROUP = 4
VMEM_LIMIT = 52 * 1024 * 1024

COL_QA, COL_KA, COL_VA, COL_ZA, COL_QM, COL_KM, COL_VM, COL_ZM, COL_OM, COL_GA, COL_GM = range(11)
N_COL_TILES = 11
N_HEAD_TILES = 9

_NT = (((1,), (1,)), ((), ()))
_TN = (((0,), (0,)), ((), ()))


def _sigmoid(x):
    return 1.0 / (1.0 + jnp.exp(-x))


def _log_sigmoid(x):
    return jnp.minimum(x, 0.0) - jnp.log1p(jnp.exp(-jnp.abs(x)))


def _ada_kernel(c_ref, w_ref, b_ref, o_ref):
    o_ref[...] = jnp.dot(c_ref[...], w_ref[...], precision=lax.Precision.HIGHEST,
                         preferred_element_type=F32) + b_ref[...]


def _ada_call(c, w_ada, b_ada):
    B, D = c.shape
    return pl.pallas_call(
        _ada_kernel,
        grid=(3,),
        in_specs=[pl.BlockSpec((B, D), lambda j: (0, 0)),
                  pl.BlockSpec((D, D), lambda j: (0, j)),
                  pl.BlockSpec((1, D), lambda j: (0, j))],
        out_specs=pl.BlockSpec((B, D), lambda j: (0, j)),
        out_shape=jax.ShapeDtypeStruct((B, 3 * D), F32),
        compiler_params=pltpu.CompilerParams(dimension_semantics=("arbitrary",),
                                             vmem_limit_bytes=VMEM_LIMIT),
        name="ada",
    )(c, w_ada, b_ada.reshape(1, 3 * D))


def _inproj_kernel(x_ref, ada_ref, nw_ref, wa_ref, wb_ref, wg_ref, gb_ref, hnw_ref,
                   p_ref, g_ref, gt_ref, h_scr):
    j = pl.program_id(1)
    tn = p_ref.shape[1]

    @pl.when(j == 0)
    def _():
        x = x_ref[...]
        ada = ada_ref[0]
        y = x * lax.rsqrt(jnp.mean(x * x, axis=-1, keepdims=True) + EPS) * nw_ref[...]
        h = (y * (1.0 + ada[1:2, :]) + ada[0:1, :]).astype(BF16)
        h_scr[...] = h
        g = jnp.dot(h, wg_ref[...], preferred_element_type=F32) + gb_ref[...]
        g_ref[...] = g
        gt_ref[...] = g.T[:GATE_COLS, :]

    @pl.when(j <= COL_KA)
    def _():
        h = h_scr[...]
        w = hnw_ref[pl.ds(j, 1), :]
        pair = 2 * ATT_HEAD_DIM
        for c0 in range(0, tn, pair):
            acc = jnp.dot(h, wa_ref[:, c0:c0 + pair], preferred_element_type=F32)
            for c in range(c0, c0 + pair, ATT_HEAD_DIM):
                a = acc[:, c - c0:c - c0 + ATT_HEAD_DIM]
                r = lax.rsqrt(jnp.mean(a * a, axis=-1, keepdims=True) + EPS)
                p_ref[:, c:c + ATT_HEAD_DIM] = (a * r * w[:, c:c + ATT_HEAD_DIM]).astype(BF16)

    @pl.when(jnp.logical_and(j > COL_KA, j < N_HEAD_TILES))
    def _():
        p_ref[...] = jnp.dot(h_scr[...], wa_ref[...], preferred_element_type=F32).astype(BF16)

    @pl.when(j >= N_HEAD_TILES)
    def _():
        p_ref[...] = jnp.dot(h_scr[...], wb_ref[...], preferred_element_type=F32).astype(BF16)


def _inproj_call(x2, ada3, norm_w, wa, wb, wg, gb, hnw, seq):
    T, D = x2.shape
    tm, tn = PROJ_TM, PROJ_TN
    tiles_per_seq = seq // tm
    return pl.pallas_call(
        _inproj_kernel,
        grid=(T // tm, N_COL_TILES),
        in_specs=[pl.BlockSpec((tm, D), lambda i, j: (i, 0)),
                  pl.BlockSpec((1, 3, D), lambda i, j: (i // tiles_per_seq, 0, 0)),
                  pl.BlockSpec((1, D), lambda i, j: (0, 0)),
                  pl.BlockSpec((D, tn), lambda i, j: (0, jnp.minimum(j, N_HEAD_TILES - 1))),
                  pl.BlockSpec((D, tn), lambda i, j: (0, jnp.maximum(j - N_HEAD_TILES, 0))),
                  pl.BlockSpec((D, LANES), lambda i, j: (0, 0)),
                  pl.BlockSpec((1, LANES), lambda i, j: (0, 0)),
                  pl.BlockSpec((2, tn), lambda i, j: (0, 0))],
        out_specs=[pl.BlockSpec((tm, tn), lambda i, j: (i, j)),
                   pl.BlockSpec((tm, LANES), lambda i, j: (i, 0)),
                   pl.BlockSpec((GATE_COLS, tm), lambda i, j: (0, i))],
        out_shape=[jax.ShapeDtypeStruct((T, N_COL_TILES * tn), BF16),
                   jax.ShapeDtypeStruct((T, LANES), F32),
                   jax.ShapeDtypeStruct((GATE_COLS, T), F32)],
        scratch_shapes=[pltpu.VMEM((tm, D), BF16)],
        compiler_params=pltpu.CompilerParams(dimension_semantics=("arbitrary", "arbitrary"),
                                             vmem_limit_bytes=VMEM_LIMIT),
        name="inproj",
    )(x2, ada3, norm_w, wa, wb, wg, gb, hnw)


def _moba_kernel(qa_ref, qb_ref, k_ref, v_ref, zaa_ref, zab_ref, bown_ref, bprev_ref, o_ref,
                 kmean_scr, vt_scr, q2_scr, selb_scr, s_scr, m8_scr):
    step = pl.program_id(2)
    blk, d = MOBA_BLOCK, ATT_HEAD_DIM
    da = d + BF16_SUBLANES
    nb = k_ref.shape[0] // blk
    heads = qa_ref.shape[1] // d
    tile = (step, nb - 1 - step)
    n_far = (jnp.maximum(step - 1, 0), nb - 2 - step)
    far_slots = nb - 2
    near_slots = 4

    def hsl(hd):
        return slice(hd * d, (hd + 1) * d)

    @pl.when(step == 0)
    def _():
        for n in range(nb):
            kb = k_ref[n * blk:(n + 1) * blk, :].astype(F32)
            kmean_scr[n:n + 1, :] = jnp.mean(kb, axis=0, keepdims=True)
            for hd in range(heads):
                vt_scr[n, hd * da:hd * da + d, :] = v_ref[n * blk:(n + 1) * blk, hsl(hd)].T
                vt_scr[n, hd * da + d:(hd + 1) * da, :] = jnp.ones((da - d, blk), BF16)

    q2_scr[0] = qa_ref[...]
    q2_scr[1] = qb_ref[...]

    def fold(x):
        return x.reshape(blk // SUBLANES, SUBLANES, blk)

    rows = lax.broadcasted_iota(jnp.int32, (nb, blk), 0)
    for x in range(2):
        valid = rows < tile[x]
        for hd in range(heads):
            gate = lax.dot_general(kmean_scr[:, hsl(hd)], q2_scr[x, :, hsl(hd)].astype(F32), _NT,
                                   precision=lax.Precision.HIGHEST, preferred_element_type=F32)
            g = jnp.where(valid, gate, -jnp.inf)
            selb = jnp.full(gate.shape, NEG_INF, F32)
            for _ in range(MOBA_TOPK):
                mx = jnp.max(g, axis=0, keepdims=True)
                first = jnp.min(jnp.where(g == mx, rows, nb), axis=0, keepdims=True)
                pick = rows == first
                selb = jnp.where(pick, 0.0, selb)
                g = jnp.where(pick, -jnp.inf, g)
            selb_scr[x, hd] = jnp.where(valid, selb, NEG_INF)

    def near_items(hd):
        out = []
        for x in range(2):
            jp = jnp.maximum(tile[x] - 1, 0)
            out.append((2 * x, x, tile[x], bown_ref[hd], None))
            out.append((2 * x + 1, x, jp, bprev_ref[hd] + selb_scr[x, hd, pl.ds(jp, 1), :], None))
        return out

    shared_slots = nb // 2 - 2

    def far_items(hd):
        out = []
        for t in range(far_slots):
            if t < shared_slots:
                is_a = t < n_far[0]
                x = jnp.where(is_a, 0, 1)
                j = jnp.where(is_a, t, t - n_far[0])
                live = j < jnp.where(is_a, n_far[0], n_far[1])
            else:
                is_a, x, j = None, 1, t - n_far[0]
                live = j < n_far[1]
            add = jnp.where(live, selb_scr[x, hd, pl.ds(j, 1), :], NEG_INF)
            out.append((near_slots + t, x, j, add, is_a))
        return out

    items = [near_items(hd) + far_items(hd) for hd in range(heads)]
    m8 = [[None, None] for _ in range(heads)]
    for i in range(near_slots + far_slots):
        for hd in range(heads):
            slot, x, j, add, is_a = items[hd][i]
            kj = k_ref[pl.ds(pl.multiple_of(j * blk, blk), blk), hsl(hd)]
            s = lax.dot_general(kj, q2_scr[x, :, hsl(hd)], _NT, preferred_element_type=F32) + add
            s_scr[hd, slot] = s
            mt = jnp.max(fold(s), axis=0)
            for y in range(2):
                if is_a is None:
                    if y != x:
                        continue
                    part = mt
                else:
                    part = jnp.where(is_a, mt, -jnp.inf) if y == 0 else jnp.where(is_a, -jnp.inf, mt)
                m8[hd][y] = part if m8[hd][y] is None else jnp.maximum(m8[hd][y], part)
    for hd in range(heads):
        for x in range(2):
            m8_scr[x, hd] = jnp.broadcast_to(jnp.max(m8[hd][x], axis=0, keepdims=True), (SUBLANES, blk))

    acc = [[None, None] for _ in range(heads)]
    acc_far = [None] * heads

    def add_to(lst, k, v):
        lst[k] = v if lst[k] is None else lst[k] + v

    for i in range(near_slots + far_slots):
        for hd in range(heads):
            slot, x, j, _, is_a = items[hd][i]
            p = jnp.exp2(fold(s_scr[hd, slot]) - m8_scr[x, hd])
            pb = p.reshape(blk, blk).astype(BF16)
            vt = vt_scr[j, hd * da:(hd + 1) * da, :]
            if is_a is None:
                add_to(acc[hd], x, jnp.dot(vt, pb, preferred_element_type=F32))
            else:
                zero = jnp.zeros_like(vt)
                lhs = jnp.concatenate([jnp.where(is_a, vt, zero), jnp.where(is_a, zero, vt)], axis=0)
                add_to(acc_far, hd, jnp.dot(lhs, pb, preferred_element_type=F32))

    for hd in range(heads):
        for x, za_ref in enumerate((zaa_ref, zab_ref)):
            tot = acc[hd][x] + acc_far[hd][x * da:(x + 1) * da]
            y = (tot[:d] * (1.0 / tot[d:d + 1])).T
            za = za_ref[:, hsl(hd)].astype(F32)
            o_ref[0, x, 0, :, hsl(hd)] = (y * (za * _sigmoid(za))).astype(BF16)


def _moba_call(p, bias_own, bias_prev, batch, seq):
    blk, d = MOBA_BLOCK, ATT_HEAD_DIM
    nq = seq // blk
    hg = MOBA_HEAD_GROUP
    w = hg * d
    gpt = PROJ_TN // w
    first = lambda col: (lambda b, h, i: (b * nq + i, col * gpt + h))
    last = lambda col: (lambda b, h, i: (b * nq + nq - 1 - i, col * gpt + h))
    return pl.pallas_call(
        _moba_kernel,
        grid=(batch, ATT_HEADS // hg, nq // 2),
        in_specs=[pl.BlockSpec((blk, w), first(COL_QA)),
                  pl.BlockSpec((blk, w), last(COL_QA)),
                  pl.BlockSpec((seq, w), lambda b, h, i: (b, COL_KA * gpt + h)),
                  pl.BlockSpec((seq, w), lambda b, h, i: (b, COL_VA * gpt + h)),
                  pl.BlockSpec((blk, w), first(COL_ZA)),
                  pl.BlockSpec((blk, w), last(COL_ZA)),
                  pl.BlockSpec((hg, blk, blk), lambda b, h, i: (h, 0, 0)),
                  pl.BlockSpec((hg, blk, blk), lambda b, h, i: (h, 0, 0))],
        out_specs=pl.BlockSpec((1, 2, 1, blk, w), lambda b, h, i: (b, 0, i, 0, h)),
        out_shape=jax.ShapeDtypeStruct((batch, 2, nq // 2, blk, ATT_HEADS * d), BF16),
        scratch_shapes=[pltpu.VMEM((nq, w), F32),
                        pltpu.VMEM((nq, hg * (d + BF16_SUBLANES), blk), BF16),
                        pltpu.VMEM((2, blk, w), BF16),
                        pltpu.VMEM((2, hg, nq, blk), F32),
                        pltpu.VMEM((hg, nq + 2, blk, blk), F32),
                        pltpu.VMEM((2, hg, SUBLANES, blk), F32)],
        compiler_params=pltpu.CompilerParams(
            dimension_semantics=("arbitrary", "arbitrary", "arbitrary"),
            vmem_limit_bytes=VMEM_LIMIT),
        name="moba",
    )(p, p, p, p, p, p, bias_own, bias_prev)


def _ua_tile_row(t, nq):
    half = nq // 2
    return jnp.where(t < half, t, half + (nq - 1 - t))


def _mlstm_kernel(qm_ref, km_ref, vm_ref, zm_ref, om_ref, g_ref, gt_ref, cw_ref, cb_ref, mlw_ref,
                  o_ref, c_scr, n_scr, m_scr, tailq_scr, tailk_scr, shift_scr):
    c_idx = pl.program_id(1)
    L = qm_ref.shape[0]
    W = ML_HEADS * ML_HEAD_DIM
    pad = SUBLANES

    r_i = lax.broadcasted_iota(jnp.int32, (L, L), 0)
    c_i = lax.broadcasted_iota(jnp.int32, (L, L), 1)

    @pl.when(c_idx == 0)
    def _():
        c_scr[...] = jnp.zeros_like(c_scr)
        n_scr[...] = jnp.zeros_like(n_scr)
        m_scr[...] = jnp.zeros_like(m_scr)
        tailq_scr[...] = jnp.zeros_like(tailq_scr)
        tailk_scr[...] = jnp.zeros_like(tailk_scr)
        for k in range(CONV_WIDTH):
            shift_scr[k] = (c_i == r_i - k).astype(F32).astype(BF16)

    row8 = lax.broadcasted_iota(jnp.int32, (pad, W), 0)

    def conv_silu(src_ref, tail_scr, col0):
        x = src_ref[...]
        tail = tail_scr[...]
        y = cb_ref[:, col0:col0 + W]
        fix = jnp.zeros((pad, W), F32)
        for w in range(CONV_WIDTH):
            k = CONV_WIDTH - 1 - w
            cw = cw_ref[w:w + 1, col0:col0 + W]
            y = y + cw * jnp.dot(shift_scr[k], x, preferred_element_type=F32)
            if k:
                fix = fix + cw * jnp.where(row8 < k, pltpu.roll(tail, k, axis=0), 0.0)
        tail_scr[...] = x[L - 2 * pad:, :].astype(F32)[pad:, :]
        y = jnp.concatenate([y[:pad] + fix, y[pad:]], axis=0)
        return y * _sigmoid(y)

    qc = conv_silu(qm_ref, tailq_scr, 0)
    kc = conv_silu(km_ref, tailk_scr, W) * (ML_HEAD_DIM ** -0.5)

    causal = c_i <= r_i
    ltri = causal.astype(F32)
    utri = (r_i <= c_i).astype(F32)
    g_col = g_ref[...]
    g_row = gt_ref[...]
    b_col_all = jnp.dot(ltri, _log_sigmoid(g_col), precision=lax.Precision.HIGHEST,
                        preferred_element_type=F32)
    b_row_all = jnp.dot(_log_sigmoid(g_row), utri, precision=lax.Precision.HIGHEST,
                        preferred_element_type=F32)

    for hh in range(ML_HEADS):
        sl = slice(hh * ML_HEAD_DIM, (hh + 1) * ML_HEAD_DIM)
        q = qc[:, sl]
        k = kc[:, sl]
        v = vm_ref[:, sl]
        b_col = b_col_all[:, ML_HEADS + hh:ML_HEADS + hh + 1]
        a_col = g_col[:, hh:hh + 1] - b_col
        a_row = g_row[hh:hh + 1, :] - b_row_all[ML_HEADS + hh:ML_HEADS + hh + 1, :]
        m_prev = m_scr[hh][0:1, 0:1]

        a_mask = jnp.where(causal, a_row, -jnp.inf)
        gcol = jnp.maximum(m_prev, jnp.max(a_mask, axis=1, keepdims=True))
        d_mat = jnp.exp(a_mask - gcol)
        inter = jnp.exp(m_prev - gcol)
        g_last = jnp.max(gcol, axis=0, keepdims=True)

        qb = q.astype(BF16)
        s = lax.dot_general(qb, k.astype(BF16), _NT, preferred_element_type=F32) * d_mat
        c_old = c_scr[hh]
        n_old = n_scr[hh]
        num = (inter * jnp.dot(qb, c_old.astype(BF16), preferred_element_type=F32)
               + jnp.dot(s.astype(BF16), v, preferred_element_type=F32))
        den = inter * jnp.sum(q * n_old, axis=1, keepdims=True) + jnp.sum(s, axis=1, keepdims=True)
        h = num / jnp.maximum(jnp.abs(den), jnp.exp(-(b_col + gcol)))

        decay = jnp.exp(m_prev - g_last)
        kw = k * jnp.exp(a_col - g_last)
        c_scr[hh] = decay * c_old + lax.dot_general(kw.astype(BF16), v, _TN,
                                                    preferred_element_type=F32)
        n_scr[hh] = decay * n_old + jnp.sum(kw, axis=0, keepdims=True)
        m_scr[hh] = jnp.broadcast_to(b_col[L - 1:L, :] + g_last, m_scr.shape[1:])

        hg = _sigmoid(om_ref[:, sl].astype(F32)) * h
        hn = hg * lax.rsqrt(jnp.mean(hg * hg, axis=-1, keepdims=True) + EPS) * mlw_ref[:, sl]
        z = zm_ref[:, sl].astype(F32)
        o_ref[:, sl] = (hn * (z * _sigmoid(z))).astype(BF16)


def _mlstm_call(p, g, gt, conv_w, conv_b, ml_norm_w, batch, seq):
    T = p.shape[0]
    L = ML_CHUNK
    nc = seq // L
    W = ML_HEADS * ML_HEAD_DIM
    row = lambda b, c: b * nc + c
    pspec = lambda col: pl.BlockSpec((L, W), lambda b, c: (row(b, c), col))
    return pl.pallas_call(
        _mlstm_kernel,
        grid=(batch, nc),
        in_specs=[pspec(COL_QM), pspec(COL_KM), pspec(COL_VM), pspec(COL_ZM), pspec(COL_OM),
                  pl.BlockSpec((L, LANES), lambda b, c: (row(b, c), 0)),
                  pl.BlockSpec((GATE_COLS, L), lambda b, c: (0, row(b, c))),
                  pl.BlockSpec((CONV_WIDTH, 2 * W), lambda b, c: (0, 0)),
                  pl.BlockSpec((1, 2 * W), lambda b, c: (0, 0)),
                  pl.BlockSpec((1, W), lambda b, c: (0, 0))],
        out_specs=pl.BlockSpec((L, W), lambda b, c: (row(b, c), 0)),
        out_shape=jax.ShapeDtypeStruct((T, W), BF16),
        scratch_shapes=[pltpu.VMEM((ML_HEADS, ML_HEAD_DIM, ML_HEAD_DIM), F32),
                        pltpu.VMEM((ML_HEADS, 1, ML_HEAD_DIM), F32),
                        pltpu.VMEM((ML_HEADS, SUBLANES, LANES), F32),
                        pltpu.VMEM((SUBLANES, W), F32),
                        pltpu.VMEM((SUBLANES, W), F32),
                        pltpu.VMEM((CONV_WIDTH, L, L), BF16)],
        compiler_params=pltpu.CompilerParams(dimension_semantics=("arbitrary", "arbitrary"),
                                             vmem_limit_bytes=VMEM_LIMIT),
        name="mlstm",
    )(p, p, p, p, p, g, gt, conv_w, conv_b, ml_norm_w)


def _outp_kernel(*refs):
    ua_refs, (um_ref, ga_ref, gm_ref, x_ref, ada_ref, wa_ref, wm_ref, wo_ref, o_ref) = refs[:-9], refs[-9:]
    ua = jnp.concatenate([r[...] for r in ua_refs], axis=0)
    ya = jnp.dot(ua, wa_ref[...], preferred_element_type=F32)
    ym = jnp.dot(um_ref[...], wm_ref[...], preferred_element_type=F32)
    y = _sigmoid(ga_ref[...].astype(F32)) * ya + _sigmoid(gm_ref[...].astype(F32)) * ym
    gate = ada_ref[0][2:3, :]
    o_ref[...] = x_ref[...] + gate * jnp.dot(y.astype(BF16), wo_ref[...], preferred_element_type=F32)


def _outp_call(ua, um, p, x2, ada3, wa, wm, wo, seq):
    T, D = x2.shape
    tm = OUT_TM
    steps_per_seq = seq // tm
    blk = MOBA_BLOCK
    nq = seq // blk
    tiles_per_step = tm // blk

    def ua_spec(e):
        def index(i):
            t = (i % steps_per_seq) * tiles_per_step + e
            return ((i // steps_per_seq) * nq + _ua_tile_row(t, nq), 0)
        return pl.BlockSpec((blk, D), index)

    act = lambda col: pl.BlockSpec((tm, D), lambda i: (i, col))
    wspec = pl.BlockSpec((D, D), lambda i: (0, 0))
    return pl.pallas_call(
        _outp_kernel,
        grid=(T // tm,),
        in_specs=[ua_spec(e) for e in range(tiles_per_step)]
                 + [act(0), act(COL_GA), act(COL_GM), act(0),
                    pl.BlockSpec((1, 3, D), lambda i: (i // steps_per_seq, 0, 0)),
                    wspec, wspec, wspec],
        out_specs=pl.BlockSpec((tm, D), lambda i: (i, 0)),
        out_shape=jax.ShapeDtypeStruct((T, D), F32),
        compiler_params=pltpu.CompilerParams(dimension_semantics=("arbitrary",),
                                             vmem_limit_bytes=VMEM_LIMIT),
        name="outp",
    )(*([ua] * tiles_per_step), um, p, p, x2, ada3, wa, wm, wo)


def _rel_bias_tables(rel_bias):
    max_exact = REL_BUCKETS // 2
    rb = rel_bias.astype(F32)
    rb = (rb - rb[REL_BUCKETS - 1:, :]) * LOG2E
    key = jnp.arange(MOBA_BLOCK)[:, None]
    qry = jnp.arange(MOBA_BLOCK)[None, :]
    d_own = qry - key

    def tile(dist):
        nf = jnp.maximum(dist, 1).astype(F32)
        large = max_exact + (jnp.log(nf / max_exact) / math.log(REL_MAX_DIST / max_exact)
                             * (REL_BUCKETS - max_exact)).astype(jnp.int32)
        bucket = jnp.where(dist < max_exact, dist, jnp.minimum(large, REL_BUCKETS - 1))
        onehot = (bucket[..., None] == jnp.arange(REL_BUCKETS)).astype(F32)
        return jnp.einsum('kqb,bh->hkq', onehot, rb, precision=lax.Precision.HIGHEST)

    own = jnp.where(d_own >= 0, tile(jnp.maximum(d_own, 0)), NEG_INF)
    prev = tile(d_own + MOBA_BLOCK)
    return own, prev


def _layer(x2, c, w_ada, b_ada, norm_w, w_in, q_norm_w, k_norm_w, rel_bias, conv_w, conv_b,
           b_igate, b_fgate, ml_norm_w, w_att_proj, w_ml_proj, w_out, batch, seq):
    D = D_MODEL
    g0 = 4 * D + 5 * D
    ada3 = _ada_call(c, w_ada, b_ada).reshape(batch, 3, D)

    wa = w_in[:, :g0].astype(BF16)
    wb = w_in[:, g0 + GATE_COLS:].astype(BF16)
    wg = jnp.pad(w_in[:, g0:g0 + GATE_COLS], ((0, 0), (0, LANES - GATE_COLS))).astype(BF16)
    gb = jnp.pad(jnp.concatenate([b_igate, b_fgate]), (0, LANES - GATE_COLS)).reshape(1, LANES)
    q_scale = ATT_HEAD_DIM ** -0.5 * LOG2E
    hnw = jnp.stack([jnp.tile(q_norm_w.astype(F32), ATT_HEADS) * q_scale,
                     jnp.tile(k_norm_w.astype(F32), ATT_HEADS)])

    p, g, gt = _inproj_call(x2, ada3, norm_w.reshape(1, D), wa, wb, wg, gb, hnw, seq)
    bias_own, bias_prev = _rel_bias_tables(rel_bias)
    ua = _moba_call(p, bias_own, bias_prev, batch, seq).reshape(batch * seq, D)
    um = _mlstm_call(p, g, gt, conv_w, conv_b.reshape(1, -1), ml_norm_w.reshape(1, -1), batch, seq)
    return _outp_call(ua, um, p, x2, ada3, w_att_proj.astype(BF16), w_ml_proj.astype(BF16),
                      w_out.astype(BF16), seq)


def kernel(x, c, w_ada, b_ada, norm_w, w_in, q_norm_w, k_norm_w, rel_bias, conv_w, conv_b, b_igate,
           b_fgate, ml_norm_w, w_att_proj, w_ml_proj, w_out):
    batch, seq, D = x.shape
    assert D == D_MODEL and seq % PROJ_TM == 0 and seq % MOBA_BLOCK == 0 and seq % ML_CHUNK == 0
    x2 = x.reshape(batch * seq, D)
    for l in range(w_in.shape[0]):
        x2 = _layer(x2, c, w_ada[l], b_ada[l], norm_w[l], w_in[l], q_norm_w[l], k_norm_w[l],
                    rel_bias, conv_w[l], conv_b[l], b_igate[l], b_fgate[l], ml_norm_w[l],
                    w_att_proj[l], w_ml_proj[l], w_out[l], batch, seq)
    return x2.reshape(batch, seq, D)
```

```python
import functools
import math

import jax
import jax.numpy as jnp
import numpy as np
from jax import lax
from jax.experimental import pallas as pl
from jax.experimental.pallas import tpu as pltpu

F32 = jnp.float32
BF16 = jnp.bfloat16

D_MODEL = 1024
ATT_HEADS = 8
ATT_HEAD_DIM = D_MODEL // ATT_HEADS
MOBA_BLOCK = 256
MOBA_TOPK = 3
REL_BUCKETS = 32
REL_MAX_DIST = 128
ML_HEADS = 4
ML_HEAD_DIM = D_MODEL // ML_HEADS
CONV_WIDTH = 4
EPS = 1e-6
NEG_INF = -1e30
LOG2E = math.log2(math.e)

LANES = 128
SUBLANES = 8
BF16_SUBLANES = 16
GATE_COLS = 2 * ML_HEADS
ML_CHUNK = 256
PROJ_TM = 1024
PROJ_TN = 1024
OUT_TM = 512
MOBA_HEAD_GROUP = 4
VMEM_LIMIT = 52 * 1024 * 1024

COL_QA, COL_KA, COL_VA, COL_ZA, COL_QM, COL_KM, COL_VM, COL_ZM, COL_OM, COL_GA, COL_GM = range(11)
N_COL_TILES = 11
N_HEAD_TILES = 9

_NT = (((1,), (1,)), ((), ()))
_TN = (((0,), (0,)), ((), ()))


def _sigmoid(x):
    return 1.0 / (1.0 + jnp.exp(-x))


def _log_sigmoid(x):
    return jnp.minimum(x, 0.0) - jnp.log1p(jnp.exp(-jnp.abs(x)))


def _ada_kernel(c_ref, w_ref, b_ref, o_ref):
    o_ref[...] = jnp.dot(c_ref[...], w_ref[...], precision=lax.Precision.HIGHEST,
                         preferred_element_type=F32) + b_ref[...]


def _ada_call(c, w_ada, b_ada):
    B, D = c.shape
    return pl.pallas_call(
        _ada_kernel,
        grid=(3,),
        in_specs=[pl.BlockSpec((B, D), lambda j: (0, 0)),
                  pl.BlockSpec((D, D), lambda j: (0, j)),
                  pl.BlockSpec((1, D), lambda j: (0, j))],
        out_specs=pl.BlockSpec((B, D), lambda j: (0, j)),
        out_shape=jax.ShapeDtypeStruct((B, 3 * D), F32),
        compiler_params=pltpu.CompilerParams(dimension_semantics=("arbitrary",),
                                             vmem_limit_bytes=VMEM_LIMIT),
        name="ada",
    )(c, w_ada, b_ada.reshape(1, 3 * D))


def _inproj_kernel(x_ref, ada_ref, nw_ref, wa_ref, wb_ref, wg_ref, gb_ref, hnw_ref,
                   p_ref, g_ref, gt_ref, h_scr):
    j = pl.program_id(1)
    tn = p_ref.shape[1]

    @pl.when(j == 0)
    def _():
        x = x_ref[...]
        ada = ada_ref[0]
        y = x * lax.rsqrt(jnp.mean(x * x, axis=-1, keepdims=True) + EPS) * nw_ref[...]
        h = (y * (1.0 + ada[1:2, :]) + ada[0:1, :]).astype(BF16)
        h_scr[...] = h
        g = jnp.dot(h, wg_ref[...], preferred_element_type=F32) + gb_ref[...]
        g_ref[...] = g
        gt_ref[...] = g.T[:GATE_COLS, :]

    @pl.when(j <= COL_KA)
    def _():
        h = h_scr[...]
        w = hnw_ref[pl.ds(j, 1), :]
        pair = 2 * ATT_HEAD_DIM
        for c0 in range(0, tn, pair):
            acc = jnp.dot(h, wa_ref[:, c0:c0 + pair], preferred_element_type=F32)
            for c in range(c0, c0 + pair, ATT_HEAD_DIM):
                a = acc[:, c - c0:c - c0 + ATT_HEAD_DIM]
                r = lax.rsqrt(jnp.mean(a * a, axis=-1, keepdims=True) + EPS)
                p_ref[:, c:c + ATT_HEAD_DIM] = (a * r * w[:, c:c + ATT_HEAD_DIM]).astype(BF16)

    @pl.when(jnp.logical_and(j > COL_KA, j < N_HEAD_TILES))
    def _():
        p_ref[...] = jnp.dot(h_scr[...], wa_ref[...], preferred_element_type=F32).astype(BF16)

    @pl.when(j >= N_HEAD_TILES)
    def _():
        p_ref[...] = jnp.dot(h_scr[...], wb_ref[...], preferred_element_type=F32).astype(BF16)


def _inproj_call(x2, ada3, norm_w, wa, wb, wg, gb, hnw, seq):
    T, D = x2.shape
    tm, tn = PROJ_TM, PROJ_TN
    tiles_per_seq = seq // tm
    return pl.pallas_call(
        _inproj_kernel,
        grid=(T // tm, N_COL_TILES),
        in_specs=[pl.BlockSpec((tm, D), lambda i, j: (i, 0)),
                  pl.BlockSpec((1, 3, D), lambda i, j: (i // tiles_per_seq, 0, 0)),
                  pl.BlockSpec((1, D), lambda i, j: (0, 0)),
                  pl.BlockSpec((D, tn), lambda i, j: (0, jnp.minimum(j, N_HEAD_TILES - 1))),
                  pl.BlockSpec((D, tn), lambda i, j: (0, jnp.maximum(j - N_HEAD_TILES, 0))),
                  pl.BlockSpec((D, LANES), lambda i, j: (0, 0)),
                  pl.BlockSpec((1, LANES), lambda i, j: (0, 0)),
                  pl.BlockSpec((2, tn), lambda i, j: (0, 0))],
        out_specs=[pl.BlockSpec((tm, tn), lambda i, j: (i, j)),
                   pl.BlockSpec((tm, LANES), lambda i, j: (i, 0)),
                   pl.BlockSpec((GATE_COLS, tm), lambda i, j: (0, i))],
        out_shape=[jax.ShapeDtypeStruct((T, N_COL_TILES * tn), BF16),
                   jax.ShapeDtypeStruct((T, LANES), F32),
                   jax.ShapeDtypeStruct((GATE_COLS, T), F32)],
        scratch_shapes=[pltpu.VMEM((tm, D), BF16)],
        compiler_params=pltpu.CompilerParams(dimension_semantics=("arbitrary", "arbitrary"),
                                             vmem_limit_bytes=VMEM_LIMIT),
        name="inproj",
    )(x2, ada3, norm_w, wa, wb, wg, gb, hnw)


def _moba_kernel(qa_ref, qb_ref, k_ref, v_ref, zaa_ref, zab_ref, bown_ref, bprev_ref, o_ref,
                 kmean_f32_scr, kmean_scr, vt_scr, q2_scr, selb_scr, s_scr, m8_scr):
    step = pl.program_id(2)
    blk, d = MOBA_BLOCK, ATT_HEAD_DIM
    da = d + BF16_SUBLANES
    nb = k_ref.shape[0] // blk
    heads = qa_ref.shape[1] // d
    tile = (step, nb - 1 - step)
    n_far = (jnp.maximum(step - 1, 0), nb - 2 - step)
    far_slots = nb - 2
    near_slots = 4

    def hsl(hd):
        return slice(hd * d, (hd + 1) * d)

    @pl.when(step == 0)
    def _():
        for n in range(nb):
            kb = k_ref[n * blk:(n + 1) * blk, :].astype(F32)
            kmean_f32_scr[n:n + 1, :] = jnp.mean(kb, axis=0, keepdims=True)
            for hd in range(heads):
                vt_scr[n, hd * da:hd * da + d, :] = v_ref[n * blk:(n + 1) * blk, hsl(hd)].T
                vt_scr[n, hd * da + d:(hd + 1) * da, :] = jnp.ones((da - d, blk), BF16)

        rest = kmean_f32_scr[...]
        for c in range(3):
            term = rest.astype(BF16)
            kmean_scr[c * nb:(c + 1) * nb, :] = term
            rest = rest - term.astype(F32)

    q2_scr[0] = qa_ref[...]
    q2_scr[1] = qb_ref[...]

    def fold(x):
        return x.reshape(blk // SUBLANES, SUBLANES, blk)

    def col_max8(s):
        x = fold(s)
        n = x.shape[0]
        while n > 1:
            n //= 2
            x = jnp.maximum(x[:n], x[n:2 * n])
        return x[0]

    rows = lax.broadcasted_iota(jnp.int32, (nb, blk), 0)
    for x in range(2):
        valid = rows < tile[x]
        for hd in range(heads):
            terms = lax.dot_general(kmean_scr[:, hsl(hd)], q2_scr[x, :, hsl(hd)], _NT,
                                    preferred_element_type=F32)
            gate = terms[:nb] + terms[nb:2 * nb] + terms[2 * nb:]
            g = jnp.where(valid, gate, -jnp.inf)
            selb = jnp.full(gate.shape, NEG_INF, F32)
            for _ in range(MOBA_TOPK):
                mx = jnp.max(g, axis=0, keepdims=True)
                first = jnp.min(jnp.where(g == mx, rows, nb), axis=0, keepdims=True)
                pick = rows == first
                selb = jnp.where(pick, 0.0, selb)
                g = jnp.where(pick, -jnp.inf, g)
            selb_scr[x, hd] = jnp.where(valid, selb, NEG_INF)

    def near_items(hd):
        out = []
        for x in range(2):
            jp = jnp.maximum(tile[x] - 1, 0)
            out.append((2 * x, x, tile[x], bown_ref[hd], None))
            out.append((2 * x + 1, x, jp, bprev_ref[hd] + selb_scr[x, hd, pl.ds(jp, 1), :], None))
        return out

    shared_slots = nb // 2 - 2

    def far_items(hd):
        out = []
        for t in range(far_slots):
            if t < shared_slots:
                is_a = t < n_far[0]
                x = jnp.where(is_a, 0, 1)
                j = jnp.where(is_a, t, t - n_far[0])
                live = j < jnp.where(is_a, n_far[0], n_far[1])
            else:
                is_a, x, j = None, 1, t - n_far[0]
                live = j < n_far[1]
            add = jnp.where(live, selb_scr[x, hd, pl.ds(j, 1), :], NEG_INF)
            out.append((near_slots + t, x, j, add, is_a))
        return out

    items = [near_items(hd) + far_items(hd) for hd in range(heads)]
    m8 = [[None, None] for _ in range(heads)]
    groups = [range(g, min(g + 2, heads)) for g in range(0, heads, 2)]
    order = [(i, grp) for grp in groups for i in range(near_slots + far_slots)]
    for i, grp in order:
        for hd in grp:
            slot, x, j, add, is_a = items[hd][i]
            kj = k_ref[pl.ds(pl.multiple_of(j * blk, blk), blk), hsl(hd)]
            s = lax.dot_general(kj, q2_scr[x, :, hsl(hd)], _NT, preferred_element_type=F32) + add
            s_scr[hd, slot] = s
            mt = col_max8(s)
            for y in range(2):
                if is_a is None:
                    if y != x:
                        continue
                    part = mt
                else:
                    part = jnp.where(is_a, mt, -jnp.inf) if y == 0 else jnp.where(is_a, -jnp.inf, mt)
                m8[hd][y] = part if m8[hd][y] is None else jnp.maximum(m8[hd][y], part)
    for hd in range(heads):
        for x in range(2):
            m8_scr[x, hd] = jnp.broadcast_to(jnp.max(m8[hd][x], axis=0, keepdims=True), (SUBLANES, blk))

    acc = [[None, None] for _ in range(heads)]
    acc_far = [None] * heads

    def add_to(lst, k, v):
        lst[k] = v if lst[k] is None else lst[k] + v

    for i, grp in order:
        for hd in grp:
            slot, x, j, _, is_a = items[hd][i]
            p = jnp.exp2(fold(s_scr[hd, slot]) - m8_scr[x, hd])
            pb = p.reshape(blk, blk).astype(BF16)
            vt = vt_scr[j, hd * da:(hd + 1) * da, :]
            if is_a is None:
                add_to(acc[hd], x, jnp.dot(vt, pb, preferred_element_type=F32))
            else:
                zero = jnp.zeros_like(vt)
                lhs = jnp.concatenate([jnp.where(is_a, vt, zero), jnp.where(is_a, zero, vt)], axis=0)
                add_to(acc_far, hd, jnp.dot(lhs, pb, preferred_element_type=F32))

    for hd in range(heads):
        for x, za_ref in enumerate((zaa_ref, zab_ref)):
            tot = acc[hd][x] + acc_far[hd][x * da:(x + 1) * da]
            y = (tot[:d] * (1.0 / tot[d:d + 1])).T
            za = za_ref[:, hsl(hd)].astype(F32)
            o_ref[0, x, 0, :, hsl(hd)] = (y * (za * _sigmoid(za))).astype(BF16)


def _moba_call(p, bias_own, bias_prev, batch, seq):
    blk, d = MOBA_BLOCK, ATT_HEAD_DIM
    nq = seq // blk
    hg = MOBA_HEAD_GROUP
    w = hg * d
    gpt = PROJ_TN // w
    first = lambda col: (lambda b, h, i: (b * nq + i, col * gpt + h))
    last = lambda col: (lambda b, h, i: (b * nq + nq - 1 - i, col * gpt + h))
    return pl.pallas_call(
        _moba_kernel,
        grid=(batch, ATT_HEADS // hg, nq // 2),
        in_specs=[pl.BlockSpec((blk, w), first(COL_QA)),
                  pl.BlockSpec((blk, w), last(COL_QA)),
                  pl.BlockSpec((seq, w), lambda b, h, i: (b, COL_KA * gpt + h)),
                  pl.BlockSpec((seq, w), lambda b, h, i: (b, COL_VA * gpt + h)),
                  pl.BlockSpec((blk, w), first(COL_ZA)),
                  pl.BlockSpec((blk, w), last(COL_ZA)),
                  pl.BlockSpec((hg, blk, blk), lambda b, h, i: (h, 0, 0)),
                  pl.BlockSpec((hg, blk, blk), lambda b, h, i: (h, 0, 0))],
        out_specs=pl.BlockSpec((1, 2, 1, blk, w), lambda b, h, i: (b, 0, i, 0, h)),
        out_shape=jax.ShapeDtypeStruct((batch, 2, nq // 2, blk, ATT_HEADS * d), BF16),
        scratch_shapes=[pltpu.VMEM((nq, w), F32),
                        pltpu.VMEM((3 * nq, w), BF16),
                        pltpu.VMEM((nq, hg * (d + BF16_SUBLANES), blk), BF16),
                        pltpu.VMEM((2, blk, w), BF16),
                        pltpu.VMEM((2, hg, nq, blk), F32),
                        pltpu.VMEM((hg, nq + 2, blk, blk), F32),
                        pltpu.VMEM((2, hg, SUBLANES, blk), F32)],
        compiler_params=pltpu.CompilerParams(
            dimension_semantics=("arbitrary", "arbitrary", "arbitrary"),
            vmem_limit_bytes=VMEM_LIMIT),
        name="moba",
    )(p, p, p, p, p, p, bias_own, bias_prev)


def _ua_tile_row(t, nq):
    half = nq // 2
    return jnp.where(t < half, t, half + (nq - 1 - t))


def _mlstm_kernel(qm_ref, km_ref, vm_ref, zm_ref, om_ref, g_ref, gt_ref, cw_ref, cb_ref, mlw_ref,
                  o_ref, c_scr, n_scr, m_scr, tailq_scr, tailk_scr, shift_scr):
    c_idx = pl.program_id(1)
    L = qm_ref.shape[0]
    W = ML_HEADS * ML_HEAD_DIM
    pad = SUBLANES

    r_i = lax.broadcasted_iota(jnp.int32, (L, L), 0)
    c_i = lax.broadcasted_iota(jnp.int32, (L, L), 1)

    @pl.when(c_idx == 0)
    def _():
        c_scr[...] = jnp.zeros_like(c_scr)
        n_scr[...] = jnp.zeros_like(n_scr)
        m_scr[...] = jnp.zeros_like(m_scr)
        tailq_scr[...] = jnp.zeros_like(tailq_scr)
        tailk_scr[...] = jnp.zeros_like(tailk_scr)
        for k in range(CONV_WIDTH):
            shift_scr[k] = (c_i == r_i - k).astype(F32).astype(BF16)

    row8 = lax.broadcasted_iota(jnp.int32, (pad, W), 0)

    def conv_silu(src_ref, tail_scr, col0):
        x = src_ref[...]
        tail = tail_scr[...]
        y = cb_ref[:, col0:col0 + W]
        fix = jnp.zeros((pad, W), F32)
        for w in range(CONV_WIDTH):
            k = CONV_WIDTH - 1 - w
            cw = cw_ref[w:w + 1, col0:col0 + W]
            y = y + cw * jnp.dot(shift_scr[k], x, preferred_element_type=F32)
            if k:
                fix = fix + cw * jnp.where(row8 < k, pltpu.roll(tail, k, axis=0), 0.0)
        tail_scr[...] = x[L - 2 * pad:, :].astype(F32)[pad:, :]
        y = jnp.concatenate([y[:pad] + fix, y[pad:]], axis=0)
        return y * _sigmoid(y)

    qc = conv_silu(qm_ref, tailq_scr, 0)
    kc = conv_silu(km_ref, tailk_scr, W) * (ML_HEAD_DIM ** -0.5)

    causal = c_i <= r_i
    ltri = causal.astype(F32)
    utri = (r_i <= c_i).astype(F32)
    g_col = g_ref[...]
    g_row = gt_ref[...]
    b_col_all = jnp.dot(ltri, _log_sigmoid(g_col), precision=lax.Precision.HIGHEST,
                        preferred_element_type=F32)
    b_row_all = jnp.dot(_log_sigmoid(g_row), utri, precision=lax.Precision.HIGHEST,
                        preferred_element_type=F32)

    for hh in range(ML_HEADS):
        sl = slice(hh * ML_HEAD_DIM, (hh + 1) * ML_HEAD_DIM)
        q = qc[:, sl]
        k = kc[:, sl]
        v = vm_ref[:, sl]
        b_col = b_col_all[:, ML_HEADS + hh:ML_HEADS + hh + 1]
        a_col = g_col[:, hh:hh + 1] - b_col
        a_row = g_row[hh:hh + 1, :] - b_row_all[ML_HEADS + hh:ML_HEADS + hh + 1, :]
        m_prev = m_scr[hh][0:1, 0:1]

        a_mask = jnp.where(causal, a_row, -jnp.inf)
        gcol = jnp.maximum(m_prev, jnp.max(a_mask, axis=1, keepdims=True))
        d_mat = jnp.exp(a_mask - gcol)
        inter = jnp.exp(m_prev - gcol)
        g_last = jnp.max(gcol, axis=0, keepdims=True)

        qb = q.astype(BF16)
        s = lax.dot_general(qb, k.astype(BF16), _NT, preferred_element_type=F32) * d_mat
        c_old = c_scr[hh]
        n_old = n_scr[hh]
        num = (inter * jnp.dot(qb, c_old.astype(BF16), preferred_element_type=F32)
               + jnp.dot(s.astype(BF16), v, preferred_element_type=F32))
        den = inter * jnp.sum(q * n_old, axis=1, keepdims=True) + jnp.sum(s, axis=1, keepdims=True)
        h = num / jnp.maximum(jnp.abs(den), jnp.exp(-(b_col + gcol)))

        decay = jnp.exp(m_prev - g_last)
        kw = k * jnp.exp(a_col - g_last)
        c_scr[hh] = decay * c_old + lax.dot_general(kw.astype(BF16), v, _TN,
                                                    preferred_element_type=F32)
        n_scr[hh] = decay * n_old + jnp.sum(kw, axis=0, keepdims=True)
        m_scr[hh] = jnp.broadcast_to(b_col[L - 1:L, :] + g_last, m_scr.shape[1:])

        hg = _sigmoid(om_ref[:, sl].astype(F32)) * h
        hn = hg * lax.rsqrt(jnp.mean(hg * hg, axis=-1, keepdims=True) + EPS) * mlw_ref[:, sl]
        z = zm_ref[:, sl].astype(F32)
        o_ref[:, sl] = (hn * (z * _sigmoid(z))).astype(BF16)


def _mlstm_call(p, g, gt, conv_w, conv_b, ml_norm_w, batch, seq):
    T = p.shape[0]
    L = ML_CHUNK
    nc = seq // L
    W = ML_HEADS * ML_HEAD_DIM
    row = lambda b, c: b * nc + c
    pspec = lambda col: pl.BlockSpec((L, W), lambda b, c: (row(b, c), col))
    return pl.pallas_call(
        _mlstm_kernel,
        grid=(batch, nc),
        in_specs=[pspec(COL_QM), pspec(COL_KM), pspec(COL_VM), pspec(COL_ZM), pspec(COL_OM),
                  pl.BlockSpec((L, LANES), lambda b, c: (row(b, c), 0)),
                  pl.BlockSpec((GATE_COLS, L), lambda b, c: (0, row(b, c))),
                  pl.BlockSpec((CONV_WIDTH, 2 * W), lambda b, c: (0, 0)),
                  pl.BlockSpec((1, 2 * W), lambda b, c: (0, 0)),
                  pl.BlockSpec((1, W), lambda b, c: (0, 0))],
        out_specs=pl.BlockSpec((L, W), lambda b, c: (row(b, c), 0)),
        out_shape=jax.ShapeDtypeStruct((T, W), BF16),
        scratch_shapes=[pltpu.VMEM((ML_HEADS, ML_HEAD_DIM, ML_HEAD_DIM), F32),
                        pltpu.VMEM((ML_HEADS, 1, ML_HEAD_DIM), F32),
                        pltpu.VMEM((ML_HEADS, SUBLANES, LANES), F32),
                        pltpu.VMEM((SUBLANES, W), F32),
                        pltpu.VMEM((SUBLANES, W), F32),
                        pltpu.VMEM((CONV_WIDTH, L, L), BF16)],
        compiler_params=pltpu.CompilerParams(dimension_semantics=("arbitrary", "arbitrary"),
                                             vmem_limit_bytes=VMEM_LIMIT),
        name="mlstm",
    )(p, p, p, p, p, g, gt, conv_w, conv_b, ml_norm_w)


def _outp_kernel(*refs):
    ua_refs, (um_ref, ga_ref, gm_ref, x_ref, ada_ref, wa_ref, wm_ref, wo_ref, o_ref) = refs[:-9], refs[-9:]
    ua = jnp.concatenate([r[...] for r in ua_refs], axis=0)
    ya = jnp.dot(ua, wa_ref[...], preferred_element_type=F32)
    ym = jnp.dot(um_ref[...], wm_ref[...], preferred_element_type=F32)
    y = _sigmoid(ga_ref[...].astype(F32)) * ya + _sigmoid(gm_ref[...].astype(F32)) * ym
    gate = ada_ref[0][2:3, :]
    o_ref[...] = x_ref[...] + gate * jnp.dot(y.astype(BF16), wo_ref[...], preferred_element_type=F32)


def _outp_call(ua, um, p, x2, ada3, wa, wm, wo, seq):
    T, D = x2.shape
    tm = OUT_TM
    steps_per_seq = seq // tm
    blk = MOBA_BLOCK
    nq = seq // blk
    tiles_per_step = tm // blk

    def ua_spec(e):
        def index(i):
            t = (i % steps_per_seq) * tiles_per_step + e
            return ((i // steps_per_seq) * nq + _ua_tile_row(t, nq), 0)
        return pl.BlockSpec((blk, D), index)

    act = lambda col: pl.BlockSpec((tm, D), lambda i: (i, col))
    wspec = pl.BlockSpec((D, D), lambda i: (0, 0))
    return pl.pallas_call(
        _outp_kernel,
        grid=(T // tm,),
        in_specs=[ua_spec(e) for e in range(tiles_per_step)]
                 + [act(0), act(COL_GA), act(COL_GM), act(0),
                    pl.BlockSpec((1, 3, D), lambda i: (i // steps_per_seq, 0, 0)),
                    wspec, wspec, wspec],
        out_specs=pl.BlockSpec((tm, D), lambda i: (i, 0)),
        out_shape=jax.ShapeDtypeStruct((T, D), F32),
        compiler_params=pltpu.CompilerParams(dimension_semantics=("arbitrary",),
                                             vmem_limit_bytes=VMEM_LIMIT),
        name="outp",
    )(*([ua] * tiles_per_step), um, p, p, x2, ada3, wa, wm, wo)


def _rel_bias_tables(rel_bias):
    max_exact = REL_BUCKETS // 2
    rb = rel_bias.astype(F32)
    rb = (rb - rb[REL_BUCKETS - 1:, :]) * LOG2E
    key = jnp.arange(MOBA_BLOCK)[:, None]
    qry = jnp.arange(MOBA_BLOCK)[None, :]
    d_own = qry - key

    def tile(dist):
        nf = jnp.maximum(dist, 1).astype(F32)
        large = max_exact + (jnp.log(nf / max_exact) / math.log(REL_MAX_DIST / max_exact)
                             * (REL_BUCKETS - max_exact)).astype(jnp.int32)
        bucket = jnp.where(dist < max_exact, dist, jnp.minimum(large, REL_BUCKETS - 1))
        onehot = (bucket[..., None] == jnp.arange(REL_BUCKETS)).astype(F32)
        return jnp.einsum('kqb,bh->hkq', onehot, rb, precision=lax.Precision.HIGHEST)

    own = jnp.where(d_own >= 0, tile(jnp.maximum(d_own, 0)), NEG_INF)
    prev = tile(d_own + MOBA_BLOCK)
    return own, prev


def _layer(x2, c, w_ada, b_ada, norm_w, w_in, q_norm_w, k_norm_w, rel_bias, conv_w, conv_b,
           b_igate, b_fgate, ml_norm_w, w_att_proj, w_ml_proj, w_out, batch, seq):
    D = D_MODEL
    g0 = 4 * D + 5 * D
    ada3 = _ada_call(c, w_ada, b_ada).reshape(batch, 3, D)

    wa = w_in[:, :g0].astype(BF16)
    wb = w_in[:, g0 + GATE_COLS:].astype(BF16)
    wg = jnp.pad(w_in[:, g0:g0 + GATE_COLS], ((0, 0), (0, LANES - GATE_COLS))).astype(BF16)
    gb = jnp.pad(jnp.concatenate([b_igate, b_fgate]), (0, LANES - GATE_COLS)).reshape(1, LANES)
    q_scale = ATT_HEAD_DIM ** -0.5 * LOG2E
    hnw = jnp.stack([jnp.tile(q_norm_w.astype(F32), ATT_HEADS) * q_scale,
                     jnp.tile(k_norm_w.astype(F32), ATT_HEADS)])

    p, g, gt = _inproj_call(x2, ada3, norm_w.reshape(1, D), wa, wb, wg, gb, hnw, seq)
    bias_own, bias_prev = _rel_bias_tables(rel_bias)
    ua = _moba_call(p, bias_own, bias_prev, batch, seq).reshape(batch * seq, D)
    um = _mlstm_call(p, g, gt, conv_w, conv_b.reshape(1, -1), ml_norm_w.reshape(1, -1), batch, seq)
    return _outp_call(ua, um, p, x2, ada3, w_att_proj.astype(BF16), w_ml_proj.astype(BF16),
                      w_out.astype(BF16), seq)


def kernel(x, c, w_ada, b_ada, norm_w, w_in, q_norm_w, k_norm_w, rel_bias, conv_w, conv_b, b_igate,
           b_fgate, ml_norm_w, w_att_proj, w_ml_proj, w_out):
    batch, seq, D = x.shape
    assert D == D_MODEL and seq % PROJ_TM == 0 and seq % MOBA_BLOCK == 0 and seq % ML_CHUNK == 0
    x2 = x.reshape(batch * seq, D)
    for l in range(w_in.shape[0]):
        x2 = _layer(x2, c, w_ada[l], b_ada[l], norm_w[l], w_in[l], q_norm_w[l], k_norm_w[l],
                    rel_bias, conv_w[l], conv_b[l], b_igate[l], b_fgate[l], ml_norm_w[l],
                    w_att_proj[l], w_ml_proj[l], w_out[l], batch, seq)
    return x2.reshape(batch, seq, D)
```

```python
import functools
import math

import jax
import jax.numpy as jnp
import numpy as np
from jax import lax
from jax.experimental import pallas as pl
from jax.experimental.pallas import tpu as pltpu

F32 = jnp.float32
BF16 = jnp.bfloat16

D_MODEL = 1024
ATT_HEADS = 8
ATT_HEAD_DIM = D_MODEL // ATT_HEADS
MOBA_BLOCK = 256
MOBA_TOPK = 3
REL_BUCKETS = 32
REL_MAX_DIST = 128
ML_HEADS = 4
ML_HEAD_DIM = D_MODEL // ML_HEADS
CONV_WIDTH = 4
EPS = 1e-6
NEG_INF = -1e30
LOG2E = math.log2(math.e)

LANES = 128
SUBLANES = 8
BF16_SUBLANES = 16
GATE_COLS = 2 * ML_HEADS
ML_CHUNK = 256
PROJ_TM = 1024
PROJ_TN = 1024
OUT_TM = 512
MOBA_HEAD_GROUP = 4
VMEM_LIMIT = 56 * 1024 * 1024

COL_QA, COL_KA, COL_VA, COL_ZA, COL_QM, COL_KM, COL_VM, COL_ZM, COL_OM, COL_GA, COL_GM = range(11)
N_COL_TILES = 11
N_HEAD_TILES = 9

_NT = (((1,), (1,)), ((), ()))
_TN = (((0,), (0,)), ((), ()))


def _sigmoid(x):
    return 1.0 / (1.0 + jnp.exp(-x))


def _log_sigmoid(x):
    return jnp.minimum(x, 0.0) - jnp.log1p(jnp.exp(-jnp.abs(x)))


def _ada_kernel(c_ref, w_ref, b_ref, o_ref):
    o_ref[...] = jnp.dot(c_ref[...], w_ref[...], precision=lax.Precision.HIGHEST,
                         preferred_element_type=F32) + b_ref[...]


def _ada_call(c, w_ada, b_ada):
    B, D = c.shape
    return pl.pallas_call(
        _ada_kernel,
        grid=(3,),
        in_specs=[pl.BlockSpec((B, D), lambda j: (0, 0)),
                  pl.BlockSpec((D, D), lambda j: (0, j)),
                  pl.BlockSpec((1, D), lambda j: (0, j))],
        out_specs=pl.BlockSpec((B, D), lambda j: (0, j)),
        out_shape=jax.ShapeDtypeStruct((B, 3 * D), F32),
        compiler_params=pltpu.CompilerParams(dimension_semantics=("arbitrary",),
                                             vmem_limit_bytes=VMEM_LIMIT),
        name="ada",
    )(c, w_ada, b_ada.reshape(1, 3 * D))


def _inproj_kernel(x_ref, ada_ref, nw_ref, wa_ref, wb_ref, wg_ref, gb_ref, hnw_ref,
                   p_ref, g_ref, gt_ref, h_scr):
    j = pl.program_id(1)
    tn = p_ref.shape[1]

    @pl.when(j == 0)
    def _():
        x = x_ref[...]
        ada = ada_ref[0]
        y = x * lax.rsqrt(jnp.mean(x * x, axis=-1, keepdims=True) + EPS) * nw_ref[...]
        h = (y * (1.0 + ada[1:2, :]) + ada[0:1, :]).astype(BF16)
        h_scr[...] = h
        g = jnp.dot(h, wg_ref[...], preferred_element_type=F32) + gb_ref[...]
        g_ref[...] = g
        gt_ref[...] = g.T[:GATE_COLS, :]

    @pl.when(j <= COL_KA)
    def _():
        h = h_scr[...]
        w = hnw_ref[pl.ds(j, 1), :]
        pair = 2 * ATT_HEAD_DIM
        for c0 in range(0, tn, pair):
            acc = jnp.dot(h, wa_ref[:, c0:c0 + pair], preferred_element_type=F32)
            for c in range(c0, c0 + pair, ATT_HEAD_DIM):
                a = acc[:, c - c0:c - c0 + ATT_HEAD_DIM]
                r = lax.rsqrt(jnp.mean(a * a, axis=-1, keepdims=True) + EPS)
                p_ref[:, c:c + ATT_HEAD_DIM] = (a * r * w[:, c:c + ATT_HEAD_DIM]).astype(BF16)

    @pl.when(jnp.logical_and(j > COL_KA, j < N_HEAD_TILES))
    def _():
        p_ref[...] = jnp.dot(h_scr[...], wa_ref[...], preferred_element_type=F32).astype(BF16)

    @pl.when(j >= N_HEAD_TILES)
    def _():
        p_ref[...] = jnp.dot(h_scr[...], wb_ref[...], preferred_element_type=F32).astype(BF16)


def _inproj_call(x2, ada3, norm_w, wa, wb, wg, gb, hnw, seq):
    T, D = x2.shape
    tm, tn = PROJ_TM, PROJ_TN
    tiles_per_seq = seq // tm
    return pl.pallas_call(
        _inproj_kernel,
        grid=(T // tm, N_COL_TILES),
        in_specs=[pl.BlockSpec((tm, D), lambda i, j: (i, 0)),
                  pl.BlockSpec((1, 3, D), lambda i, j: (i // tiles_per_seq, 0, 0)),
                  pl.BlockSpec((1, D), lambda i, j: (0, 0)),
                  pl.BlockSpec((D, tn), lambda i, j: (0, jnp.minimum(j, N_HEAD_TILES - 1))),
                  pl.BlockSpec((D, tn), lambda i, j: (0, jnp.maximum(j - N_HEAD_TILES, 0))),
                  pl.BlockSpec((D, LANES), lambda i, j: (0, 0)),
                  pl.BlockSpec((1, LANES), lambda i, j: (0, 0)),
                  pl.BlockSpec((2, tn), lambda i, j: (0, 0))],
        out_specs=[pl.BlockSpec((tm, tn), lambda i, j: (i, j)),
                   pl.BlockSpec((tm, LANES), lambda i, j: (i, 0)),
                   pl.BlockSpec((GATE_COLS, tm), lambda i, j: (0, i))],
        out_shape=[jax.ShapeDtypeStruct((T, N_COL_TILES * tn), BF16),
                   jax.ShapeDtypeStruct((T, LANES), F32),
                   jax.ShapeDtypeStruct((GATE_COLS, T), F32)],
        scratch_shapes=[pltpu.VMEM((tm, D), BF16)],
        compiler_params=pltpu.CompilerParams(dimension_semantics=("arbitrary", "arbitrary"),
                                             vmem_limit_bytes=VMEM_LIMIT),
        name="inproj",
    )(x2, ada3, norm_w, wa, wb, wg, gb, hnw)


def _moba_body(step, qa_ref, qb_ref, k_ref, v_ref, zaa_ref, zab_ref, bown_ref, bprev_ref, o_ref,
               kmean_f32_scr, kmean_scr, vt_scr, q2_scr, selb_scr, s_scr, m8_scr):
    blk, d = MOBA_BLOCK, ATT_HEAD_DIM
    da = d + BF16_SUBLANES
    nb = k_ref.shape[0] // blk
    heads = qa_ref.shape[1] // d
    tile = (step, nb - 1 - step)
    n_far = (jnp.maximum(step - 1, 0), nb - 2 - step)
    far_slots = nb - 2
    near_slots = 4

    def hsl(hd):
        return slice(hd * d, (hd + 1) * d)

    @pl.when(step == 0)
    def _():
        for n in range(nb):
            kb = k_ref[n * blk:(n + 1) * blk, :].astype(F32)
            kmean_f32_scr[n:n + 1, :] = jnp.mean(kb, axis=0, keepdims=True)
            for hd in range(heads):
                vt_scr[n, hd * da:hd * da + d, :] = v_ref[n * blk:(n + 1) * blk, hsl(hd)].T
                vt_scr[n, hd * da + d:(hd + 1) * da, :] = jnp.ones((da - d, blk), BF16)

        rest = kmean_f32_scr[...]
        for c in range(3):
            term = rest.astype(BF16)
            kmean_scr[c * nb:(c + 1) * nb, :] = term
            rest = rest - term.astype(F32)

    yield

    q2_scr[0] = qa_ref[...]
    q2_scr[1] = qb_ref[...]

    def fold(x):
        return x.reshape(blk // SUBLANES, SUBLANES, blk)

    def col_max8(s):
        x = fold(s)
        n = x.shape[0]
        while n > 1:
            n //= 2
            x = jnp.maximum(x[:n], x[n:2 * n])
        return x[0]

    rows = lax.broadcasted_iota(jnp.int32, (nb, blk), 0)
    for x in range(2):
        valid = rows < tile[x]
        for hd in range(heads):
            terms = lax.dot_general(kmean_scr[:, hsl(hd)], q2_scr[x, :, hsl(hd)], _NT,
                                    preferred_element_type=F32)
            gate = terms[:nb] + terms[nb:2 * nb] + terms[2 * nb:]
            g = jnp.where(valid, gate, -jnp.inf)
            selb = jnp.full(gate.shape, NEG_INF, F32)
            for _ in range(MOBA_TOPK):
                mx = jnp.max(g, axis=0, keepdims=True)
                first = jnp.min(jnp.where(g == mx, rows, nb), axis=0, keepdims=True)
                pick = rows == first
                selb = jnp.where(pick, 0.0, selb)
                g = jnp.where(pick, -jnp.inf, g)
            selb_scr[x, hd] = jnp.where(valid, selb, NEG_INF)

    def near_items(hd):
        out = []
        for x in range(2):
            jp = jnp.maximum(tile[x] - 1, 0)
            out.append((2 * x, x, tile[x], bown_ref[hd], None))
            out.append((2 * x + 1, x, jp, bprev_ref[hd] + selb_scr[x, hd, pl.ds(jp, 1), :], None))
        return out

    shared_slots = nb // 2 - 2

    def far_items(hd):
        out = []
        for t in range(far_slots):
            if t < shared_slots:
                is_a = t < n_far[0]
                x = jnp.where(is_a, 0, 1)
                j = jnp.where(is_a, t, t - n_far[0])
                live = j < jnp.where(is_a, n_far[0], n_far[1])
            else:
                is_a, x, j = None, 1, t - n_far[0]
                live = j < n_far[1]
            add = jnp.where(live, selb_scr[x, hd, pl.ds(j, 1), :], NEG_INF)
            out.append((near_slots + t, x, j, add, is_a))
        return out

    items = [near_items(hd) + far_items(hd) for hd in range(heads)]
    m8 = [[None, None] for _ in range(heads)]
    groups = [range(g, min(g + 2, heads)) for g in range(0, heads, 2)]
    order = [(i, grp) for grp in groups for i in range(near_slots + far_slots)]
    for i, grp in order:
        for hd in grp:
            slot, x, j, add, is_a = items[hd][i]
            kj = k_ref[pl.ds(pl.multiple_of(j * blk, blk), blk), hsl(hd)]
            s = lax.dot_general(kj, q2_scr[x, :, hsl(hd)], _NT, preferred_element_type=F32) + add
            s_scr[hd, slot] = s
            mt = col_max8(s)
            for y in range(2):
                if is_a is None:
                    if y != x:
                        continue
                    part = mt
                else:
                    part = jnp.where(is_a, mt, -jnp.inf) if y == 0 else jnp.where(is_a, -jnp.inf, mt)
                m8[hd][y] = part if m8[hd][y] is None else jnp.maximum(m8[hd][y], part)
    for hd in range(heads):
        for x in range(2):
            m8_scr[x, hd] = jnp.broadcast_to(jnp.max(m8[hd][x], axis=0, keepdims=True), (SUBLANES, blk))

    acc = [[None, None] for _ in range(heads)]
    acc_far = [None] * heads

    def add_to(lst, k, v):
        lst[k] = v if lst[k] is None else lst[k] + v

    for i, grp in order:
        for hd in grp:
            slot, x, j, _, is_a = items[hd][i]
            p = jnp.exp2(fold(s_scr[hd, slot]) - m8_scr[x, hd])
            pb = p.reshape(blk, blk).astype(BF16)
            vt = vt_scr[j, hd * da:(hd + 1) * da, :]
            if is_a is None:
                add_to(acc[hd], x, jnp.dot(vt, pb, preferred_element_type=F32))
            else:
                zero = jnp.zeros_like(vt)
                lhs = jnp.concatenate([jnp.where(is_a, vt, zero), jnp.where(is_a, zero, vt)], axis=0)
                add_to(acc_far, hd, jnp.dot(lhs, pb, preferred_element_type=F32))

    for hd in range(heads):
        for x, za_ref in enumerate((zaa_ref, zab_ref)):
            tot = acc[hd][x] + acc_far[hd][x * da:(x + 1) * da]
            y = (tot[:d] * (1.0 / tot[d:d + 1])).T
            za = za_ref[:, hsl(hd)].astype(F32)
            o_ref[0, x, 0, :, hsl(hd)] = (y * (za * _sigmoid(za))).astype(BF16)


def _ua_tile_row(t, nq):
    half = nq // 2
    return jnp.where(t < half, t, half + (nq - 1 - t))


def _mlstm_body(c_idx, qm_ref, km_ref, vm_ref, zm_ref, om_ref, g_ref, gt_ref, cw_ref, cb_ref, mlw_ref,
                o_ref, c_scr, n_scr, m_scr, tailq_scr, tailk_scr, shift_scr):
    L = qm_ref.shape[0]
    W = ML_HEADS * ML_HEAD_DIM
    pad = SUBLANES

    r_i = lax.broadcasted_iota(jnp.int32, (L, L), 0)
    c_i = lax.broadcasted_iota(jnp.int32, (L, L), 1)

    @pl.when(c_idx == 0)
    def _():
        c_scr[...] = jnp.zeros_like(c_scr)
        n_scr[...] = jnp.zeros_like(n_scr)
        m_scr[...] = jnp.zeros_like(m_scr)
        tailq_scr[...] = jnp.zeros_like(tailq_scr)
        tailk_scr[...] = jnp.zeros_like(tailk_scr)
        for k in range(CONV_WIDTH):
            shift_scr[k] = (c_i == r_i - k).astype(F32).astype(BF16)

    yield

    row8 = lax.broadcasted_iota(jnp.int32, (pad, W), 0)

    def conv_silu(src_ref, tail_scr, col0):
        x = src_ref[...]
        tail = tail_scr[...]
        y = cb_ref[:, col0:col0 + W]
        fix = jnp.zeros((pad, W), F32)
        for w in range(CONV_WIDTH):
            k = CONV_WIDTH - 1 - w
            cw = cw_ref[w:w + 1, col0:col0 + W]
            y = y + cw * jnp.dot(shift_scr[k], x, preferred_element_type=F32)
            if k:
                fix = fix + cw * jnp.where(row8 < k, pltpu.roll(tail, k, axis=0), 0.0)
        tail_scr[...] = x[L - 2 * pad:, :].astype(F32)[pad:, :]
        y = jnp.concatenate([y[:pad] + fix, y[pad:]], axis=0)
        return y * _sigmoid(y)

    qc = conv_silu(qm_ref, tailq_scr, 0)
    kc = conv_silu(km_ref, tailk_scr, W) * (ML_HEAD_DIM ** -0.5)

    causal = c_i <= r_i
    ltri = causal.astype(F32)
    utri = (r_i <= c_i).astype(F32)
    g_col = g_ref[...]
    g_row = gt_ref[...]
    b_col_all = jnp.dot(ltri, _log_sigmoid(g_col), precision=lax.Precision.HIGHEST,
                        preferred_element_type=F32)
    b_row_all = jnp.dot(_log_sigmoid(g_row), utri, precision=lax.Precision.HIGHEST,
                        preferred_element_type=F32)

    for hh in range(ML_HEADS):
        sl = slice(hh * ML_HEAD_DIM, (hh + 1) * ML_HEAD_DIM)
        q = qc[:, sl]
        k = kc[:, sl]
        v = vm_ref[:, sl]
        b_col = b_col_all[:, ML_HEADS + hh:ML_HEADS + hh + 1]
        a_col = g_col[:, hh:hh + 1] - b_col
        a_row = g_row[hh:hh + 1, :] - b_row_all[ML_HEADS + hh:ML_HEADS + hh + 1, :]
        m_prev = m_scr[hh][0:1, 0:1]

        a_mask = jnp.where(causal, a_row, -jnp.inf)
        gcol = jnp.maximum(m_prev, jnp.max(a_mask, axis=1, keepdims=True))
        d_mat = jnp.exp(a_mask - gcol)
        inter = jnp.exp(m_prev - gcol)
        g_last = jnp.max(gcol, axis=0, keepdims=True)

        qb = q.astype(BF16)
        s = lax.dot_general(qb, k.astype(BF16), _NT, preferred_element_type=F32) * d_mat
        c_old = c_scr[hh]
        n_old = n_scr[hh]
        num = (inter * jnp.dot(qb, c_old.astype(BF16), preferred_element_type=F32)
               + jnp.dot(s.astype(BF16), v, preferred_element_type=F32))
        den = inter * jnp.sum(q * n_old, axis=1, keepdims=True) + jnp.sum(s, axis=1, keepdims=True)
        h = num / jnp.maximum(jnp.abs(den), jnp.exp(-(b_col + gcol)))

        decay = jnp.exp(m_prev - g_last)
        kw = k * jnp.exp(a_col - g_last)
        c_scr[hh] = decay * c_old + lax.dot_general(kw.astype(BF16), v, _TN,
                                                    preferred_element_type=F32)
        n_scr[hh] = decay * n_old + jnp.sum(kw, axis=0, keepdims=True)
        m_scr[hh] = jnp.broadcast_to(b_col[L - 1:L, :] + g_last, m_scr.shape[1:])

        hg = _sigmoid(om_ref[:, sl].astype(F32)) * h
        hn = hg * lax.rsqrt(jnp.mean(hg * hg, axis=-1, keepdims=True) + EPS) * mlw_ref[:, sl]
        z = zm_ref[:, sl].astype(F32)
        o_ref[:, sl] = (hn * (z * _sigmoid(z))).astype(BF16)


N_MOBA_IN, N_MLSTM_IN = 8, 10
N_MOBA_SCRATCH = 7


def _mixers_kernel(*refs):
    n_in = N_MOBA_IN + N_MLSTM_IN
    moba_in, mlstm_in = refs[:N_MOBA_IN], refs[N_MOBA_IN:n_in]
    o_att, o_ml = refs[n_in:n_in + 2]
    scratch = refs[n_in + 2:]
    s = pl.program_id(1)
    moba_steps = moba_in[2].shape[0] // MOBA_BLOCK // 2
    bodies = [_moba_body(s % moba_steps, *moba_in, o_att, *scratch[:N_MOBA_SCRATCH]),
              _mlstm_body(s, *mlstm_in, o_ml, *scratch[N_MOBA_SCRATCH:])]
    for body in bodies:
        next(body)
    for body in reversed(bodies):
        for _ in body:
            pass


def _mixers_call(p, bias_own, bias_prev, g, gt, conv_w, conv_b, ml_norm_w, batch, seq):
    T = p.shape[0]
    blk, d = MOBA_BLOCK, ATT_HEAD_DIM
    nq = seq // blk
    hg = MOBA_HEAD_GROUP
    w = hg * d
    gpt = PROJ_TN // w
    half = nq // 2
    L = ML_CHUNK
    nc = seq // L
    W = ML_HEADS * ML_HEAD_DIM
    assert nc == half * (ATT_HEADS // hg), "one mLSTM chunk per MoBA step"

    first = lambda col: (lambda b, s: (b * nq + s % half, col * gpt + s // half))
    last = lambda col: (lambda b, s: (b * nq + nq - 1 - s % half, col * gpt + s // half))
    whole = lambda col: (lambda b, s: (b, col * gpt + s // half))
    once = pl.Buffered(1)
    row = lambda b, s: b * nc + s
    pspec = lambda col: pl.BlockSpec((L, W), lambda b, s: (row(b, s), col))
    return pl.pallas_call(
        _mixers_kernel,
        grid=(batch, nc),
        in_specs=[pl.BlockSpec((blk, w), first(COL_QA)),
                  pl.BlockSpec((blk, w), last(COL_QA)),
                  pl.BlockSpec((seq, w), whole(COL_KA), pipeline_mode=once),
                  pl.BlockSpec((seq, w), whole(COL_VA), pipeline_mode=once),
                  pl.BlockSpec((blk, w), first(COL_ZA)),
                  pl.BlockSpec((blk, w), last(COL_ZA)),
                  pl.BlockSpec((hg, blk, blk), lambda b, s: (s // half, 0, 0), pipeline_mode=once),
                  pl.BlockSpec((hg, blk, blk), lambda b, s: (s // half, 0, 0), pipeline_mode=once),
                  pspec(COL_QM), pspec(COL_KM), pspec(COL_VM), pspec(COL_ZM), pspec(COL_OM),
                  pl.BlockSpec((L, LANES), lambda b, s: (row(b, s), 0)),
                  pl.BlockSpec((GATE_COLS, L), lambda b, s: (0, row(b, s))),
                  pl.BlockSpec((CONV_WIDTH, 2 * W), lambda b, s: (0, 0)),
                  pl.BlockSpec((1, 2 * W), lambda b, s: (0, 0)),
                  pl.BlockSpec((1, W), lambda b, s: (0, 0))],
        out_specs=[pl.BlockSpec((1, 2, 1, blk, w), lambda b, s: (b, 0, s % half, 0, s // half)),
                   pl.BlockSpec((L, W), lambda b, s: (row(b, s), 0))],
        out_shape=[jax.ShapeDtypeStruct((batch, 2, half, blk, ATT_HEADS * d), BF16),
                   jax.ShapeDtypeStruct((T, W), BF16)],
        scratch_shapes=[pltpu.VMEM((nq, w), F32),
                        pltpu.VMEM((3 * nq, w), BF16),
                        pltpu.VMEM((nq, hg * (d + BF16_SUBLANES), blk), BF16),
                        pltpu.VMEM((2, blk, w), BF16),
                        pltpu.VMEM((2, hg, nq, blk), F32),
                        pltpu.VMEM((hg, nq + 2, blk, blk), F32),
                        pltpu.VMEM((2, hg, SUBLANES, blk), F32),
                        pltpu.VMEM((ML_HEADS, ML_HEAD_DIM, ML_HEAD_DIM), F32),
                        pltpu.VMEM((ML_HEADS, 1, ML_HEAD_DIM), F32),
                        pltpu.VMEM((ML_HEADS, SUBLANES, LANES), F32),
                        pltpu.VMEM((SUBLANES, W), F32),
                        pltpu.VMEM((SUBLANES, W), F32),
                        pltpu.VMEM((CONV_WIDTH, L, L), BF16)],
        compiler_params=pltpu.CompilerParams(dimension_semantics=("arbitrary", "arbitrary"),
                                             vmem_limit_bytes=VMEM_LIMIT),
        name="mixers",
    )(p, p, p, p, p, p, bias_own, bias_prev, p, p, p, p, p, g, gt, conv_w, conv_b, ml_norm_w)


def _outp_kernel(*refs):
    ua_refs, (um_ref, ga_ref, gm_ref, x_ref, ada_ref, wa_ref, wm_ref, wo_ref, o_ref) = refs[:-9], refs[-9:]
    ua = jnp.concatenate([r[...] for r in ua_refs], axis=0)
    ya = jnp.dot(ua, wa_ref[...], preferred_element_type=F32)
    ym = jnp.dot(um_ref[...], wm_ref[...], preferred_element_type=F32)
    y = _sigmoid(ga_ref[...].astype(F32)) * ya + _sigmoid(gm_ref[...].astype(F32)) * ym
    gate = ada_ref[0][2:3, :]
    o_ref[...] = x_ref[...] + gate * jnp.dot(y.astype(BF16), wo_ref[...], preferred_element_type=F32)


def _outp_call(ua, um, p, x2, ada3, wa, wm, wo, seq):
    T, D = x2.shape
    tm = OUT_TM
    steps_per_seq = seq // tm
    blk = MOBA_BLOCK
    nq = seq // blk
    tiles_per_step = tm // blk

    def ua_spec(e):
        def index(i):
            t = (i % steps_per_seq) * tiles_per_step + e
            return ((i // steps_per_seq) * nq + _ua_tile_row(t, nq), 0)
        return pl.BlockSpec((blk, D), index)

    act = lambda col: pl.BlockSpec((tm, D), lambda i: (i, col))
    wspec = pl.BlockSpec((D, D), lambda i: (0, 0))
    return pl.pallas_call(
        _outp_kernel,
        grid=(T // tm,),
        in_specs=[ua_spec(e) for e in range(tiles_per_step)]
                 + [act(0), act(COL_GA), act(COL_GM), act(0),
                    pl.BlockSpec((1, 3, D), lambda i: (i // steps_per_seq, 0, 0)),
                    wspec, wspec, wspec],
        out_specs=pl.BlockSpec((tm, D), lambda i: (i, 0)),
        out_shape=jax.ShapeDtypeStruct((T, D), F32),
        compiler_params=pltpu.CompilerParams(dimension_semantics=("arbitrary",),
                                             vmem_limit_bytes=VMEM_LIMIT),
        name="outp",
    )(*([ua] * tiles_per_step), um, p, p, x2, ada3, wa, wm, wo)


def _rel_bias_tables(rel_bias):
    max_exact = REL_BUCKETS // 2
    rb = rel_bias.astype(F32)
    rb = (rb - rb[REL_BUCKETS - 1:, :]) * LOG2E
    key = jnp.arange(MOBA_BLOCK)[:, None]
    qry = jnp.arange(MOBA_BLOCK)[None, :]
    d_own = qry - key

    def tile(dist):
        nf = jnp.maximum(dist, 1).astype(F32)
        large = max_exact + (jnp.log(nf / max_exact) / math.log(REL_MAX_DIST / max_exact)
                             * (REL_BUCKETS - max_exact)).astype(jnp.int32)
        bucket = jnp.where(dist < max_exact, dist, jnp.minimum(large, REL_BUCKETS - 1))
        onehot = (bucket[..., None] == jnp.arange(REL_BUCKETS)).astype(F32)
        return jnp.einsum('kqb,bh->hkq', onehot, rb, precision=lax.Precision.HIGHEST)

    own = jnp.where(d_own >= 0, tile(jnp.maximum(d_own, 0)), NEG_INF)
    prev = tile(d_own + MOBA_BLOCK)
    return own, prev


def _layer(x2, c, w_ada, b_ada, norm_w, w_in, q_norm_w, k_norm_w, rel_bias, conv_w, conv_b,
           b_igate, b_fgate, ml_norm_w, w_att_proj, w_ml_proj, w_out, batch, seq):
    D = D_MODEL
    g0 = 4 * D + 5 * D
    ada3 = _ada_call(c, w_ada, b_ada).reshape(batch, 3, D)

    wa = w_in[:, :g0].astype(BF16)
    wb = w_in[:, g0 + GATE_COLS:].astype(BF16)
    wg = jnp.pad(w_in[:, g0:g0 + GATE_COLS], ((0, 0), (0, LANES - GATE_COLS))).astype(BF16)
    gb = jnp.pad(jnp.concatenate([b_igate, b_fgate]), (0, LANES - GATE_COLS)).reshape(1, LANES)
    q_scale = ATT_HEAD_DIM ** -0.5 * LOG2E
    hnw = jnp.stack([jnp.tile(q_norm_w.astype(F32), ATT_HEADS) * q_scale,
                     jnp.tile(k_norm_w.astype(F32), ATT_HEADS)])

    p, g, gt = _inproj_call(x2, ada3, norm_w.reshape(1, D), wa, wb, wg, gb, hnw, seq)
    bias_own, bias_prev = _rel_bias_tables(rel_bias)
    ua, um = _mixers_call(p, bias_own, bias_prev, g, gt, conv_w, conv_b.reshape(1, -1),
                          ml_norm_w.reshape(1, -1), batch, seq)
    ua = ua.reshape(batch * seq, D)
    return _outp_call(ua, um, p, x2, ada3, w_att_proj.astype(BF16), w_ml_proj.astype(BF16),
                      w_out.astype(BF16), seq)


def kernel(x, c, w_ada, b_ada, norm_w, w_in, q_norm_w, k_norm_w, rel_bias, conv_w, conv_b, b_igate,
           b_fgate, ml_norm_w, w_att_proj, w_ml_proj, w_out):
    batch, seq, D = x.shape
    assert D == D_MODEL and seq % PROJ_TM == 0 and seq % MOBA_BLOCK == 0 and seq % ML_CHUNK == 0
    x2 = x.reshape(batch * seq, D)
    for l in range(w_in.shape[0]):
        x2 = _layer(x2, c, w_ada[l], b_ada[l], norm_w[l], w_in[l], q_norm_w[l], k_norm_w[l],
                    rel_bias, conv_w[l], conv_b[l], b_igate[l], b_fgate[l], ml_norm_w[l],
                    w_att_proj[l], w_ml_proj[l], w_out[l], batch, seq)
    return x2.reshape(batch, seq, D)
```

```python
import functools
import math

import jax
import jax.numpy as jnp
import numpy as np
from jax import lax
from jax.experimental import pallas as pl
from jax.experimental.pallas import tpu as pltpu

F32 = jnp.float32
BF16 = jnp.bfloat16

D_MODEL = 1024
ATT_HEADS = 8
ATT_HEAD_DIM = D_MODEL // ATT_HEADS
MOBA_BLOCK = 256
MOBA_TOPK = 3
REL_BUCKETS = 32
REL_MAX_DIST = 128
ML_HEADS = 4
ML_HEAD_DIM = D_MODEL // ML_HEADS
CONV_WIDTH = 4
EPS = 1e-6
NEG_INF = -1e30
LOG2E = math.log2(math.e)

LANES = 128
SUBLANES = 8
BF16_SUBLANES = 16
GATE_COLS = 2 * ML_HEADS
ML_CHUNK = 256
PROJ_TM = 1024
PROJ_TN = 1024
OUT_TM = 512
MOBA_HEAD_GROUP = 4
VMEM_LIMIT = 52 * 1024 * 1024

COL_QA, COL_KA, COL_VA, COL_ZA, COL_QM, COL_KM, COL_VM, COL_ZM, COL_OM, COL_GA, COL_GM = range(11)
N_COL_TILES = 11
N_HEAD_TILES = 9

_NT = (((1,), (1,)), ((), ()))
_TN = (((0,), (0,)), ((), ()))


def _sigmoid(x):
    return 1.0 / (1.0 + jnp.exp(-x))


def _log_sigmoid(x):
    return jnp.minimum(x, 0.0) - jnp.log1p(jnp.exp(-jnp.abs(x)))


def _ada_kernel(c_ref, w_ref, b_ref, o_ref):
    o_ref[...] = jnp.dot(c_ref[...], w_ref[...], precision=lax.Precision.HIGHEST,
                         preferred_element_type=F32) + b_ref[...]


def _ada_call(c, w_ada, b_ada):
    B, D = c.shape
    return pl.pallas_call(
        _ada_kernel,
        grid=(3,),
        in_specs=[pl.BlockSpec((B, D), lambda j: (0, 0)),
                  pl.BlockSpec((D, D), lambda j: (0, j)),
                  pl.BlockSpec((1, D), lambda j: (0, j))],
        out_specs=pl.BlockSpec((B, D), lambda j: (0, j)),
        out_shape=jax.ShapeDtypeStruct((B, 3 * D), F32),
        compiler_params=pltpu.CompilerParams(dimension_semantics=("arbitrary",),
                                             vmem_limit_bytes=VMEM_LIMIT),
        name="ada",
    )(c, w_ada, b_ada.reshape(1, 3 * D))


def _inproj_kernel(x_ref, ada_ref, nw_ref, wa_ref, wb_ref, wg_ref, gb_ref, hnw_ref,
                   p_ref, g_ref, gt_ref, h_scr):
    j = pl.program_id(1)
    tn = p_ref.shape[1]

    @pl.when(j == 0)
    def _():
        x = x_ref[...]
        ada = ada_ref[0]
        y = x * lax.rsqrt(jnp.mean(x * x, axis=-1, keepdims=True) + EPS) * nw_ref[...]
        h = (y * (1.0 + ada[1:2, :]) + ada[0:1, :]).astype(BF16)
        h_scr[...] = h
        g = jnp.dot(h, wg_ref[...], preferred_element_type=F32) + gb_ref[...]
        g_ref[...] = g
        gt_ref[...] = g.T[:GATE_COLS, :]

    @pl.when(j <= COL_KA)
    def _():
        h = h_scr[...]
        w = hnw_ref[pl.ds(j, 1), :]
        pair = 2 * ATT_HEAD_DIM
        for c0 in range(0, tn, pair):
            acc = jnp.dot(h, wa_ref[:, c0:c0 + pair].astype(BF16), preferred_element_type=F32)
            for c in range(c0, c0 + pair, ATT_HEAD_DIM):
                a = acc[:, c - c0:c - c0 + ATT_HEAD_DIM]
                r = lax.rsqrt(jnp.mean(a * a, axis=-1, keepdims=True) + EPS)
                p_ref[:, c:c + ATT_HEAD_DIM] = (a * r * w[:, c:c + ATT_HEAD_DIM]).astype(BF16)

    @pl.when(jnp.logical_and(j > COL_KA, j < N_HEAD_TILES))
    def _():
        p_ref[...] = jnp.dot(h_scr[...], wa_ref[...].astype(BF16),
                             preferred_element_type=F32).astype(BF16)

    @pl.when(j >= N_HEAD_TILES)
    def _():
        p_ref[...] = jnp.dot(h_scr[...], wb_ref[...], preferred_element_type=F32).astype(BF16)


def _inproj_call(x2, ada3, norm_w, w_in, layer, wb, wg, gb, hnw, seq):
    T, D = x2.shape
    tm, tn = PROJ_TM, PROJ_TN
    tiles_per_seq = seq // tm
    return pl.pallas_call(
        _inproj_kernel,
        grid=(T // tm, N_COL_TILES),
        in_specs=[pl.BlockSpec((tm, D), lambda i, j: (i, 0)),
                  pl.BlockSpec((1, 3, D), lambda i, j: (i // tiles_per_seq, 0, 0)),
                  pl.BlockSpec((1, D), lambda i, j: (0, 0)),
                  pl.BlockSpec((None, D, tn), lambda i, j: (layer, 0, jnp.minimum(j, N_HEAD_TILES - 1))),
                  pl.BlockSpec((D, tn), lambda i, j: (0, jnp.maximum(j - N_HEAD_TILES, 0))),
                  pl.BlockSpec((D, LANES), lambda i, j: (0, 0)),
                  pl.BlockSpec((1, LANES), lambda i, j: (0, 0)),
                  pl.BlockSpec((2, tn), lambda i, j: (0, 0))],
        out_specs=[pl.BlockSpec((tm, tn), lambda i, j: (i, j)),
                   pl.BlockSpec((tm, LANES), lambda i, j: (i, 0)),
                   pl.BlockSpec((GATE_COLS, tm), lambda i, j: (0, i))],
        out_shape=[jax.ShapeDtypeStruct((T, N_COL_TILES * tn), BF16),
                   jax.ShapeDtypeStruct((T, LANES), F32),
                   jax.ShapeDtypeStruct((GATE_COLS, T), F32)],
        scratch_shapes=[pltpu.VMEM((tm, D), BF16)],
        compiler_params=pltpu.CompilerParams(dimension_semantics=("arbitrary", "arbitrary"),
                                             vmem_limit_bytes=VMEM_LIMIT),
        name="inproj",
    )(x2, ada3, norm_w, w_in, wb, wg, gb, hnw)


def _moba_kernel(qa_ref, qb_ref, k_ref, v_ref, zaa_ref, zab_ref, bown_ref, bprev_ref, o_ref,
                 kmean_f32_scr, kmean_scr, vt_scr, q2_scr, selb_scr, s_scr, m8_scr):
    step = pl.program_id(2)
    blk, d = MOBA_BLOCK, ATT_HEAD_DIM
    da = d + BF16_SUBLANES
    nb = k_ref.shape[0] // blk
    heads = qa_ref.shape[1] // d
    tile = (step, nb - 1 - step)
    n_far = (jnp.maximum(step - 1, 0), nb - 2 - step)
    far_slots = nb - 2
    near_slots = 4

    def hsl(hd):
        return slice(hd * d, (hd + 1) * d)

    @pl.when(step == 0)
    def _():
        for n in range(nb):
            kb = k_ref[n * blk:(n + 1) * blk, :].astype(F32)
            kmean_f32_scr[n:n + 1, :] = jnp.mean(kb, axis=0, keepdims=True)
            for hd in range(heads):
                vt_scr[n, hd * da:hd * da + d, :] = v_ref[n * blk:(n + 1) * blk, hsl(hd)].T
                vt_scr[n, hd * da + d:(hd + 1) * da, :] = jnp.ones((da - d, blk), BF16)

        rest = kmean_f32_scr[...]
        for c in range(3):
            term = rest.astype(BF16)
            kmean_scr[c * nb:(c + 1) * nb, :] = term
            rest = rest - term.astype(F32)

    q2_scr[0] = qa_ref[...]
    q2_scr[1] = qb_ref[...]

    def fold(x):
        return x.reshape(blk // SUBLANES, SUBLANES, blk)

    def col_max8(s):
        x = fold(s)
        n = x.shape[0]
        while n > 1:
            n //= 2
            x = jnp.maximum(x[:n], x[n:2 * n])
        return x[0]

    rows = lax.broadcasted_iota(jnp.int32, (nb, blk), 0)
    for x in range(2):
        valid = rows < tile[x]
        for hd in range(heads):
            terms = lax.dot_general(kmean_scr[:, hsl(hd)], q2_scr[x, :, hsl(hd)], _NT,
                                    preferred_element_type=F32)
            gate = terms[:nb] + terms[nb:2 * nb] + terms[2 * nb:]
            g = jnp.where(valid, gate, -jnp.inf)
            selb = jnp.full(gate.shape, NEG_INF, F32)
            for _ in range(MOBA_TOPK):
                mx = jnp.max(g, axis=0, keepdims=True)
                first = jnp.min(jnp.where(g == mx, rows, nb), axis=0, keepdims=True)
                pick = rows == first
                selb = jnp.where(pick, 0.0, selb)
                g = jnp.where(pick, -jnp.inf, g)
            selb_scr[x, hd] = jnp.where(valid, selb, NEG_INF)

    def near_items(hd):
        out = []
        for x in range(2):
            jp = jnp.maximum(tile[x] - 1, 0)
            out.append((2 * x, x, tile[x], bown_ref[hd], None))
            out.append((2 * x + 1, x, jp, bprev_ref[hd] + selb_scr[x, hd, pl.ds(jp, 1), :], None))
        return out

    shared_slots = nb // 2 - 2

    def far_items(hd):
        out = []
        for t in range(far_slots):
            if t < shared_slots:
                is_a = t < n_far[0]
                x = jnp.where(is_a, 0, 1)
                j = jnp.where(is_a, t, t - n_far[0])
                live = j < jnp.where(is_a, n_far[0], n_far[1])
            else:
                is_a, x, j = None, 1, t - n_far[0]
                live = j < n_far[1]
            add = jnp.where(live, selb_scr[x, hd, pl.ds(j, 1), :], NEG_INF)
            out.append((near_slots + t, x, j, add, is_a))
        return out

    items = [near_items(hd) + far_items(hd) for hd in range(heads)]
    m8 = [[None, None] for _ in range(heads)]
    groups = [range(g, min(g + 2, heads)) for g in range(0, heads, 2)]
    order = [(i, grp) for grp in groups for i in range(near_slots + far_slots)]
    for i, grp in order:
        for hd in grp:
            slot, x, j, add, is_a = items[hd][i]
            kj = k_ref[pl.ds(pl.multiple_of(j * blk, blk), blk), hsl(hd)]
            s = lax.dot_general(kj, q2_scr[x, :, hsl(hd)], _NT, preferred_element_type=F32) + add
            s_scr[hd, slot] = s
            mt = col_max8(s)
            for y in range(2):
                if is_a is None:
                    if y != x:
                        continue
                    part = mt
                else:
                    part = jnp.where(is_a, mt, -jnp.inf) if y == 0 else jnp.where(is_a, -jnp.inf, mt)
                m8[hd][y] = part if m8[hd][y] is None else jnp.maximum(m8[hd][y], part)
    for hd in range(heads):
        for x in range(2):
            m8_scr[x, hd] = jnp.broadcast_to(jnp.max(m8[hd][x], axis=0, keepdims=True), (SUBLANES, blk))

    acc = [[None, None] for _ in range(heads)]
    acc_far = [None] * heads

    def add_to(lst, k, v):
        lst[k] = v if lst[k] is None else lst[k] + v

    for i, grp in order:
        for hd in grp:
            slot, x, j, _, is_a = items[hd][i]
            p = jnp.exp2(fold(s_scr[hd, slot]) - m8_scr[x, hd])
            pb = p.reshape(blk, blk).astype(BF16)
            vt = vt_scr[j, hd * da:(hd + 1) * da, :]
            if is_a is None:
                add_to(acc[hd], x, jnp.dot(vt, pb, preferred_element_type=F32))
            else:
                zero = jnp.zeros_like(vt)
                lhs = jnp.concatenate([jnp.where(is_a, vt, zero), jnp.where(is_a, zero, vt)], axis=0)
                add_to(acc_far, hd, jnp.dot(lhs, pb, preferred_element_type=F32))

    for hd in range(heads):
        for x, za_ref in enumerate((zaa_ref, zab_ref)):
            tot = acc[hd][x] + acc_far[hd][x * da:(x + 1) * da]
            y = (tot[:d] * (1.0 / tot[d:d + 1])).T
            za = za_ref[:, hsl(hd)].astype(F32)
            o_ref[0, x, 0, :, hsl(hd)] = (y * (za * _sigmoid(za))).astype(BF16)


def _moba_call(p, bias_own, bias_prev, batch, seq):
    blk, d = MOBA_BLOCK, ATT_HEAD_DIM
    nq = seq // blk
    hg = MOBA_HEAD_GROUP
    w = hg * d
    gpt = PROJ_TN // w
    first = lambda col: (lambda b, h, i: (b * nq + i, col * gpt + h))
    last = lambda col: (lambda b, h, i: (b * nq + nq - 1 - i, col * gpt + h))
    return pl.pallas_call(
        _moba_kernel,
        grid=(batch, ATT_HEADS // hg, nq // 2),
        in_specs=[pl.BlockSpec((blk, w), first(COL_QA)),
                  pl.BlockSpec((blk, w), last(COL_QA)),
                  pl.BlockSpec((seq, w), lambda b, h, i: (b, COL_KA * gpt + h)),
                  pl.BlockSpec((seq, w), lambda b, h, i: (b, COL_VA * gpt + h)),
                  pl.BlockSpec((blk, w), first(COL_ZA)),
                  pl.BlockSpec((blk, w), last(COL_ZA)),
                  pl.BlockSpec((hg, blk, blk), lambda b, h, i: (h, 0, 0)),
                  pl.BlockSpec((hg, blk, blk), lambda b, h, i: (h, 0, 0))],
        out_specs=pl.BlockSpec((1, 2, 1, blk, w), lambda b, h, i: (b, 0, i, 0, h)),
        out_shape=jax.ShapeDtypeStruct((batch, 2, nq // 2, blk, ATT_HEADS * d), BF16),
        scratch_shapes=[pltpu.VMEM((nq, w), F32),
                        pltpu.VMEM((3 * nq, w), BF16),
                        pltpu.VMEM((nq, hg * (d + BF16_SUBLANES), blk), BF16),
                        pltpu.VMEM((2, blk, w), BF16),
                        pltpu.VMEM((2, hg, nq, blk), F32),
                        pltpu.VMEM((hg, nq + 2, blk, blk), F32),
                        pltpu.VMEM((2, hg, SUBLANES, blk), F32)],
        compiler_params=pltpu.CompilerParams(
            dimension_semantics=("arbitrary", "arbitrary", "arbitrary"),
            vmem_limit_bytes=VMEM_LIMIT),
        name="moba",
    )(p, p, p, p, p, p, bias_own, bias_prev)


def _ua_tile_row(t, nq):
    half = nq // 2
    return jnp.where(t < half, t, half + (nq - 1 - t))


def _mlstm_kernel(qm_ref, km_ref, vm_ref, zm_ref, om_ref, g_ref, gt_ref, cw_ref, cb_ref, mlw_ref,
                  o_ref, c_scr, n_scr, m_scr, tailq_scr, tailk_scr, shift_scr):
    c_idx = pl.program_id(1)
    L = qm_ref.shape[0]
    W = ML_HEADS * ML_HEAD_DIM
    pad = SUBLANES

    r_i = lax.broadcasted_iota(jnp.int32, (L, L), 0)
    c_i = lax.broadcasted_iota(jnp.int32, (L, L), 1)

    @pl.when(c_idx == 0)
    def _():
        c_scr[...] = jnp.zeros_like(c_scr)
        n_scr[...] = jnp.zeros_like(n_scr)
        m_scr[...] = jnp.zeros_like(m_scr)
        tailq_scr[...] = jnp.zeros_like(tailq_scr)
        tailk_scr[...] = jnp.zeros_like(tailk_scr)
        for k in range(CONV_WIDTH):
            shift_scr[k] = (c_i == r_i - k).astype(F32).astype(BF16)

    row8 = lax.broadcasted_iota(jnp.int32, (pad, W), 0)

    def conv_silu(src_ref, tail_scr, col0):
        x = src_ref[...]
        tail = tail_scr[...]
        y = cb_ref[:, col0:col0 + W]
        fix = jnp.zeros((pad, W), F32)
        for w in range(CONV_WIDTH):
            k = CONV_WIDTH - 1 - w
            cw = cw_ref[w:w + 1, col0:col0 + W]
            y = y + cw * jnp.dot(shift_scr[k], x, preferred_element_type=F32)
            if k:
                fix = fix + cw * jnp.where(row8 < k, pltpu.roll(tail, k, axis=0), 0.0)
        tail_scr[...] = x[L - 2 * pad:, :].astype(F32)[pad:, :]
        y = jnp.concatenate([y[:pad] + fix, y[pad:]], axis=0)
        return y * _sigmoid(y)

    qc = conv_silu(qm_ref, tailq_scr, 0)
    kc = conv_silu(km_ref, tailk_scr, W) * (ML_HEAD_DIM ** -0.5)

    causal = c_i <= r_i
    ltri = causal.astype(F32)
    utri = (r_i <= c_i).astype(F32)
    g_col = g_ref[...]
    g_row = gt_ref[...]
    b_col_all = jnp.dot(ltri, _log_sigmoid(g_col), precision=lax.Precision.HIGHEST,
                        preferred_element_type=F32)
    b_row_all = jnp.dot(_log_sigmoid(g_row), utri, precision=lax.Precision.HIGHEST,
                        preferred_element_type=F32)

    for hh in range(ML_HEADS):
        sl = slice(hh * ML_HEAD_DIM, (hh + 1) * ML_HEAD_DIM)
        q = qc[:, sl]
        k = kc[:, sl]
        v = vm_ref[:, sl]
        b_col = b_col_all[:, ML_HEADS + hh:ML_HEADS + hh + 1]
        a_col = g_col[:, hh:hh + 1] - b_col
        a_row = g_row[hh:hh + 1, :] - b_row_all[ML_HEADS + hh:ML_HEADS + hh + 1, :]
        m_prev = m_scr[hh][0:1, 0:1]

        a_mask = jnp.where(causal, a_row, -jnp.inf)
        gcol = jnp.maximum(m_prev, jnp.max(a_mask, axis=1, keepdims=True))
        d_mat = jnp.exp(a_mask - gcol)
        inter = jnp.exp(m_prev - gcol)
        g_last = jnp.max(gcol, axis=0, keepdims=True)

        qb = q.astype(BF16)
        s = lax.dot_general(qb, k.astype(BF16), _NT, preferred_element_type=F32) * d_mat
        c_old = c_scr[hh]
        n_old = n_scr[hh]
        num = (inter * jnp.dot(qb, c_old.astype(BF16), preferred_element_type=F32)
               + jnp.dot(s.astype(BF16), v, preferred_element_type=F32))
        den = inter * jnp.sum(q * n_old, axis=1, keepdims=True) + jnp.sum(s, axis=1, keepdims=True)
        h = num / jnp.maximum(jnp.abs(den), jnp.exp(-(b_col + gcol)))

        decay = jnp.exp(m_prev - g_last)
        kw = k * jnp.exp(a_col - g_last)
        c_scr[hh] = decay * c_old + lax.dot_general(kw.astype(BF16), v, _TN,
                                                    preferred_element_type=F32)
        n_scr[hh] = decay * n_old + jnp.sum(kw, axis=0, keepdims=True)
        m_scr[hh] = jnp.broadcast_to(b_col[L - 1:L, :] + g_last, m_scr.shape[1:])

        hg = _sigmoid(om_ref[:, sl].astype(F32)) * h
        hn = hg * lax.rsqrt(jnp.mean(hg * hg, axis=-1, keepdims=True) + EPS) * mlw_ref[:, sl]
        z = zm_ref[:, sl].astype(F32)
        o_ref[:, sl] = (hn * (z * _sigmoid(z))).astype(BF16)


def _mlstm_call(p, g, gt, conv_w, conv_b, ml_norm_w, batch, seq):
    T = p.shape[0]
    L = ML_CHUNK
    nc = seq // L
    W = ML_HEADS * ML_HEAD_DIM
    row = lambda b, c: b * nc + c
    pspec = lambda col: pl.BlockSpec((L, W), lambda b, c: (row(b, c), col))
    return pl.pallas_call(
        _mlstm_kernel,
        grid=(batch, nc),
        in_specs=[pspec(COL_QM), pspec(COL_KM), pspec(COL_VM), pspec(COL_ZM), pspec(COL_OM),
                  pl.BlockSpec((L, LANES), lambda b, c: (row(b, c), 0)),
                  pl.BlockSpec((GATE_COLS, L), lambda b, c: (0, row(b, c))),
                  pl.BlockSpec((CONV_WIDTH, 2 * W), lambda b, c: (0, 0)),
                  pl.BlockSpec((1, 2 * W), lambda b, c: (0, 0)),
                  pl.BlockSpec((1, W), lambda b, c: (0, 0))],
        out_specs=pl.BlockSpec((L, W), lambda b, c: (row(b, c), 0)),
        out_shape=jax.ShapeDtypeStruct((T, W), BF16),
        scratch_shapes=[pltpu.VMEM((ML_HEADS, ML_HEAD_DIM, ML_HEAD_DIM), F32),
                        pltpu.VMEM((ML_HEADS, 1, ML_HEAD_DIM), F32),
                        pltpu.VMEM((ML_HEADS, SUBLANES, LANES), F32),
                        pltpu.VMEM((SUBLANES, W), F32),
                        pltpu.VMEM((SUBLANES, W), F32),
                        pltpu.VMEM((CONV_WIDTH, L, L), BF16)],
        compiler_params=pltpu.CompilerParams(dimension_semantics=("arbitrary", "arbitrary"),
                                             vmem_limit_bytes=VMEM_LIMIT),
        name="mlstm",
    )(p, p, p, p, p, g, gt, conv_w, conv_b, ml_norm_w)


def _outp_kernel(*refs):
    ua_refs, (um_ref, ga_ref, gm_ref, x_ref, ada_ref, wa_ref, wm_ref, wo_ref, o_ref) = refs[:-9], refs[-9:]
    ua = jnp.concatenate([r[...] for r in ua_refs], axis=0)
    ya = jnp.dot(ua, wa_ref[...], preferred_element_type=F32)
    ym = jnp.dot(um_ref[...], wm_ref[...], preferred_element_type=F32)
    y = _sigmoid(ga_ref[...].astype(F32)) * ya + _sigmoid(gm_ref[...].astype(F32)) * ym
    gate = ada_ref[0][2:3, :]
    o_ref[...] = x_ref[...] + gate * jnp.dot(y.astype(BF16), wo_ref[...], preferred_element_type=F32)


def _outp_call(ua, um, p, x2, ada3, wa, wm, wo, seq):
    T, D = x2.shape
    tm = OUT_TM
    steps_per_seq = seq // tm
    blk = MOBA_BLOCK
    nq = seq // blk
    tiles_per_step = tm // blk

    def ua_spec(e):
        def index(i):
            t = (i % steps_per_seq) * tiles_per_step + e
            return ((i // steps_per_seq) * nq + _ua_tile_row(t, nq), 0)
        return pl.BlockSpec((blk, D), index)

    act = lambda col: pl.BlockSpec((tm, D), lambda i: (i, col))
    wspec = pl.BlockSpec((D, D), lambda i: (0, 0))
    return pl.pallas_call(
        _outp_kernel,
        grid=(T // tm,),
        in_specs=[ua_spec(e) for e in range(tiles_per_step)]
                 + [act(0), act(COL_GA), act(COL_GM), act(0),
                    pl.BlockSpec((1, 3, D), lambda i: (i // steps_per_seq, 0, 0)),
                    wspec, wspec, wspec],
        out_specs=pl.BlockSpec((tm, D), lambda i: (i, 0)),
        out_shape=jax.ShapeDtypeStruct((T, D), F32),
        compiler_params=pltpu.CompilerParams(dimension_semantics=("arbitrary",),
                                             vmem_limit_bytes=VMEM_LIMIT),
        name="outp",
    )(*([ua] * tiles_per_step), um, p, p, x2, ada3, wa, wm, wo)


def _rel_bias_tables(rel_bias):
    max_exact = REL_BUCKETS // 2
    rb = rel_bias.astype(F32)
    rb = (rb - rb[REL_BUCKETS - 1:, :]) * LOG2E
    key = jnp.arange(MOBA_BLOCK)[:, None]
    qry = jnp.arange(MOBA_BLOCK)[None, :]
    d_own = qry - key

    def tile(dist):
        nf = jnp.maximum(dist, 1).astype(F32)
        large = max_exact + (jnp.log(nf / max_exact) / math.log(REL_MAX_DIST / max_exact)
                             * (REL_BUCKETS - max_exact)).astype(jnp.int32)
        bucket = jnp.where(dist < max_exact, dist, jnp.minimum(large, REL_BUCKETS - 1))
        onehot = (bucket[..., None] == jnp.arange(REL_BUCKETS)).astype(F32)
        return jnp.einsum('kqb,bh->hkq', onehot, rb, precision=lax.Precision.HIGHEST)

    own = jnp.where(d_own >= 0, tile(jnp.maximum(d_own, 0)), NEG_INF)
    prev = tile(d_own + MOBA_BLOCK)
    return own, prev


def _layer(x2, c, w_ada, b_ada, norm_w, w_in_all, layer, q_norm_w, k_norm_w, rel_bias, conv_w, conv_b,
           b_igate, b_fgate, ml_norm_w, w_att_proj, w_ml_proj, w_out, batch, seq):
    D = D_MODEL
    g0 = 4 * D + 5 * D
    ada3 = _ada_call(c, w_ada, b_ada).reshape(batch, 3, D)

    wb = w_in_all[layer, :, g0 + GATE_COLS:].astype(BF16)
    wg = jnp.pad(w_in_all[layer, :, g0:g0 + GATE_COLS], ((0, 0), (0, LANES - GATE_COLS))).astype(BF16)
    gb = jnp.pad(jnp.concatenate([b_igate, b_fgate]), (0, LANES - GATE_COLS)).reshape(1, LANES)
    q_scale = ATT_HEAD_DIM ** -0.5 * LOG2E
    hnw = jnp.stack([jnp.tile(q_norm_w.astype(F32), ATT_HEADS) * q_scale,
                     jnp.tile(k_norm_w.astype(F32), ATT_HEADS)])

    p, g, gt = _inproj_call(x2, ada3, norm_w.reshape(1, D), w_in_all, layer, wb, wg, gb, hnw, seq)
    bias_own, bias_prev = _rel_bias_tables(rel_bias)
    ua = _moba_call(p, bias_own, bias_prev, batch, seq).reshape(batch * seq, D)
    um = _mlstm_call(p, g, gt, conv_w, conv_b.reshape(1, -1), ml_norm_w.reshape(1, -1), batch, seq)
    return _outp_call(ua, um, p, x2, ada3, w_att_proj.astype(BF16), w_ml_proj.astype(BF16),
                      w_out.astype(BF16), seq)


def kernel(x, c, w_ada, b_ada, norm_w, w_in, q_norm_w, k_norm_w, rel_bias, conv_w, conv_b, b_igate,
           b_fgate, ml_norm_w, w_att_proj, w_ml_proj, w_out):
    batch, seq, D = x.shape
    assert D == D_MODEL and seq % PROJ_TM == 0 and seq % MOBA_BLOCK == 0 and seq % ML_CHUNK == 0
    x2 = x.reshape(batch * seq, D)
    for l in range(w_in.shape[0]):
        x2 = _layer(x2, c, w_ada[l], b_ada[l], norm_w[l], w_in, l, q_norm_w[l], k_norm_w[l],
                    rel_bias, conv_w[l], conv_b[l], b_igate[l], b_fgate[l], ml_norm_w[l],
                    w_att_proj[l], w_ml_proj[l], w_out[l], batch, seq)
    return x2.reshape(batch, seq, D)
```

```python
import functools
import math

import jax
import jax.numpy as jnp
import numpy as np
from jax import lax
from jax.experimental import pallas as pl
from jax.experimental.pallas import tpu as pltpu

F32 = jnp.float32
BF16 = jnp.bfloat16

D_MODEL = 1024
ATT_HEADS = 8
ATT_HEAD_DIM = D_MODEL // ATT_HEADS
MOBA_BLOCK = 256
MOBA_TOPK = 3
REL_BUCKETS = 32
REL_MAX_DIST = 128
ML_HEADS = 4
ML_HEAD_DIM = D_MODEL // ML_HEADS
CONV_WIDTH = 4
EPS = 1e-6
NEG_INF = -1e30
LOG2E = math.log2(math.e)

LANES = 128
SUBLANES = 8
BF16_SUBLANES = 16
GATE_COLS = 2 * ML_HEADS
ML_CHUNK = 256
PROJ_TM = 1024
PROJ_TN = 1024
OUT_TM = 512
MOBA_HEAD_GROUP = 4
VMEM_LIMIT = 52 * 1024 * 1024

COL_QA, COL_KA, COL_VA, COL_ZA, COL_QM, COL_KM, COL_VM, COL_ZM, COL_OM, COL_GA, COL_GM = range(11)
N_COL_TILES = 11
N_HEAD_TILES = 9

_NT = (((1,), (1,)), ((), ()))
_TN = (((0,), (0,)), ((), ()))


def _sigmoid(x):
    return 1.0 / (1.0 + jnp.exp2(x * (-LOG2E)))


def _log_sigmoid(x):
    return jnp.minimum(x, 0.0) - jnp.log1p(jnp.exp(-jnp.abs(x)))


def _ada_kernel(c_ref, w_ref, b_ref, o_ref):
    o_ref[...] = jnp.dot(c_ref[...], w_ref[...], precision=lax.Precision.HIGHEST,
                         preferred_element_type=F32) + b_ref[...]


def _ada_call(c, w_ada, b_ada):
    B, D = c.shape
    return pl.pallas_call(
        _ada_kernel,
        grid=(3,),
        in_specs=[pl.BlockSpec((B, D), lambda j: (0, 0)),
                  pl.BlockSpec((D, D), lambda j: (0, j)),
                  pl.BlockSpec((1, D), lambda j: (0, j))],
        out_specs=pl.BlockSpec((B, D), lambda j: (0, j)),
        out_shape=jax.ShapeDtypeStruct((B, 3 * D), F32),
        compiler_params=pltpu.CompilerParams(dimension_semantics=("arbitrary",),
                                             vmem_limit_bytes=VMEM_LIMIT),
        name="ada",
    )(c, w_ada, b_ada.reshape(1, 3 * D))


def _inproj_kernel(x_ref, ada_ref, nw_ref, wa_ref, wb_ref, wg_ref, gb_ref, hnw_ref,
                   p_ref, g_ref, gt_ref, h_scr):
    j = pl.program_id(1)
    tn = p_ref.shape[1]

    @pl.when(j == 0)
    def _():
        x = x_ref[...]
        ada = ada_ref[0]
        y = x * lax.rsqrt(jnp.mean(x * x, axis=-1, keepdims=True) + EPS) * nw_ref[...]
        h = (y * (1.0 + ada[1:2, :]) + ada[0:1, :]).astype(BF16)
        h_scr[...] = h
        g = jnp.dot(h, wg_ref[...], preferred_element_type=F32) + gb_ref[...]
        g_ref[...] = g
        gt_ref[...] = g.T[:GATE_COLS, :]

    @pl.when(j <= COL_KA)
    def _():
        h = h_scr[...]
        w = hnw_ref[pl.ds(j, 1), :]
        pair = 2 * ATT_HEAD_DIM
        for c0 in range(0, tn, pair):
            acc = jnp.dot(h, wa_ref[:, c0:c0 + pair], preferred_element_type=F32)
            for c in range(c0, c0 + pair, ATT_HEAD_DIM):
                a = acc[:, c - c0:c - c0 + ATT_HEAD_DIM]
                r = lax.rsqrt(jnp.mean(a * a, axis=-1, keepdims=True) + EPS)
                p_ref[:, c:c + ATT_HEAD_DIM] = (a * r * w[:, c:c + ATT_HEAD_DIM]).astype(BF16)

    @pl.when(jnp.logical_and(j > COL_KA, j < N_HEAD_TILES))
    def _():
        p_ref[...] = jnp.dot(h_scr[...], wa_ref[...], preferred_element_type=F32).astype(BF16)

    @pl.when(j >= N_HEAD_TILES)
    def _():
        p_ref[...] = jnp.dot(h_scr[...], wb_ref[...], preferred_element_type=F32).astype(BF16)


def _inproj_call(x2, ada3, norm_w, wa, wb, wg, gb, hnw, seq):
    T, D = x2.shape
    tm, tn = PROJ_TM, PROJ_TN
    tiles_per_seq = seq // tm
    return pl.pallas_call(
        _inproj_kernel,
        grid=(T // tm, N_COL_TILES),
        in_specs=[pl.BlockSpec((tm, D), lambda i, j: (i, 0)),
                  pl.BlockSpec((1, 3, D), lambda i, j: (i // tiles_per_seq, 0, 0)),
                  pl.BlockSpec((1, D), lambda i, j: (0, 0)),
                  pl.BlockSpec((D, tn), lambda i, j: (0, jnp.minimum(j, N_HEAD_TILES - 1))),
                  pl.BlockSpec((D, tn), lambda i, j: (0, jnp.maximum(j - N_HEAD_TILES, 0))),
                  pl.BlockSpec((D, LANES), lambda i, j: (0, 0)),
                  pl.BlockSpec((1, LANES), lambda i, j: (0, 0)),
                  pl.BlockSpec((2, tn), lambda i, j: (0, 0))],
        out_specs=[pl.BlockSpec((tm, tn), lambda i, j: (i, j)),
                   pl.BlockSpec((tm, LANES), lambda i, j: (i, 0)),
                   pl.BlockSpec((GATE_COLS, tm), lambda i, j: (0, i))],
        out_shape=[jax.ShapeDtypeStruct((T, N_COL_TILES * tn), BF16),
                   jax.ShapeDtypeStruct((T, LANES), F32),
                   jax.ShapeDtypeStruct((GATE_COLS, T), F32)],
        scratch_shapes=[pltpu.VMEM((tm, D), BF16)],
        compiler_params=pltpu.CompilerParams(dimension_semantics=("arbitrary", "arbitrary"),
                                             vmem_limit_bytes=VMEM_LIMIT),
        name="inproj",
    )(x2, ada3, norm_w, wa, wb, wg, gb, hnw)


def _moba_kernel(qa_ref, qb_ref, k_ref, v_ref, zaa_ref, zab_ref, bown_ref, bprev_ref, o_ref,
                 kmean_f32_scr, kmean_scr, vt_scr, q2_scr, selb_scr, s_scr, m8_scr):
    step = pl.program_id(2)
    blk, d = MOBA_BLOCK, ATT_HEAD_DIM
    da = d + BF16_SUBLANES
    nb = k_ref.shape[0] // blk
    heads = qa_ref.shape[1] // d
    tile = (step, nb - 1 - step)
    n_far = (jnp.maximum(step - 1, 0), nb - 2 - step)
    far_slots = nb - 2
    near_slots = 4

    def hsl(hd):
        return slice(hd * d, (hd + 1) * d)

    @pl.when(step == 0)
    def _():
        for n in range(nb):
            kb = k_ref[n * blk:(n + 1) * blk, :].astype(F32)
            kmean_f32_scr[n:n + 1, :] = jnp.mean(kb, axis=0, keepdims=True)
            for hd in range(heads):
                vt_scr[n, hd * da:hd * da + d, :] = v_ref[n * blk:(n + 1) * blk, hsl(hd)].T
                vt_scr[n, hd * da + d:(hd + 1) * da, :] = jnp.ones((da - d, blk), BF16)

        rest = kmean_f32_scr[...]
        for c in range(3):
            term = rest.astype(BF16)
            kmean_scr[c * nb:(c + 1) * nb, :] = term
            rest = rest - term.astype(F32)

    q2_scr[0] = qa_ref[...]
    q2_scr[1] = qb_ref[...]

    def fold(x):
        return x.reshape(blk // SUBLANES, SUBLANES, blk)

    def col_max8(s):
        x = fold(s)
        n = x.shape[0]
        while n > 1:
            n //= 2
            x = jnp.maximum(x[:n], x[n:2 * n])
        return x[0]

    rows = lax.broadcasted_iota(jnp.int32, (nb, blk), 0)
    for x in range(2):
        valid = rows < tile[x]
        for hd in range(heads):
            terms = lax.dot_general(kmean_scr[:, hsl(hd)], q2_scr[x, :, hsl(hd)], _NT,
                                    preferred_element_type=F32)
            gate = terms[:nb] + terms[nb:2 * nb] + terms[2 * nb:]
            g = jnp.where(valid, gate, -jnp.inf)
            selb = jnp.full(gate.shape, NEG_INF, F32)
            for _ in range(MOBA_TOPK):
                mx = jnp.max(g, axis=0, keepdims=True)
                first = jnp.min(jnp.where(g == mx, rows, nb), axis=0, keepdims=True)
                pick = rows == first
                selb = jnp.where(pick, 0.0, selb)
                g = jnp.where(pick, -jnp.inf, g)
            selb_scr[x, hd] = jnp.where(valid, selb, NEG_INF)

    def near_items(hd):
        out = []
        for x in range(2):
            jp = jnp.maximum(tile[x] - 1, 0)
            out.append((2 * x, x, tile[x], bown_ref[hd], None))
            out.append((2 * x + 1, x, jp, bprev_ref[hd] + selb_scr[x, hd, pl.ds(jp, 1), :], None))
        return out

    shared_slots = nb // 2 - 2

    def far_items(hd):
        out = []
        for t in range(far_slots):
            if t < shared_slots:
                is_a = t < n_far[0]
                x = jnp.where(is_a, 0, 1)
                j = jnp.where(is_a, t, t - n_far[0])
                live = j < jnp.where(is_a, n_far[0], n_far[1])
            else:
                is_a, x, j = None, 1, t - n_far[0]
                live = j < n_far[1]
            add = jnp.where(live, selb_scr[x, hd, pl.ds(j, 1), :], NEG_INF)
            out.append((near_slots + t, x, j, add, is_a))
        return out

    items = [near_items(hd) + far_items(hd) for hd in range(heads)]
    m8 = [[None, None] for _ in range(heads)]
    groups = [range(g, min(g + 2, heads)) for g in range(0, heads, 2)]
    order = [(i, grp) for grp in groups for i in range(near_slots + far_slots)]
    for i, grp in order:
        for hd in grp:
            slot, x, j, add, is_a = items[hd][i]
            kj = k_ref[pl.ds(pl.multiple_of(j * blk, blk), blk), hsl(hd)]
            s = lax.dot_general(kj, q2_scr[x, :, hsl(hd)], _NT, preferred_element_type=F32) + add
            s_scr[hd, slot] = s
            mt = col_max8(s)
            for y in range(2):
                if is_a is None:
                    if y != x:
                        continue
                    part = mt
                else:
                    part = jnp.where(is_a, mt, -jnp.inf) if y == 0 else jnp.where(is_a, -jnp.inf, mt)
                m8[hd][y] = part if m8[hd][y] is None else jnp.maximum(m8[hd][y], part)
    for hd in range(heads):
        for x in range(2):
            m8_scr[x, hd] = jnp.broadcast_to(jnp.max(m8[hd][x], axis=0, keepdims=True), (SUBLANES, blk))

    acc = [[None, None] for _ in range(heads)]
    acc_far = [None] * heads

    def add_to(lst, k, v):
        lst[k] = v if lst[k] is None else lst[k] + v

    for i, grp in order:
        for hd in grp:
            slot, x, j, _, is_a = items[hd][i]
            p = jnp.exp2(fold(s_scr[hd, slot]) - m8_scr[x, hd])
            pb = p.reshape(blk, blk).astype(BF16)
            vt = vt_scr[j, hd * da:(hd + 1) * da, :]
            if is_a is None:
                add_to(acc[hd], x, jnp.dot(vt, pb, preferred_element_type=F32))
            else:
                zero = jnp.zeros_like(vt)
                lhs = jnp.concatenate([jnp.where(is_a, vt, zero), jnp.where(is_a, zero, vt)], axis=0)
                add_to(acc_far, hd, jnp.dot(lhs, pb, preferred_element_type=F32))

    for hd in range(heads):
        for x, za_ref in enumerate((zaa_ref, zab_ref)):
            tot = acc[hd][x] + acc_far[hd][x * da:(x + 1) * da]
            y = (tot[:d] * (1.0 / tot[d:d + 1])).T
            za = za_ref[:, hsl(hd)].astype(F32)
            o_ref[0, x, 0, :, hsl(hd)] = (y * (za * _sigmoid(za))).astype(BF16)


def _moba_call(p, bias_own, bias_prev, batch, seq):
    blk, d = MOBA_BLOCK, ATT_HEAD_DIM
    nq = seq // blk
    hg = MOBA_HEAD_GROUP
    w = hg * d
    gpt = PROJ_TN // w
    first = lambda col: (lambda b, h, i: (b * nq + i, col * gpt + h))
    last = lambda col: (lambda b, h, i: (b * nq + nq - 1 - i, col * gpt + h))
    return pl.pallas_call(
        _moba_kernel,
        grid=(batch, ATT_HEADS // hg, nq // 2),
        in_specs=[pl.BlockSpec((blk, w), first(COL_QA)),
                  pl.BlockSpec((blk, w), last(COL_QA)),
                  pl.BlockSpec((seq, w), lambda b, h, i: (b, COL_KA * gpt + h)),
                  pl.BlockSpec((seq, w), lambda b, h, i: (b, COL_VA * gpt + h)),
                  pl.BlockSpec((blk, w), first(COL_ZA)),
                  pl.BlockSpec((blk, w), last(COL_ZA)),
                  pl.BlockSpec((hg, blk, blk), lambda b, h, i: (h, 0, 0)),
                  pl.BlockSpec((hg, blk, blk), lambda b, h, i: (h, 0, 0))],
        out_specs=pl.BlockSpec((1, 2, 1, blk, w), lambda b, h, i: (b, 0, i, 0, h)),
        out_shape=jax.ShapeDtypeStruct((batch, 2, nq // 2, blk, ATT_HEADS * d), BF16),
        scratch_shapes=[pltpu.VMEM((nq, w), F32),
                        pltpu.VMEM((3 * nq, w), BF16),
                        pltpu.VMEM((nq, hg * (d + BF16_SUBLANES), blk), BF16),
                        pltpu.VMEM((2, blk, w), BF16),
                        pltpu.VMEM((2, hg, nq, blk), F32),
                        pltpu.VMEM((hg, nq + 2, blk, blk), F32),
                        pltpu.VMEM((2, hg, SUBLANES, blk), F32)],
        compiler_params=pltpu.CompilerParams(
            dimension_semantics=("arbitrary", "arbitrary", "arbitrary"),
            vmem_limit_bytes=VMEM_LIMIT),
        name="moba",
    )(p, p, p, p, p, p, bias_own, bias_prev)


def _ua_tile_row(t, nq):
    half = nq // 2
    return jnp.where(t < half, t, half + (nq - 1 - t))


def _mlstm_kernel(qm_ref, km_ref, vm_ref, zm_ref, om_ref, g_ref, gt_ref, cw_ref, cb_ref, mlw_ref,
                  o_ref, c_scr, n_scr, m_scr, tailq_scr, tailk_scr, shift_scr):
    c_idx = pl.program_id(1)
    L = qm_ref.shape[0]
    W = ML_HEADS * ML_HEAD_DIM
    pad = SUBLANES

    r_i = lax.broadcasted_iota(jnp.int32, (L, L), 0)
    c_i = lax.broadcasted_iota(jnp.int32, (L, L), 1)

    @pl.when(c_idx == 0)
    def _():
        c_scr[...] = jnp.zeros_like(c_scr)
        n_scr[...] = jnp.zeros_like(n_scr)
        m_scr[...] = jnp.zeros_like(m_scr)
        tailq_scr[...] = jnp.zeros_like(tailq_scr)
        tailk_scr[...] = jnp.zeros_like(tailk_scr)
        for k in range(CONV_WIDTH):
            shift_scr[k] = (c_i == r_i - k).astype(F32).astype(BF16)

    row8 = lax.broadcasted_iota(jnp.int32, (pad, W), 0)

    def conv_silu(src_ref, tail_scr, col0):
        x = src_ref[...]
        tail = tail_scr[...]
        y = cb_ref[:, col0:col0 + W]
        fix = jnp.zeros((pad, W), F32)
        for w in range(CONV_WIDTH):
            k = CONV_WIDTH - 1 - w
            cw = cw_ref[w:w + 1, col0:col0 + W]
            y = y + cw * jnp.dot(shift_scr[k], x, preferred_element_type=F32)
            if k:
                fix = fix + cw * jnp.where(row8 < k, pltpu.roll(tail, k, axis=0), 0.0)
        tail_scr[...] = x[L - 2 * pad:, :].astype(F32)[pad:, :]
        y = jnp.concatenate([y[:pad] + fix, y[pad:]], axis=0)
        return y * _sigmoid(y)

    qc = conv_silu(qm_ref, tailq_scr, 0)
    kc = conv_silu(km_ref, tailk_scr, W) * (ML_HEAD_DIM ** -0.5)

    causal = c_i <= r_i
    ltri = causal.astype(F32)
    utri = (r_i <= c_i).astype(F32)
    g_col = g_ref[...]
    g_row = gt_ref[...]
    b_col_all = jnp.dot(ltri, _log_sigmoid(g_col), precision=lax.Precision.HIGHEST,
                        preferred_element_type=F32)
    b_row_all = jnp.dot(_log_sigmoid(g_row), utri, precision=lax.Precision.HIGHEST,
                        preferred_element_type=F32)

    for hh in range(ML_HEADS):
        sl = slice(hh * ML_HEAD_DIM, (hh + 1) * ML_HEAD_DIM)
        q = qc[:, sl]
        k = kc[:, sl]
        v = vm_ref[:, sl]
        b_col = b_col_all[:, ML_HEADS + hh:ML_HEADS + hh + 1]
        a_col = g_col[:, hh:hh + 1] - b_col
        a_row = g_row[hh:hh + 1, :] - b_row_all[ML_HEADS + hh:ML_HEADS + hh + 1, :]
        m_prev = m_scr[hh][0:1, 0:1]

        a_mask = jnp.where(causal, a_row, -jnp.inf)
        gcol = jnp.maximum(m_prev, jnp.max(a_mask, axis=1, keepdims=True))
        d_mat = jnp.exp(a_mask - gcol)
        inter = jnp.exp(m_prev - gcol)
        g_last = jnp.max(gcol, axis=0, keepdims=True)

        qb = q.astype(BF16)
        s = lax.dot_general(qb, k.astype(BF16), _NT, preferred_element_type=F32) * d_mat
        c_old = c_scr[hh]
        n_old = n_scr[hh]
        num = (inter * jnp.dot(qb, c_old.astype(BF16), preferred_element_type=F32)
               + jnp.dot(s.astype(BF16), v, preferred_element_type=F32))
        den = inter * jnp.sum(q * n_old, axis=1, keepdims=True) + jnp.sum(s, axis=1, keepdims=True)
        h = num / jnp.maximum(jnp.abs(den), jnp.exp(-(b_col + gcol)))

        decay = jnp.exp(m_prev - g_last)
        kw = k * jnp.exp(a_col - g_last)
        c_scr[hh] = decay * c_old + lax.dot_general(kw.astype(BF16), v, _TN,
                                                    preferred_element_type=F32)
        n_scr[hh] = decay * n_old + jnp.sum(kw, axis=0, keepdims=True)
        m_scr[hh] = jnp.broadcast_to(b_col[L - 1:L, :] + g_last, m_scr.shape[1:])

        hg = _sigmoid(om_ref[:, sl].astype(F32)) * h
        hn = hg * lax.rsqrt(jnp.mean(hg * hg, axis=-1, keepdims=True) + EPS) * mlw_ref[:, sl]
        z = zm_ref[:, sl].astype(F32)
        o_ref[:, sl] = (hn * (z * _sigmoid(z))).astype(BF16)


def _mlstm_call(p, g, gt, conv_w, conv_b, ml_norm_w, batch, seq):
    T = p.shape[0]
    L = ML_CHUNK
    nc = seq // L
    W = ML_HEADS * ML_HEAD_DIM
    row = lambda b, c: b * nc + c
    pspec = lambda col: pl.BlockSpec((L, W), lambda b, c: (row(b, c), col))
    return pl.pallas_call(
        _mlstm_kernel,
        grid=(batch, nc),
        in_specs=[pspec(COL_QM), pspec(COL_KM), pspec(COL_VM), pspec(COL_ZM), pspec(COL_OM),
                  pl.BlockSpec((L, LANES), lambda b, c: (row(b, c), 0)),
                  pl.BlockSpec((GATE_COLS, L), lambda b, c: (0, row(b, c))),
                  pl.BlockSpec((CONV_WIDTH, 2 * W), lambda b, c: (0, 0)),
                  pl.BlockSpec((1, 2 * W), lambda b, c: (0, 0)),
                  pl.BlockSpec((1, W), lambda b, c: (0, 0))],
        out_specs=pl.BlockSpec((L, W), lambda b, c: (row(b, c), 0)),
        out_shape=jax.ShapeDtypeStruct((T, W), BF16),
        scratch_shapes=[pltpu.VMEM((ML_HEADS, ML_HEAD_DIM, ML_HEAD_DIM), F32),
                        pltpu.VMEM((ML_HEADS, 1, ML_HEAD_DIM), F32),
                        pltpu.VMEM((ML_HEADS, SUBLANES, LANES), F32),
                        pltpu.VMEM((SUBLANES, W), F32),
                        pltpu.VMEM((SUBLANES, W), F32),
                        pltpu.VMEM((CONV_WIDTH, L, L), BF16)],
        compiler_params=pltpu.CompilerParams(dimension_semantics=("arbitrary", "arbitrary"),
                                             vmem_limit_bytes=VMEM_LIMIT),
        name="mlstm",
    )(p, p, p, p, p, g, gt, conv_w, conv_b, ml_norm_w)


def _outp_kernel(*refs):
    ua_refs, (um_ref, ga_ref, gm_ref, x_ref, ada_ref, wa_ref, wm_ref, wo_ref, o_ref) = refs[:-9], refs[-9:]
    ua = jnp.concatenate([r[...] for r in ua_refs], axis=0)
    ya = jnp.dot(ua, wa_ref[...], preferred_element_type=F32)
    ym = jnp.dot(um_ref[...], wm_ref[...], preferred_element_type=F32)
    y = _sigmoid(ga_ref[...].astype(F32)) * ya + _sigmoid(gm_ref[...].astype(F32)) * ym
    gate = ada_ref[0][2:3, :]
    o_ref[...] = x_ref[...] + gate * jnp.dot(y.astype(BF16), wo_ref[...], preferred_element_type=F32)


def _outp_call(ua, um, p, x2, ada3, wa, wm, wo, seq):
    T, D = x2.shape
    tm = OUT_TM
    steps_per_seq = seq // tm
    blk = MOBA_BLOCK
    nq = seq // blk
    tiles_per_step = tm // blk

    def ua_spec(e):
        def index(i):
            t = (i % steps_per_seq) * tiles_per_step + e
            return ((i // steps_per_seq) * nq + _ua_tile_row(t, nq), 0)
        return pl.BlockSpec((blk, D), index)

    act = lambda col: pl.BlockSpec((tm, D), lambda i: (i, col))
    wspec = pl.BlockSpec((D, D), lambda i: (0, 0))
    return pl.pallas_call(
        _outp_kernel,
        grid=(T // tm,),
        in_specs=[ua_spec(e) for e in range(tiles_per_step)]
                 + [act(0), act(COL_GA), act(COL_GM), act(0),
                    pl.BlockSpec((1, 3, D), lambda i: (i // steps_per_seq, 0, 0)),
                    wspec, wspec, wspec],
        out_specs=pl.BlockSpec((tm, D), lambda i: (i, 0)),
        out_shape=jax.ShapeDtypeStruct((T, D), F32),
        compiler_params=pltpu.CompilerParams(dimension_semantics=("arbitrary",),
                                             vmem_limit_bytes=VMEM_LIMIT),
        name="outp",
    )(*([ua] * tiles_per_step), um, p, p, x2, ada3, wa, wm, wo)


def _rel_bias_tables(rel_bias):
    max_exact = REL_BUCKETS // 2
    rb = rel_bias.astype(F32)
    rb = (rb - rb[REL_BUCKETS - 1:, :]) * LOG2E
    key = jnp.arange(MOBA_BLOCK)[:, None]
    qry = jnp.arange(MOBA_BLOCK)[None, :]
    d_own = qry - key

    def tile(dist):
        nf = jnp.maximum(dist, 1).astype(F32)
        large = max_exact + (jnp.log(nf / max_exact) / math.log(REL_MAX_DIST / max_exact)
                             * (REL_BUCKETS - max_exact)).astype(jnp.int32)
        bucket = jnp.where(dist < max_exact, dist, jnp.minimum(large, REL_BUCKETS - 1))
        onehot = (bucket[..., None] == jnp.arange(REL_BUCKETS)).astype(F32)
        return jnp.einsum('kqb,bh->hkq', onehot, rb, precision=lax.Precision.HIGHEST)

    own = jnp.where(d_own >= 0, tile(jnp.maximum(d_own, 0)), NEG_INF)
    prev = tile(d_own + MOBA_BLOCK)
    return own, prev


def _layer(x2, c, w_ada, b_ada, norm_w, w_in_all, layer, q_norm_w, k_norm_w, rel_bias, conv_w, conv_b,
           b_igate, b_fgate, ml_norm_w, w_att_proj, w_ml_proj, w_out, batch, seq):
    D = D_MODEL
    g0 = 4 * D + 5 * D
    ada3 = _ada_call(c, w_ada, b_ada).reshape(batch, 3, D)

    wa = w_in_all[layer, :, :g0].astype(BF16)
    wb = w_in_all[layer, :, g0 + GATE_COLS:].astype(BF16)
    wg = jnp.pad(w_in_all[layer, :, g0:g0 + GATE_COLS], ((0, 0), (0, LANES - GATE_COLS))).astype(BF16)
    gb = jnp.pad(jnp.concatenate([b_igate, b_fgate]), (0, LANES - GATE_COLS)).reshape(1, LANES)
    q_scale = ATT_HEAD_DIM ** -0.5 * LOG2E
    hnw = jnp.stack([jnp.tile(q_norm_w.astype(F32), ATT_HEADS) * q_scale,
                     jnp.tile(k_norm_w.astype(F32), ATT_HEADS)])

    p, g, gt = _inproj_call(x2, ada3, norm_w.reshape(1, D), wa, wb, wg, gb, hnw, seq)
    bias_own, bias_prev = _rel_bias_tables(rel_bias)
    ua = _moba_call(p, bias_own, bias_prev, batch, seq).reshape(batch * seq, D)
    um = _mlstm_call(p, g, gt, conv_w, conv_b.reshape(1, -1), ml_norm_w.reshape(1, -1), batch, seq)
    return _outp_call(ua, um, p, x2, ada3, w_att_proj.astype(BF16), w_ml_proj.astype(BF16),
                      w_out.astype(BF16), seq)


def kernel(x, c, w_ada, b_ada, norm_w, w_in, q_norm_w, k_norm_w, rel_bias, conv_w, conv_b, b_igate,
           b_fgate, ml_norm_w, w_att_proj, w_ml_proj, w_out):
    batch, seq, D = x.shape
    assert D == D_MODEL and seq % PROJ_TM == 0 and seq % MOBA_BLOCK == 0 and seq % ML_CHUNK == 0
    x2 = x.reshape(batch * seq, D)
    for l in range(w_in.shape[0]):
        x2 = _layer(x2, c, w_ada[l], b_ada[l], norm_w[l], w_in, l, q_norm_w[l], k_norm_w[l],
                    rel_bias, conv_w[l], conv_b[l], b_igate[l], b_fgate[l], ml_norm_w[l],
                    w_att_proj[l], w_ml_proj[l], w_out[l], batch, seq)
    return x2.reshape(batch, seq, D)
```

```python
import functools
import math

import jax
import jax.numpy as jnp
import numpy as np
from jax import lax
from jax.experimental import pallas as pl
from jax.experimental.pallas import tpu as pltpu

F32 = jnp.float32
BF16 = jnp.bfloat16

D_MODEL = 1024
ATT_HEADS = 8
ATT_HEAD_DIM = D_MODEL // ATT_HEADS
MOBA_BLOCK = 256
MOBA_TOPK = 3
REL_BUCKETS = 32
REL_MAX_DIST = 128
ML_HEADS = 4
ML_HEAD_DIM = D_MODEL // ML_HEADS
CONV_WIDTH = 4
EPS = 1e-6
NEG_INF = -1e30
LOG2E = math.log2(math.e)

LANES = 128
SUBLANES = 8
BF16_SUBLANES = 16
GATE_COLS = 2 * ML_HEADS
ML_CHUNK = 256
PROJ_TM = 1024
PROJ_TN = 1024
OUT_TM = 512
MOBA_HEAD_GROUP = 4
VMEM_LIMIT = 52 * 1024 * 1024

COL_QA, COL_KA, COL_VA, COL_ZA, COL_QM, COL_KM, COL_VM, COL_ZM, COL_OM, COL_GA, COL_GM = range(11)
N_COL_TILES = 11
N_HEAD_TILES = 9

_NT = (((1,), (1,)), ((), ()))
_TN = (((0,), (0,)), ((), ()))


def _sigmoid(x):
    return 1.0 / (1.0 + jnp.exp2(x * (-LOG2E)))


def _log_sigmoid(x):
    return jnp.minimum(x, 0.0) - jnp.log1p(jnp.exp(-jnp.abs(x)))


def _ada_kernel(c_ref, w_ref, b_ref, o_ref):
    o_ref[...] = jnp.dot(c_ref[...], w_ref[...], precision=lax.Precision.HIGHEST,
                         preferred_element_type=F32) + b_ref[...]


def _ada_call(c, w_ada, b_ada):
    B, D = c.shape
    return pl.pallas_call(
        _ada_kernel,
        grid=(3,),
        in_specs=[pl.BlockSpec((B, D), lambda j: (0, 0)),
                  pl.BlockSpec((D, D), lambda j: (0, j)),
                  pl.BlockSpec((1, D), lambda j: (0, j))],
        out_specs=pl.BlockSpec((B, D), lambda j: (0, j)),
        out_shape=jax.ShapeDtypeStruct((B, 3 * D), F32),
        compiler_params=pltpu.CompilerParams(dimension_semantics=("arbitrary",),
                                             vmem_limit_bytes=VMEM_LIMIT),
        name="ada",
    )(c, w_ada, b_ada.reshape(1, 3 * D))


def _inproj_kernel(x_ref, ada_ref, nw_ref, wa_ref, wb_ref, wg_ref, gb_ref, hnw_ref,
                   p_ref, g_ref, gt_ref, h_scr):
    j = pl.program_id(1)
    tn = p_ref.shape[1]

    @pl.when(j == 0)
    def _():
        x = x_ref[...]
        ada = ada_ref[0]
        y = x * lax.rsqrt(jnp.mean(x * x, axis=-1, keepdims=True) + EPS) * nw_ref[...]
        h = (y * (1.0 + ada[1:2, :]) + ada[0:1, :]).astype(BF16)
        h_scr[...] = h
        g = jnp.dot(h, wg_ref[...], preferred_element_type=F32) + gb_ref[...]
        g_ref[...] = g
        gt_ref[...] = g.T[:GATE_COLS, :]

    @pl.when(j <= COL_KA)
    def _():
        h = h_scr[...]
        w = hnw_ref[pl.ds(j, 1), :]
        pair = 2 * ATT_HEAD_DIM
        for c0 in range(0, tn, pair):
            acc = jnp.dot(h, wa_ref[:, c0:c0 + pair], preferred_element_type=F32)
            for c in range(c0, c0 + pair, ATT_HEAD_DIM):
                a = acc[:, c - c0:c - c0 + ATT_HEAD_DIM]
                r = lax.rsqrt(jnp.mean(a * a, axis=-1, keepdims=True) + EPS)
                p_ref[:, c:c + ATT_HEAD_DIM] = (a * r * w[:, c:c + ATT_HEAD_DIM]).astype(BF16)

    @pl.when(jnp.logical_and(j > COL_KA, j < N_HEAD_TILES))
    def _():
        p_ref[...] = jnp.dot(h_scr[...], wa_ref[...], preferred_element_type=F32).astype(BF16)

    @pl.when(j >= N_HEAD_TILES)
    def _():
        p_ref[...] = jnp.dot(h_scr[...], wb_ref[...], preferred_element_type=F32).astype(BF16)


def _inproj_call(x2, ada3, norm_w, wa, wb, wg, gb, hnw, seq):
    T, D = x2.shape
    tm, tn = PROJ_TM, PROJ_TN
    tiles_per_seq = seq // tm
    return pl.pallas_call(
        _inproj_kernel,
        grid=(T // tm, N_COL_TILES),
        in_specs=[pl.BlockSpec((tm, D), lambda i, j: (i, 0)),
                  pl.BlockSpec((1, 3, D), lambda i, j: (i // tiles_per_seq, 0, 0)),
                  pl.BlockSpec((1, D), lambda i, j: (0, 0)),
                  pl.BlockSpec((D, tn), lambda i, j: (0, jnp.minimum(j, N_HEAD_TILES - 1))),
                  pl.BlockSpec((D, tn), lambda i, j: (0, jnp.maximum(j - N_HEAD_TILES, 0))),
                  pl.BlockSpec((D, LANES), lambda i, j: (0, 0)),
                  pl.BlockSpec((1, LANES), lambda i, j: (0, 0)),
                  pl.BlockSpec((2, tn), lambda i, j: (0, 0))],
        out_specs=[pl.BlockSpec((tm, tn), lambda i, j: (i, j)),
                   pl.BlockSpec((tm, LANES), lambda i, j: (i, 0)),
                   pl.BlockSpec((GATE_COLS, tm), lambda i, j: (0, i))],
        out_shape=[jax.ShapeDtypeStruct((T, N_COL_TILES * tn), BF16),
                   jax.ShapeDtypeStruct((T, LANES), F32),
                   jax.ShapeDtypeStruct((GATE_COLS, T), F32)],
        scratch_shapes=[pltpu.VMEM((tm, D), BF16)],
        compiler_params=pltpu.CompilerParams(dimension_semantics=("arbitrary", "arbitrary"),
                                             vmem_limit_bytes=VMEM_LIMIT),
        name="inproj",
    )(x2, ada3, norm_w, wa, wb, wg, gb, hnw)


def _moba_kernel(qa_ref, qb_ref, k_ref, v_ref, zaa_ref, zab_ref, bown_ref, bprev_ref, o_ref,
                 kmean_f32_scr, kmean_scr, vt_scr, q2_scr, selb_scr, s_scr, m8_scr):
    step = pl.program_id(2)
    blk, d = MOBA_BLOCK, ATT_HEAD_DIM
    da = d + BF16_SUBLANES
    nb = k_ref.shape[0] // blk
    heads = qa_ref.shape[1] // d
    tile = (step, nb - 1 - step)
    n_far = (jnp.maximum(step - 1, 0), nb - 2 - step)
    near_slots = 4
    far_slots = nb - near_slots + 1

    def hsl(hd):
        return slice(hd * d, (hd + 1) * d)

    @pl.when(step == 0)
    def _():
        for n in range(nb):
            kb = k_ref[n * blk:(n + 1) * blk, :].astype(F32)
            kmean_f32_scr[n:n + 1, :] = jnp.mean(kb, axis=0, keepdims=True)
            for hd in range(heads):
                vt_scr[n, hd * da:hd * da + d, :] = v_ref[n * blk:(n + 1) * blk, hsl(hd)].T
                vt_scr[n, hd * da + d:(hd + 1) * da, :] = jnp.ones((da - d, blk), BF16)

        rest = kmean_f32_scr[...]
        for c in range(3):
            term = rest.astype(BF16)
            kmean_scr[c * nb:(c + 1) * nb, :] = term
            rest = rest - term.astype(F32)

    q2_scr[0] = qa_ref[...]
    q2_scr[1] = qb_ref[...]

    def fold(x):
        return x.reshape(blk // SUBLANES, SUBLANES, blk)

    def col_max8(s):
        x = fold(s)
        n = x.shape[0]
        while n > 1:
            n //= 2
            x = jnp.maximum(x[:n], x[n:2 * n])
        return x[0]

    rows = lax.broadcasted_iota(jnp.int32, (nb, blk), 0)
    for x in range(2):
        valid = rows < tile[x]
        for hd in range(heads):
            terms = lax.dot_general(kmean_scr[:, hsl(hd)], q2_scr[x, :, hsl(hd)], _NT,
                                    preferred_element_type=F32)
            gate = terms[:nb] + terms[nb:2 * nb] + terms[2 * nb:]
            g = jnp.where(valid, gate, -jnp.inf)
            selb = jnp.full(gate.shape, NEG_INF, F32)
            for _ in range(MOBA_TOPK):
                mx = jnp.max(g, axis=0, keepdims=True)
                first = jnp.min(jnp.where(g == mx, rows, nb), axis=0, keepdims=True)
                pick = rows == first
                selb = jnp.where(pick, 0.0, selb)
                g = jnp.where(pick, -jnp.inf, g)
            selb_scr[x, hd] = jnp.where(valid, selb, NEG_INF)

    def near_items(hd):
        first = step == 0
        jb = tile[1] - 1
        ja = jnp.where(first, far_slots, step - 1)
        add_a = jnp.where(first, selb_scr[1, hd, pl.ds(far_slots, 1), :],
                          bprev_ref[hd] + selb_scr[0, hd, pl.ds(jnp.maximum(step - 1, 0), 1), :])
        return [(0, 0, tile[0], bown_ref[hd], None),
                (1, 1, tile[1], bown_ref[hd], None),
                (2, 1, jb, bprev_ref[hd] + selb_scr[1, hd, pl.ds(jb, 1), :], None),
                (3, jnp.where(first, 1, 0), ja, add_a, jnp.logical_not(first))]

    shared_slots = nb // 2 - 2

    def far_items(hd):
        out = []
        for t in range(far_slots):
            if t < shared_slots:
                is_a = t < n_far[0]
                x = jnp.where(is_a, 0, 1)
                j = jnp.where(is_a, t, t - n_far[0])
            else:
                is_a, x, j = None, 1, t - n_far[0]
            out.append((near_slots + t, x, j, selb_scr[x, hd, pl.ds(j, 1), :], is_a))
        return out

    items = [near_items(hd) + far_items(hd) for hd in range(heads)]
    m8 = [[None, None] for _ in range(heads)]
    groups = [range(g, min(g + 2, heads)) for g in range(0, heads, 2)]
    order = [(i, grp) for grp in groups for i in range(near_slots + far_slots)]
    for i, grp in order:
        for hd in grp:
            slot, x, j, add, is_a = items[hd][i]
            kj = k_ref[pl.ds(pl.multiple_of(j * blk, blk), blk), hsl(hd)]
            s = lax.dot_general(kj, q2_scr[x, :, hsl(hd)], _NT, preferred_element_type=F32) + add
            s_scr[hd, slot] = s
            mt = col_max8(s)
            for y in range(2):
                if is_a is None:
                    if y != x:
                        continue
                    part = mt
                else:
                    part = jnp.where(is_a, mt, -jnp.inf) if y == 0 else jnp.where(is_a, -jnp.inf, mt)
                m8[hd][y] = part if m8[hd][y] is None else jnp.maximum(m8[hd][y], part)
    for hd in range(heads):
        for x in range(2):
            m8_scr[x, hd] = jnp.broadcast_to(jnp.max(m8[hd][x], axis=0, keepdims=True), (SUBLANES, blk))

    acc = [[None, None] for _ in range(heads)]
    acc_far = [None] * heads

    def add_to(lst, k, v):
        lst[k] = v if lst[k] is None else lst[k] + v

    for i, grp in order:
        for hd in grp:
            slot, x, j, _, is_a = items[hd][i]
            p = jnp.exp2(fold(s_scr[hd, slot]) - m8_scr[x, hd])
            pb = p.reshape(blk, blk).astype(BF16)
            vt = vt_scr[j, hd * da:(hd + 1) * da, :]
            if is_a is None:
                add_to(acc[hd], x, jnp.dot(vt, pb, preferred_element_type=F32))
            else:
                zero = jnp.zeros_like(vt)
                lhs = jnp.concatenate([jnp.where(is_a, vt, zero), jnp.where(is_a, zero, vt)], axis=0)
                add_to(acc_far, hd, jnp.dot(lhs, pb, preferred_element_type=F32))

    for hd in range(heads):
        for x, za_ref in enumerate((zaa_ref, zab_ref)):
            tot = acc[hd][x] + acc_far[hd][x * da:(x + 1) * da]
            y = (tot[:d] * (1.0 / tot[d:d + 1])).T
            za = za_ref[:, hsl(hd)].astype(F32)
            o_ref[0, x, 0, :, hsl(hd)] = (y * (za * _sigmoid(za))).astype(BF16)


def _moba_call(p, bias_own, bias_prev, batch, seq):
    blk, d = MOBA_BLOCK, ATT_HEAD_DIM
    nq = seq // blk
    hg = MOBA_HEAD_GROUP
    w = hg * d
    gpt = PROJ_TN // w
    first = lambda col: (lambda b, h, i: (b * nq + i, col * gpt + h))
    last = lambda col: (lambda b, h, i: (b * nq + nq - 1 - i, col * gpt + h))
    return pl.pallas_call(
        _moba_kernel,
        grid=(batch, ATT_HEADS // hg, nq // 2),
        in_specs=[pl.BlockSpec((blk, w), first(COL_QA)),
                  pl.BlockSpec((blk, w), last(COL_QA)),
                  pl.BlockSpec((seq, w), lambda b, h, i: (b, COL_KA * gpt + h)),
                  pl.BlockSpec((seq, w), lambda b, h, i: (b, COL_VA * gpt + h)),
                  pl.BlockSpec((blk, w), first(COL_ZA)),
                  pl.BlockSpec((blk, w), last(COL_ZA)),
                  pl.BlockSpec((hg, blk, blk), lambda b, h, i: (h, 0, 0)),
                  pl.BlockSpec((hg, blk, blk), lambda b, h, i: (h, 0, 0))],
        out_specs=pl.BlockSpec((1, 2, 1, blk, w), lambda b, h, i: (b, 0, i, 0, h)),
        out_shape=jax.ShapeDtypeStruct((batch, 2, nq // 2, blk, ATT_HEADS * d), BF16),
        scratch_shapes=[pltpu.VMEM((nq, w), F32),
                        pltpu.VMEM((3 * nq, w), BF16),
                        pltpu.VMEM((nq, hg * (d + BF16_SUBLANES), blk), BF16),
                        pltpu.VMEM((2, blk, w), BF16),
                        pltpu.VMEM((2, hg, nq, blk), F32),
                        pltpu.VMEM((hg, nq + 1, blk, blk), F32),
                        pltpu.VMEM((2, hg, SUBLANES, blk), F32)],
        compiler_params=pltpu.CompilerParams(
            dimension_semantics=("arbitrary", "arbitrary", "arbitrary"),
            vmem_limit_bytes=VMEM_LIMIT),
        name="moba",
    )(p, p, p, p, p, p, bias_own, bias_prev)


def _ua_tile_row(t, nq):
    half = nq // 2
    return jnp.where(t < half, t, half + (nq - 1 - t))


def _mlstm_kernel(qm_ref, km_ref, vm_ref, zm_ref, om_ref, g_ref, gt_ref, cw_ref, cb_ref, mlw_ref,
                  o_ref, c_scr, n_scr, m_scr, tailq_scr, tailk_scr, shift_scr):
    c_idx = pl.program_id(1)
    L = qm_ref.shape[0]
    W = ML_HEADS * ML_HEAD_DIM
    pad = SUBLANES

    r_i = lax.broadcasted_iota(jnp.int32, (L, L), 0)
    c_i = lax.broadcasted_iota(jnp.int32, (L, L), 1)

    @pl.when(c_idx == 0)
    def _():
        c_scr[...] = jnp.zeros_like(c_scr)
        n_scr[...] = jnp.zeros_like(n_scr)
        m_scr[...] = jnp.zeros_like(m_scr)
        tailq_scr[...] = jnp.zeros_like(tailq_scr)
        tailk_scr[...] = jnp.zeros_like(tailk_scr)
        for k in range(CONV_WIDTH):
            shift_scr[k] = (c_i == r_i - k).astype(F32).astype(BF16)

    dh = ML_HEAD_DIM
    heads = range(ML_HEADS)
    cols = [slice(hh * dh, (hh + 1) * dh) for hh in heads]
    row8 = lax.broadcasted_iota(jnp.int32, (pad, dh), 0)

    def conv_silu(src_ref, tail_scr, col0, sl):
        wsl = slice(col0 + sl.start, col0 + sl.stop)
        x = src_ref[:, sl]
        tail = tail_scr[:, sl]
        y = cb_ref[:, wsl]
        fix = jnp.zeros((pad, dh), F32)
        for w in range(CONV_WIDTH):
            k = CONV_WIDTH - 1 - w
            cw = cw_ref[w:w + 1, wsl]
            y = y + cw * jnp.dot(shift_scr[k], x, preferred_element_type=F32)
            if k:
                fix = fix + cw * jnp.where(row8 < k, pltpu.roll(tail, k, axis=0), 0.0)
        tail_scr[:, sl] = x[L - 2 * pad:, :].astype(F32)[pad:, :]
        y = jnp.concatenate([y[:pad] + fix, y[pad:]], axis=0)
        return y * _sigmoid(y)

    causal = c_i <= r_i
    ltri = causal.astype(F32)
    utri = (r_i <= c_i).astype(F32)
    g_col = g_ref[...]
    g_row = gt_ref[...]
    b_col_all = jnp.dot(ltri, _log_sigmoid(g_col), precision=lax.Precision.HIGHEST,
                        preferred_element_type=F32)
    b_row_all = jnp.dot(_log_sigmoid(g_row), utri, precision=lax.Precision.HIGHEST,
                        preferred_element_type=F32)

    for hh in heads:
        sl = cols[hh]
        q = conv_silu(qm_ref, tailq_scr, 0, sl)
        k = conv_silu(km_ref, tailk_scr, W, sl) * (dh ** -0.5)
        v = vm_ref[:, sl]
        b_col = b_col_all[:, ML_HEADS + hh:ML_HEADS + hh + 1]
        a_col = g_col[:, hh:hh + 1] - b_col
        a_row = g_row[hh:hh + 1, :] - b_row_all[ML_HEADS + hh:ML_HEADS + hh + 1, :]
        m_prev = m_scr[hh][0:1, 0:1]

        a_mask = jnp.where(causal, a_row, -jnp.inf)
        gcol = jnp.maximum(m_prev, jnp.max(a_mask, axis=1, keepdims=True))
        d_mat = jnp.exp(a_mask - gcol)
        inter = jnp.exp(m_prev - gcol)
        g_last = jnp.max(gcol, axis=0, keepdims=True)

        qb = q.astype(BF16)
        s = lax.dot_general(qb, k.astype(BF16), _NT, preferred_element_type=F32) * d_mat
        c_old = c_scr[hh]
        n_old = n_scr[hh]
        num = (inter * jnp.dot(qb, c_old.astype(BF16), preferred_element_type=F32)
               + jnp.dot(s.astype(BF16), v, preferred_element_type=F32))
        den = inter * jnp.sum(q * n_old, axis=1, keepdims=True) + jnp.sum(s, axis=1, keepdims=True)
        h = num / jnp.maximum(jnp.abs(den), jnp.exp(-(b_col + gcol)))

        decay = jnp.exp(m_prev - g_last)
        kw = k * jnp.exp(a_col - g_last)
        c_scr[hh] = decay * c_old + lax.dot_general(kw.astype(BF16), v, _TN,
                                                    preferred_element_type=F32)
        n_scr[hh] = decay * n_old + jnp.sum(kw, axis=0, keepdims=True)
        m_scr[hh] = jnp.broadcast_to(b_col[L - 1:L, :] + g_last, m_scr.shape[1:])

        hg = _sigmoid(om_ref[:, sl].astype(F32)) * h
        hn = hg * lax.rsqrt(jnp.mean(hg * hg, axis=-1, keepdims=True) + EPS) * mlw_ref[:, sl]
        z = zm_ref[:, sl].astype(F32)
        o_ref[:, sl] = (hn * (z * _sigmoid(z))).astype(BF16)


def _mlstm_call(p, g, gt, conv_w, conv_b, ml_norm_w, batch, seq):
    T = p.shape[0]
    L = ML_CHUNK
    nc = seq // L
    W = ML_HEADS * ML_HEAD_DIM
    row = lambda b, c: b * nc + c
    pspec = lambda col: pl.BlockSpec((L, W), lambda b, c: (row(b, c), col))
    return pl.pallas_call(
        _mlstm_kernel,
        grid=(batch, nc),
        in_specs=[pspec(COL_QM), pspec(COL_KM), pspec(COL_VM), pspec(COL_ZM), pspec(COL_OM),
                  pl.BlockSpec((L, LANES), lambda b, c: (row(b, c), 0)),
                  pl.BlockSpec((GATE_COLS, L), lambda b, c: (0, row(b, c))),
                  pl.BlockSpec((CONV_WIDTH, 2 * W), lambda b, c: (0, 0)),
                  pl.BlockSpec((1, 2 * W), lambda b, c: (0, 0)),
                  pl.BlockSpec((1, W), lambda b, c: (0, 0))],
        out_specs=pl.BlockSpec((L, W), lambda b, c: (row(b, c), 0)),
        out_shape=jax.ShapeDtypeStruct((T, W), BF16),
        scratch_shapes=[pltpu.VMEM((ML_HEADS, ML_HEAD_DIM, ML_HEAD_DIM), F32),
                        pltpu.VMEM((ML_HEADS, 1, ML_HEAD_DIM), F32),
                        pltpu.VMEM((ML_HEADS, SUBLANES, LANES), F32),
                        pltpu.VMEM((SUBLANES, W), F32),
                        pltpu.VMEM((SUBLANES, W), F32),
                        pltpu.VMEM((CONV_WIDTH, L, L), BF16)],
        compiler_params=pltpu.CompilerParams(dimension_semantics=("arbitrary", "arbitrary"),
                                             vmem_limit_bytes=VMEM_LIMIT),
        name="mlstm",
    )(p, p, p, p, p, g, gt, conv_w, conv_b, ml_norm_w)


def _outp_kernel(*refs):
    ua_refs, (um_ref, ga_ref, gm_ref, x_ref, ada_ref, wa_ref, wm_ref, wo_ref, o_ref) = refs[:-9], refs[-9:]
    ua = jnp.concatenate([r[...] for r in ua_refs], axis=0)
    ya = jnp.dot(ua, wa_ref[...], preferred_element_type=F32)
    ym = jnp.dot(um_ref[...], wm_ref[...], preferred_element_type=F32)
    y = _sigmoid(ga_ref[...].astype(F32)) * ya + _sigmoid(gm_ref[...].astype(F32)) * ym
    gate = ada_ref[0][2:3, :]
    o_ref[...] = x_ref[...] + gate * jnp.dot(y.astype(BF16), wo_ref[...], preferred_element_type=F32)


def _outp_call(ua, um, p, x2, ada3, wa, wm, wo, seq):
    T, D = x2.shape
    tm = OUT_TM
    steps_per_seq = seq // tm
    blk = MOBA_BLOCK
    nq = seq // blk
    tiles_per_step = tm // blk

    def ua_spec(e):
        def index(i):
            t = (i % steps_per_seq) * tiles_per_step + e
            return ((i // steps_per_seq) * nq + _ua_tile_row(t, nq), 0)
        return pl.BlockSpec((blk, D), index)

    act = lambda col: pl.BlockSpec((tm, D), lambda i: (i, col))
    wspec = pl.BlockSpec((D, D), lambda i: (0, 0))
    return pl.pallas_call(
        _outp_kernel,
        grid=(T // tm,),
        in_specs=[ua_spec(e) for e in range(tiles_per_step)]
                 + [act(0), act(COL_GA), act(COL_GM), act(0),
                    pl.BlockSpec((1, 3, D), lambda i: (i // steps_per_seq, 0, 0)),
                    wspec, wspec, wspec],
        out_specs=pl.BlockSpec((tm, D), lambda i: (i, 0)),
        out_shape=jax.ShapeDtypeStruct((T, D), F32),
        compiler_params=pltpu.CompilerParams(dimension_semantics=("arbitrary",),
                                             vmem_limit_bytes=VMEM_LIMIT),
        name="outp",
    )(*([ua] * tiles_per_step), um, p, p, x2, ada3, wa, wm, wo)


def _rel_bias_tables(rel_bias):
    max_exact = REL_BUCKETS // 2
    rb = rel_bias.astype(F32)
    rb = (rb - rb[REL_BUCKETS - 1:, :]) * LOG2E
    key = jnp.arange(MOBA_BLOCK)[:, None]
    qry = jnp.arange(MOBA_BLOCK)[None, :]
    d_own = qry - key

    def tile(dist):
        nf = jnp.maximum(dist, 1).astype(F32)
        large = max_exact + (jnp.log(nf / max_exact) / math.log(REL_MAX_DIST / max_exact)
                             * (REL_BUCKETS - max_exact)).astype(jnp.int32)
        bucket = jnp.where(dist < max_exact, dist, jnp.minimum(large, REL_BUCKETS - 1))
        onehot = (bucket[..., None] == jnp.arange(REL_BUCKETS)).astype(F32)
        return jnp.einsum('kqb,bh->hkq', onehot, rb, precision=lax.Precision.HIGHEST)

    own = jnp.where(d_own >= 0, tile(jnp.maximum(d_own, 0)), NEG_INF)
    prev = tile(d_own + MOBA_BLOCK)
    return own, prev


def _layer(x2, c, w_ada, b_ada, norm_w, w_in_all, layer, q_norm_w, k_norm_w, rel_bias, conv_w, conv_b,
           b_igate, b_fgate, ml_norm_w, w_att_proj, w_ml_proj, w_out, batch, seq):
    D = D_MODEL
    g0 = 4 * D + 5 * D
    ada3 = _ada_call(c, w_ada, b_ada).reshape(batch, 3, D)

    wa = w_in_all[layer, :, :g0].astype(BF16)
    wb = w_in_all[layer, :, g0 + GATE_COLS:].astype(BF16)
    wg = jnp.pad(w_in_all[layer, :, g0:g0 + GATE_COLS], ((0, 0), (0, LANES - GATE_COLS))).astype(BF16)
    gb = jnp.pad(jnp.concatenate([b_igate, b_fgate]), (0, LANES - GATE_COLS)).reshape(1, LANES)
    q_scale = ATT_HEAD_DIM ** -0.5 * LOG2E
    hnw = jnp.stack([jnp.tile(q_norm_w.astype(F32), ATT_HEADS) * q_scale,
                     jnp.tile(k_norm_w.astype(F32), ATT_HEADS)])

    p, g, gt = _inproj_call(x2, ada3, norm_w.reshape(1, D), wa, wb, wg, gb, hnw, seq)
    bias_own, bias_prev = _rel_bias_tables(rel_bias)
    ua = _moba_call(p, bias_own, bias_prev, batch, seq).reshape(batch * seq, D)
    um = _mlstm_call(p, g, gt, conv_w, conv_b.reshape(1, -1), ml_norm_w.reshape(1, -1), batch, seq)
    return _outp_call(ua, um, p, x2, ada3, w_att_proj.astype(BF16), w_ml_proj.astype(BF16),
                      w_out.astype(BF16), seq)


def kernel(x, c, w_ada, b_ada, norm_w, w_in, q_norm_w, k_norm_w, rel_bias, conv_w, conv_b, b_igate,
           b_fgate, ml_norm_w, w_att_proj, w_ml_proj, w_out):
    batch, seq, D = x.shape
    assert D == D_MODEL and seq % PROJ_TM == 0 and seq % MOBA_BLOCK == 0 and seq % ML_CHUNK == 0
    x2 = x.reshape(batch * seq, D)
    for l in range(w_in.shape[0]):
        x2 = _layer(x2, c, w_ada[l], b_ada[l], norm_w[l], w_in, l, q_norm_w[l], k_norm_w[l],
                    rel_bias, conv_w[l], conv_b[l], b_igate[l], b_fgate[l], ml_norm_w[l],
                    w_att_proj[l], w_ml_proj[l], w_out[l], batch, seq)
    return x2.reshape(batch, seq, D)
```

```python
import functools
import math

import jax
import jax.numpy as jnp
import numpy as np
from jax import lax
from jax.experimental import pallas as pl
from jax.experimental.pallas import tpu as pltpu

F32 = jnp.float32
BF16 = jnp.bfloat16

D_MODEL = 1024
ATT_HEADS = 8
ATT_HEAD_DIM = D_MODEL // ATT_HEADS
MOBA_BLOCK = 256
MOBA_TOPK = 3
REL_BUCKETS = 32
REL_MAX_DIST = 128
ML_HEADS = 4
ML_HEAD_DIM = D_MODEL // ML_HEADS
CONV_WIDTH = 4
EPS = 1e-6
NEG_INF = -1e30
LOG2E = math.log2(math.e)

LANES = 128
SUBLANES = 8
BF16_SUBLANES = 16
GATE_COLS = 2 * ML_HEADS
ML_CHUNK = 256
PROJ_TM = 1024
PROJ_TN = 1024
OUT_TM = 512
MOBA_HEAD_GROUP = 4
VMEM_LIMIT = 52 * 1024 * 1024

COL_QA, COL_KA, COL_VA, COL_ZA, COL_QM, COL_KM, COL_VM, COL_ZM, COL_OM, COL_GA, COL_GM = range(11)
N_COL_TILES = 11
N_HEAD_TILES = 9

_NT = (((1,), (1,)), ((), ()))
_TN = (((0,), (0,)), ((), ()))


def _sigmoid(x):
    return 1.0 / (1.0 + jnp.exp2(x * (-LOG2E)))


def _log_sigmoid(x):
    return jnp.minimum(x, 0.0) - jnp.log1p(jnp.exp(-jnp.abs(x)))


def _ada_kernel(c_ref, w_ref, b_ref, o_ref):
    o_ref[...] = jnp.dot(c_ref[...], w_ref[...], precision=lax.Precision.HIGHEST,
                         preferred_element_type=F32) + b_ref[...]


def _ada_call(c, w_ada, b_ada):
    B, D = c.shape
    return pl.pallas_call(
        _ada_kernel,
        grid=(3,),
        in_specs=[pl.BlockSpec((B, D), lambda j: (0, 0)),
                  pl.BlockSpec((D, D), lambda j: (0, j)),
                  pl.BlockSpec((1, D), lambda j: (0, j))],
        out_specs=pl.BlockSpec((B, D), lambda j: (0, j)),
        out_shape=jax.ShapeDtypeStruct((B, 3 * D), F32),
        compiler_params=pltpu.CompilerParams(dimension_semantics=("arbitrary",),
                                             vmem_limit_bytes=VMEM_LIMIT),
        name="ada",
    )(c, w_ada, b_ada.reshape(1, 3 * D))


def _inproj_kernel(x_ref, ada_ref, nw_ref, wa_ref, wb_ref, wg_ref, gb_ref, hnw_ref,
                   p_ref, g_ref, gt_ref, h_scr):
    j = pl.program_id(1)
    tn = p_ref.shape[1]

    @pl.when(j == 0)
    def _():
        x = x_ref[...]
        ada = ada_ref[0]
        y = x * lax.rsqrt(jnp.mean(x * x, axis=-1, keepdims=True) + EPS) * nw_ref[...]
        h = (y * (1.0 + ada[1:2, :]) + ada[0:1, :]).astype(BF16)
        h_scr[...] = h
        g = jnp.dot(h, wg_ref[...], preferred_element_type=F32) + gb_ref[...]
        g_ref[...] = g
        gt_ref[...] = g.T[:GATE_COLS, :]

    @pl.when(j <= COL_KA)
    def _():
        h = h_scr[...]
        w = hnw_ref[pl.ds(j, 1), :]
        pair = 2 * ATT_HEAD_DIM
        for c0 in range(0, tn, pair):
            acc = jnp.dot(h, wa_ref[:, c0:c0 + pair], preferred_element_type=F32)
            for c in range(c0, c0 + pair, ATT_HEAD_DIM):
                a = acc[:, c - c0:c - c0 + ATT_HEAD_DIM]
                r = lax.rsqrt(jnp.mean(a * a, axis=-1, keepdims=True) + EPS)
                p_ref[:, c:c + ATT_HEAD_DIM] = (a * r * w[:, c:c + ATT_HEAD_DIM]).astype(BF16)

    @pl.when(jnp.logical_and(j > COL_KA, j < N_HEAD_TILES))
    def _():
        p_ref[...] = jnp.dot(h_scr[...], wa_ref[...], preferred_element_type=F32).astype(BF16)

    @pl.when(j >= N_HEAD_TILES)
    def _():
        p_ref[...] = jnp.dot(h_scr[...], wb_ref[...], preferred_element_type=F32).astype(BF16)


def _inproj_call(x2, ada3, norm_w, wa, wb, wg, gb, hnw, seq):
    T, D = x2.shape
    tm, tn = PROJ_TM, PROJ_TN
    tiles_per_seq = seq // tm
    return pl.pallas_call(
        _inproj_kernel,
        grid=(T // tm, N_COL_TILES),
        in_specs=[pl.BlockSpec((tm, D), lambda i, j: (i, 0)),
                  pl.BlockSpec((1, 3, D), lambda i, j: (i // tiles_per_seq, 0, 0)),
                  pl.BlockSpec((1, D), lambda i, j: (0, 0)),
                  pl.BlockSpec((D, tn), lambda i, j: (0, jnp.minimum(j, N_HEAD_TILES - 1))),
                  pl.BlockSpec((D, tn), lambda i, j: (0, jnp.maximum(j - N_HEAD_TILES, 0))),
                  pl.BlockSpec((D, LANES), lambda i, j: (0, 0)),
                  pl.BlockSpec((1, LANES), lambda i, j: (0, 0)),
                  pl.BlockSpec((2, tn), lambda i, j: (0, 0))],
        out_specs=[pl.BlockSpec((tm, tn), lambda i, j: (i, j)),
                   pl.BlockSpec((tm, LANES), lambda i, j: (i, 0)),
                   pl.BlockSpec((GATE_COLS, tm), lambda i, j: (0, i))],
        out_shape=[jax.ShapeDtypeStruct((T, N_COL_TILES * tn), BF16),
                   jax.ShapeDtypeStruct((T, LANES), F32),
                   jax.ShapeDtypeStruct((GATE_COLS, T), F32)],
        scratch_shapes=[pltpu.VMEM((tm, D), BF16)],
        compiler_params=pltpu.CompilerParams(dimension_semantics=("arbitrary", "arbitrary"),
                                             vmem_limit_bytes=VMEM_LIMIT),
        name="inproj",
    )(x2, ada3, norm_w, wa, wb, wg, gb, hnw)


def _moba_kernel(qa_ref, qb_ref, k_ref, v_ref, zaa_ref, zab_ref, bown_ref, bprev_ref, o_ref,
                 kmean_f32_scr, kmean_scr, vt_scr, q2_scr, selb_scr, s_scr, m8_scr):
    step = pl.program_id(2)
    blk, d = MOBA_BLOCK, ATT_HEAD_DIM
    da = d + BF16_SUBLANES
    nb = k_ref.shape[0] // blk
    heads = qa_ref.shape[1] // d
    tile = (step, nb - 1 - step)
    n_far = (jnp.maximum(step - 1, 0), nb - 2 - step)
    near_slots = 4
    far_slots = nb - near_slots + 1

    def hsl(hd):
        return slice(hd * d, (hd + 1) * d)

    @pl.when(step == 0)
    def _():
        for n in range(nb):
            kb = k_ref[n * blk:(n + 1) * blk, :].astype(F32)
            kmean_f32_scr[n:n + 1, :] = jnp.mean(kb, axis=0, keepdims=True)
            for hd in range(heads):
                vt_scr[n, hd * da:hd * da + d, :] = v_ref[n * blk:(n + 1) * blk, hsl(hd)].T
                vt_scr[n, hd * da + d:(hd + 1) * da, :] = jnp.ones((da - d, blk), BF16)

        rest = kmean_f32_scr[...]
        for c in range(3):
            term = rest.astype(BF16)
            kmean_scr[c * nb:(c + 1) * nb, :] = term
            rest = rest - term.astype(F32)

    q2_scr[0] = qa_ref[...]
    q2_scr[1] = qb_ref[...]

    def fold(x):
        return x.reshape(blk // SUBLANES, SUBLANES, blk)

    def col_max8(s):
        x = fold(s)
        n = x.shape[0]
        while n > 1:
            n //= 2
            x = jnp.maximum(x[:n], x[n:2 * n])
        return x[0]

    rows = lax.broadcasted_iota(jnp.int32, (nb, blk), 0)
    for x in range(2):
        valid = rows < tile[x]
        for hd in range(heads):
            terms = lax.dot_general(kmean_scr[:, hsl(hd)], q2_scr[x, :, hsl(hd)], _NT,
                                    preferred_element_type=F32)
            gate = terms[:nb] + terms[nb:2 * nb] + terms[2 * nb:]
            g = jnp.where(valid, gate, -jnp.inf)
            selb = jnp.full(gate.shape, NEG_INF, F32)
            for _ in range(MOBA_TOPK):
                mx = jnp.max(g, axis=0, keepdims=True)
                first = jnp.min(jnp.where(g == mx, rows, nb), axis=0, keepdims=True)
                pick = rows == first
                selb = jnp.where(pick, 0.0, selb)
                g = jnp.where(pick, -jnp.inf, g)
            selb_scr[x, hd] = jnp.where(valid, selb, NEG_INF)

    def near_items(hd):
        first = step == 0
        jb = tile[1] - 1
        ja = jnp.where(first, far_slots, step - 1)
        add_a = jnp.where(first, selb_scr[1, hd, pl.ds(far_slots, 1), :],
                          bprev_ref[hd] + selb_scr[0, hd, pl.ds(jnp.maximum(step - 1, 0), 1), :])
        return [(0, 0, tile[0], bown_ref[hd], None),
                (1, 1, tile[1], bown_ref[hd], None),
                (2, 1, jb, bprev_ref[hd] + selb_scr[1, hd, pl.ds(jb, 1), :], None),
                (3, jnp.where(first, 1, 0), ja, add_a, jnp.logical_not(first))]

    shared_slots = nb // 2 - 2

    def far_items(hd):
        out = []
        for t in range(far_slots):
            if t < shared_slots:
                is_a = t < n_far[0]
                x = jnp.where(is_a, 0, 1)
                j = jnp.where(is_a, t, t - n_far[0])
            else:
                is_a, x, j = None, 1, t - n_far[0]
            out.append((near_slots + t, x, j, selb_scr[x, hd, pl.ds(j, 1), :], is_a))
        return out

    items = [near_items(hd) + far_items(hd) for hd in range(heads)]
    m8 = [[None, None] for _ in range(heads)]
    groups = [range(g, min(g + 1, heads)) for g in range(0, heads, 1)]
    order = [(i, grp) for grp in groups for i in range(near_slots + far_slots)]
    for i, grp in order:
        for hd in grp:
            slot, x, j, add, is_a = items[hd][i]
            kj = k_ref[pl.ds(pl.multiple_of(j * blk, blk), blk), hsl(hd)]
            s = lax.dot_general(kj, q2_scr[x, :, hsl(hd)], _NT, preferred_element_type=F32) + add
            s_scr[hd, slot] = s
            mt = col_max8(s)
            for y in range(2):
                if is_a is None:
                    if y != x:
                        continue
                    part = mt
                else:
                    part = jnp.where(is_a, mt, -jnp.inf) if y == 0 else jnp.where(is_a, -jnp.inf, mt)
                m8[hd][y] = part if m8[hd][y] is None else jnp.maximum(m8[hd][y], part)
    for hd in range(heads):
        for x in range(2):
            m8_scr[x, hd] = jnp.broadcast_to(jnp.max(m8[hd][x], axis=0, keepdims=True), (SUBLANES, blk))

    acc = [[None, None] for _ in range(heads)]
    acc_far = [None] * heads

    def add_to(lst, k, v):
        lst[k] = v if lst[k] is None else lst[k] + v

    for i, grp in order:
        for hd in grp:
            slot, x, j, _, is_a = items[hd][i]
            p = jnp.exp2(fold(s_scr[hd, slot]) - m8_scr[x, hd])
            pb = p.reshape(blk, blk).astype(BF16)
            vt = vt_scr[j, hd * da:(hd + 1) * da, :]
            if is_a is None:
                add_to(acc[hd], x, jnp.dot(vt, pb, preferred_element_type=F32))
            else:
                zero = jnp.zeros_like(vt)
                lhs = jnp.concatenate([jnp.where(is_a, vt, zero), jnp.where(is_a, zero, vt)], axis=0)
                add_to(acc_far, hd, jnp.dot(lhs, pb, preferred_element_type=F32))

    for hd in range(heads):
        for x, za_ref in enumerate((zaa_ref, zab_ref)):
            tot = acc[hd][x] + acc_far[hd][x * da:(x + 1) * da]
            y = (tot[:d] * (1.0 / tot[d:d + 1])).T
            za = za_ref[:, hsl(hd)].astype(F32)
            o_ref[0, x, 0, :, hsl(hd)] = (y * (za * _sigmoid(za))).astype(BF16)


def _moba_call(p, bias_own, bias_prev, batch, seq):
    blk, d = MOBA_BLOCK, ATT_HEAD_DIM
    nq = seq // blk
    hg = MOBA_HEAD_GROUP
    w = hg * d
    gpt = PROJ_TN // w
    first = lambda col: (lambda b, h, i: (b * nq + i, col * gpt + h))
    last = lambda col: (lambda b, h, i: (b * nq + nq - 1 - i, col * gpt + h))
    return pl.pallas_call(
        _moba_kernel,
        grid=(batch, ATT_HEADS // hg, nq // 2),
        in_specs=[pl.BlockSpec((blk, w), first(COL_QA)),
                  pl.BlockSpec((blk, w), last(COL_QA)),
                  pl.BlockSpec((seq, w), lambda b, h, i: (b, COL_KA * gpt + h)),
                  pl.BlockSpec((seq, w), lambda b, h, i: (b, COL_VA * gpt + h)),
                  pl.BlockSpec((blk, w), first(COL_ZA)),
                  pl.BlockSpec((blk, w), last(COL_ZA)),
                  pl.BlockSpec((hg, blk, blk), lambda b, h, i: (h, 0, 0)),
                  pl.BlockSpec((hg, blk, blk), lambda b, h, i: (h, 0, 0))],
        out_specs=pl.BlockSpec((1, 2, 1, blk, w), lambda b, h, i: (b, 0, i, 0, h)),
        out_shape=jax.ShapeDtypeStruct((batch, 2, nq // 2, blk, ATT_HEADS * d), BF16),
        scratch_shapes=[pltpu.VMEM((nq, w), F32),
                        pltpu.VMEM((3 * nq, w), BF16),
                        pltpu.VMEM((nq, hg * (d + BF16_SUBLANES), blk), BF16),
                        pltpu.VMEM((2, blk, w), BF16),
                        pltpu.VMEM((2, hg, nq, blk), F32),
                        pltpu.VMEM((hg, nq + 1, blk, blk), F32),
                        pltpu.VMEM((2, hg, SUBLANES, blk), F32)],
        compiler_params=pltpu.CompilerParams(
            dimension_semantics=("arbitrary", "arbitrary", "arbitrary"),
            vmem_limit_bytes=VMEM_LIMIT),
        name="moba",
    )(p, p, p, p, p, p, bias_own, bias_prev)


def _ua_tile_row(t, nq):
    half = nq // 2
    return jnp.where(t < half, t, half + (nq - 1 - t))


def _mlstm_kernel(qm_ref, km_ref, vm_ref, zm_ref, om_ref, g_ref, gt_ref, cw_ref, cb_ref, mlw_ref,
                  o_ref, c_scr, n_scr, m_scr, tailq_scr, tailk_scr, shift_scr):
    c_idx = pl.program_id(1)
    L = qm_ref.shape[0]
    W = ML_HEADS * ML_HEAD_DIM
    pad = SUBLANES

    r_i = lax.broadcasted_iota(jnp.int32, (L, L), 0)
    c_i = lax.broadcasted_iota(jnp.int32, (L, L), 1)

    @pl.when(c_idx == 0)
    def _():
        c_scr[...] = jnp.zeros_like(c_scr)
        n_scr[...] = jnp.zeros_like(n_scr)
        m_scr[...] = jnp.zeros_like(m_scr)
        tailq_scr[...] = jnp.zeros_like(tailq_scr)
        tailk_scr[...] = jnp.zeros_like(tailk_scr)
        for k in range(CONV_WIDTH):
            shift_scr[k] = (c_i == r_i - k).astype(F32).astype(BF16)

    row8 = lax.broadcasted_iota(jnp.int32, (pad, W), 0)

    def conv_silu(src_ref, tail_scr, col0):
        x = src_ref[...]
        tail = tail_scr[...]
        y = cb_ref[:, col0:col0 + W]
        fix = jnp.zeros((pad, W), F32)
        for w in range(CONV_WIDTH):
            k = CONV_WIDTH - 1 - w
            cw = cw_ref[w:w + 1, col0:col0 + W]
            y = y + cw * jnp.dot(shift_scr[k], x, preferred_element_type=F32)
            if k:
                fix = fix + cw * jnp.where(row8 < k, pltpu.roll(tail, k, axis=0), 0.0)
        tail_scr[...] = x[L - 2 * pad:, :].astype(F32)[pad:, :]
        y = jnp.concatenate([y[:pad] + fix, y[pad:]], axis=0)
        return y * _sigmoid(y)

    qc = conv_silu(qm_ref, tailq_scr, 0)
    kc = conv_silu(km_ref, tailk_scr, W) * (ML_HEAD_DIM ** -0.5)

    causal = c_i <= r_i
    ltri = causal.astype(F32)
    utri = (r_i <= c_i).astype(F32)
    g_col = g_ref[...]
    g_row = gt_ref[...]
    b_col_all = jnp.dot(ltri, _log_sigmoid(g_col), precision=lax.Precision.HIGHEST,
                        preferred_element_type=F32)
    b_row_all = jnp.dot(_log_sigmoid(g_row), utri, precision=lax.Precision.HIGHEST,
                        preferred_element_type=F32)

    for hh in range(ML_HEADS):
        sl = slice(hh * ML_HEAD_DIM, (hh + 1) * ML_HEAD_DIM)
        q = qc[:, sl]
        k = kc[:, sl]
        v = vm_ref[:, sl]
        b_col = b_col_all[:, ML_HEADS + hh:ML_HEADS + hh + 1]
        a_col = g_col[:, hh:hh + 1] - b_col
        a_row = g_row[hh:hh + 1, :] - b_row_all[ML_HEADS + hh:ML_HEADS + hh + 1, :]
        m_prev = m_scr[hh][0:1, 0:1]

        a_mask = jnp.where(causal, a_row, -jnp.inf)
        gcol = jnp.maximum(m_prev, jnp.max(a_mask, axis=1, keepdims=True))
        d_mat = jnp.exp(a_mask - gcol)
        inter = jnp.exp(m_prev - gcol)
        g_last = jnp.max(gcol, axis=0, keepdims=True)

        qb = q.astype(BF16)
        s = lax.dot_general(qb, k.astype(BF16), _NT, preferred_element_type=F32) * d_mat
        c_old = c_scr[hh]
        n_old = n_scr[hh]
        num = (inter * jnp.dot(qb, c_old.astype(BF16), preferred_element_type=F32)
               + jnp.dot(s.astype(BF16), v, preferred_element_type=F32))
        den = inter * jnp.sum(q * n_old, axis=1, keepdims=True) + jnp.sum(s, axis=1, keepdims=True)
        h = num / jnp.maximum(jnp.abs(den), jnp.exp(-(b_col + gcol)))

        decay = jnp.exp(m_prev - g_last)
        kw = k * jnp.exp(a_col - g_last)
        c_scr[hh] = decay * c_old + lax.dot_general(kw.astype(BF16), v, _TN,
                                                    preferred_element_type=F32)
        n_scr[hh] = decay * n_old + jnp.sum(kw, axis=0, keepdims=True)
        m_scr[hh] = jnp.broadcast_to(b_col[L - 1:L, :] + g_last, m_scr.shape[1:])

        hg = _sigmoid(om_ref[:, sl].astype(F32)) * h
        hn = hg * lax.rsqrt(jnp.mean(hg * hg, axis=-1, keepdims=True) + EPS) * mlw_ref[:, sl]
        z = zm_ref[:, sl].astype(F32)
        o_ref[:, sl] = (hn * (z * _sigmoid(z))).astype(BF16)


def _mlstm_call(p, g, gt, conv_w, conv_b, ml_norm_w, batch, seq):
    T = p.shape[0]
    L = ML_CHUNK
    nc = seq // L
    W = ML_HEADS * ML_HEAD_DIM
    row = lambda b, c: b * nc + c
    pspec = lambda col: pl.BlockSpec((L, W), lambda b, c: (row(b, c), col))
    return pl.pallas_call(
        _mlstm_kernel,
        grid=(batch, nc),
        in_specs=[pspec(COL_QM), pspec(COL_KM), pspec(COL_VM), pspec(COL_ZM), pspec(COL_OM),
                  pl.BlockSpec((L, LANES), lambda b, c: (row(b, c), 0)),
                  pl.BlockSpec((GATE_COLS, L), lambda b, c: (0, row(b, c))),
                  pl.BlockSpec((CONV_WIDTH, 2 * W), lambda b, c: (0, 0)),
                  pl.BlockSpec((1, 2 * W), lambda b, c: (0, 0)),
                  pl.BlockSpec((1, W), lambda b, c: (0, 0))],
        out_specs=pl.BlockSpec((L, W), lambda b, c: (row(b, c), 0)),
        out_shape=jax.ShapeDtypeStruct((T, W), BF16),
        scratch_shapes=[pltpu.VMEM((ML_HEADS, ML_HEAD_DIM, ML_HEAD_DIM), F32),
                        pltpu.VMEM((ML_HEADS, 1, ML_HEAD_DIM), F32),
                        pltpu.VMEM((ML_HEADS, SUBLANES, LANES), F32),
                        pltpu.VMEM((SUBLANES, W), F32),
                        pltpu.VMEM((SUBLANES, W), F32),
                        pltpu.VMEM((CONV_WIDTH, L, L), BF16)],
        compiler_params=pltpu.CompilerParams(dimension_semantics=("arbitrary", "arbitrary"),
                                             vmem_limit_bytes=VMEM_LIMIT),
        name="mlstm",
    )(p, p, p, p, p, g, gt, conv_w, conv_b, ml_norm_w)


def _outp_kernel(*refs):
    ua_refs, (um_ref, ga_ref, gm_ref, x_ref, ada_ref, wa_ref, wm_ref, wo_ref, o_ref) = refs[:-9], refs[-9:]
    ua = jnp.concatenate([r[...] for r in ua_refs], axis=0)
    ya = jnp.dot(ua, wa_ref[...], preferred_element_type=F32)
    ym = jnp.dot(um_ref[...], wm_ref[...], preferred_element_type=F32)
    y = _sigmoid(ga_ref[...].astype(F32)) * ya + _sigmoid(gm_ref[...].astype(F32)) * ym
    gate = ada_ref[0][2:3, :]
    o_ref[...] = x_ref[...] + gate * jnp.dot(y.astype(BF16), wo_ref[...], preferred_element_type=F32)


def _outp_call(ua, um, p, x2, ada3, wa, wm, wo, seq):
    T, D = x2.shape
    tm = OUT_TM
    steps_per_seq = seq // tm
    blk = MOBA_BLOCK
    nq = seq // blk
    tiles_per_step = tm // blk

    def ua_spec(e):
        def index(i):
            t = (i % steps_per_seq) * tiles_per_step + e
            return ((i // steps_per_seq) * nq + _ua_tile_row(t, nq), 0)
        return pl.BlockSpec((blk, D), index)

    act = lambda col: pl.BlockSpec((tm, D), lambda i: (i, col))
    wspec = pl.BlockSpec((D, D), lambda i: (0, 0))
    return pl.pallas_call(
        _outp_kernel,
        grid=(T // tm,),
        in_specs=[ua_spec(e) for e in range(tiles_per_step)]
                 + [act(0), act(COL_GA), act(COL_GM), act(0),
                    pl.BlockSpec((1, 3, D), lambda i: (i // steps_per_seq, 0, 0)),
                    wspec, wspec, wspec],
        out_specs=pl.BlockSpec((tm, D), lambda i: (i, 0)),
        out_shape=jax.ShapeDtypeStruct((T, D), F32),
        compiler_params=pltpu.CompilerParams(dimension_semantics=("arbitrary",),
                                             vmem_limit_bytes=VMEM_LIMIT),
        name="outp",
    )(*([ua] * tiles_per_step), um, p, p, x2, ada3, wa, wm, wo)


def _rel_bias_tables(rel_bias):
    max_exact = REL_BUCKETS // 2
    rb = rel_bias.astype(F32)
    rb = (rb - rb[REL_BUCKETS - 1:, :]) * LOG2E
    key = jnp.arange(MOBA_BLOCK)[:, None]
    qry = jnp.arange(MOBA_BLOCK)[None, :]
    d_own = qry - key

    def tile(dist):
        nf = jnp.maximum(dist, 1).astype(F32)
        large = max_exact + (jnp.log(nf / max_exact) / math.log(REL_MAX_DIST / max_exact)
                             * (REL_BUCKETS - max_exact)).astype(jnp.int32)
        bucket = jnp.where(dist < max_exact, dist, jnp.minimum(large, REL_BUCKETS - 1))
        onehot = (bucket[..., None] == jnp.arange(REL_BUCKETS)).astype(F32)
        return jnp.einsum('kqb,bh->hkq', onehot, rb, precision=lax.Precision.HIGHEST)

    own = jnp.where(d_own >= 0, tile(jnp.maximum(d_own, 0)), NEG_INF)
    prev = tile(d_own + MOBA_BLOCK)
    return own, prev


def _layer(x2, c, w_ada, b_ada, norm_w, w_in_all, layer, q_norm_w, k_norm_w, rel_bias, conv_w, conv_b,
           b_igate, b_fgate, ml_norm_w, w_att_proj, w_ml_proj, w_out, batch, seq):
    D = D_MODEL
    g0 = 4 * D + 5 * D
    ada3 = _ada_call(c, w_ada, b_ada).reshape(batch, 3, D)

    wa = w_in_all[layer, :, :g0].astype(BF16)
    wb = w_in_all[layer, :, g0 + GATE_COLS:].astype(BF16)
    wg = jnp.pad(w_in_all[layer, :, g0:g0 + GATE_COLS], ((0, 0), (0, LANES - GATE_COLS))).astype(BF16)
    gb = jnp.pad(jnp.concatenate([b_igate, b_fgate]), (0, LANES - GATE_COLS)).reshape(1, LANES)
    q_scale = ATT_HEAD_DIM ** -0.5 * LOG2E
    hnw = jnp.stack([jnp.tile(q_norm_w.astype(F32), ATT_HEADS) * q_scale,
                     jnp.tile(k_norm_w.astype(F32), ATT_HEADS)])

    p, g, gt = _inproj_call(x2, ada3, norm_w.reshape(1, D), wa, wb, wg, gb, hnw, seq)
    bias_own, bias_prev = _rel_bias_tables(rel_bias)
    ua = _moba_call(p, bias_own, bias_prev, batch, seq).reshape(batch * seq, D)
    um = _mlstm_call(p, g, gt, conv_w, conv_b.reshape(1, -1), ml_norm_w.reshape(1, -1), batch, seq)
    return _outp_call(ua, um, p, x2, ada3, w_att_proj.astype(BF16), w_ml_proj.astype(BF16),
                      w_out.astype(BF16), seq)


def kernel(x, c, w_ada, b_ada, norm_w, w_in, q_norm_w, k_norm_w, rel_bias, conv_w, conv_b, b_igate,
           b_fgate, ml_norm_w, w_att_proj, w_ml_proj, w_out):
    batch, seq, D = x.shape
    assert D == D_MODEL and seq % PROJ_TM == 0 and seq % MOBA_BLOCK == 0 and seq % ML_CHUNK == 0
    x2 = x.reshape(batch * seq, D)
    for l in range(w_in.shape[0]):
        x2 = _layer(x2, c, w_ada[l], b_ada[l], norm_w[l], w_in, l, q_norm_w[l], k_norm_w[l],
                    rel_bias, conv_w[l], conv_b[l], b_igate[l], b_fgate[l], ml_norm_w[l],
                    w_att_proj[l], w_ml_proj[l], w_out[l], batch, seq)
    return x2.reshape(batch, seq, D)
```

```python
import functools
import math

import jax
import jax.numpy as jnp
import numpy as np
from jax import lax
from jax.experimental import pallas as pl
from jax.experimental.pallas import tpu as pltpu

F32 = jnp.float32
BF16 = jnp.bfloat16

D_MODEL = 1024
ATT_HEADS = 8
ATT_HEAD_DIM = D_MODEL // ATT_HEADS
MOBA_BLOCK = 256
MOBA_TOPK = 3
REL_BUCKETS = 32
REL_MAX_DIST = 128
ML_HEADS = 4
ML_HEAD_DIM = D_MODEL // ML_HEADS
CONV_WIDTH = 4
EPS = 1e-6
NEG_INF = -1e30
LOG2E = math.log2(math.e)

LANES = 128
SUBLANES = 8
BF16_SUBLANES = 16
GATE_COLS = 2 * ML_HEADS
ML_CHUNK = 256
PROJ_TM = 1024
PROJ_TN = 1024
OUT_TM = 512
MOBA_HEAD_GROUP = 4
VMEM_LIMIT = 52 * 1024 * 1024

COL_QA, COL_KA, COL_VA, COL_ZA, COL_QM, COL_KM, COL_VM, COL_ZM, COL_OM, COL_GA, COL_GM = range(11)
N_COL_TILES = 11
N_HEAD_TILES = 9

_NT = (((1,), (1,)), ((), ()))
_TN = (((0,), (0,)), ((), ()))


def _sigmoid(x):
    return 1.0 / (1.0 + jnp.exp2(x * (-LOG2E)))


def _log_sigmoid(x):
    return jnp.minimum(x, 0.0) - jnp.log1p(jnp.exp(-jnp.abs(x)))


def _ada_kernel(c_ref, w_ref, b_ref, o_ref):
    o_ref[...] = jnp.dot(c_ref[...], w_ref[...], precision=lax.Precision.HIGHEST,
                         preferred_element_type=F32) + b_ref[...]


def _ada_call(c, w_ada, b_ada):
    B, D = c.shape
    return pl.pallas_call(
        _ada_kernel,
        grid=(3,),
        in_specs=[pl.BlockSpec((B, D), lambda j: (0, 0)),
                  pl.BlockSpec((D, D), lambda j: (0, j)),
                  pl.BlockSpec((1, D), lambda j: (0, j))],
        out_specs=pl.BlockSpec((B, D), lambda j: (0, j)),
        out_shape=jax.ShapeDtypeStruct((B, 3 * D), F32),
        compiler_params=pltpu.CompilerParams(dimension_semantics=("arbitrary",),
                                             vmem_limit_bytes=VMEM_LIMIT),
        name="ada",
    )(c, w_ada, b_ada.reshape(1, 3 * D))


def _inproj_kernel(x_ref, ada_ref, nw_ref, wa_ref, wb_ref, wg_ref, gb_ref, hnw_ref,
                   p_ref, g_ref, gt_ref, h_scr):
    j = pl.program_id(1)
    tn = p_ref.shape[1]

    @pl.when(j == 0)
    def _():
        x = x_ref[...]
        ada = ada_ref[0]
        y = x * lax.rsqrt(jnp.mean(x * x, axis=-1, keepdims=True) + EPS) * nw_ref[...]
        h = (y * (1.0 + ada[1:2, :]) + ada[0:1, :]).astype(BF16)
        h_scr[...] = h
        g = jnp.dot(h, wg_ref[...], preferred_element_type=F32) + gb_ref[...]
        g_ref[...] = g
        gt_ref[...] = g.T[:GATE_COLS, :]

    @pl.when(j <= COL_KA)
    def _():
        h = h_scr[...]
        w = hnw_ref[pl.ds(j, 1), :]
        pair = 2 * ATT_HEAD_DIM
        for c0 in range(0, tn, pair):
            acc = jnp.dot(h, wa_ref[:, c0:c0 + pair], preferred_element_type=F32)
            for c in range(c0, c0 + pair, ATT_HEAD_DIM):
                a = acc[:, c - c0:c - c0 + ATT_HEAD_DIM]
                r = lax.rsqrt(jnp.mean(a * a, axis=-1, keepdims=True) + EPS)
                p_ref[:, c:c + ATT_HEAD_DIM] = (a * r * w[:, c:c + ATT_HEAD_DIM]).astype(BF16)

    @pl.when(jnp.logical_and(j > COL_KA, j < N_HEAD_TILES))
    def _():
        p_ref[...] = jnp.dot(h_scr[...], wa_ref[...], preferred_element_type=F32).astype(BF16)

    @pl.when(j >= N_HEAD_TILES)
    def _():
        p_ref[...] = jnp.dot(h_scr[...], wb_ref[...], preferred_element_type=F32).astype(BF16)


def _inproj_call(x2, ada3, norm_w, wa, wb, wg, gb, hnw, seq):
    T, D = x2.shape
    tm, tn = PROJ_TM, PROJ_TN
    tiles_per_seq = seq // tm
    return pl.pallas_call(
        _inproj_kernel,
        grid=(T // tm, N_COL_TILES),
        in_specs=[pl.BlockSpec((tm, D), lambda i, j: (i, 0)),
                  pl.BlockSpec((1, 3, D), lambda i, j: (i // tiles_per_seq, 0, 0)),
                  pl.BlockSpec((1, D), lambda i, j: (0, 0)),
                  pl.BlockSpec((D, tn), lambda i, j: (0, jnp.minimum(j, N_HEAD_TILES - 1))),
                  pl.BlockSpec((D, tn), lambda i, j: (0, jnp.maximum(j - N_HEAD_TILES, 0))),
                  pl.BlockSpec((D, LANES), lambda i, j: (0, 0)),
                  pl.BlockSpec((1, LANES), lambda i, j: (0, 0)),
                  pl.BlockSpec((2, tn), lambda i, j: (0, 0))],
        out_specs=[pl.BlockSpec((tm, tn), lambda i, j: (i, j)),
                   pl.BlockSpec((tm, LANES), lambda i, j: (i, 0)),
                   pl.BlockSpec((GATE_COLS, tm), lambda i, j: (0, i))],
        out_shape=[jax.ShapeDtypeStruct((T, N_COL_TILES * tn), BF16),
                   jax.ShapeDtypeStruct((T, LANES), F32),
                   jax.ShapeDtypeStruct((GATE_COLS, T), F32)],
        scratch_shapes=[pltpu.VMEM((tm, D), BF16)],
        compiler_params=pltpu.CompilerParams(dimension_semantics=("arbitrary", "arbitrary"),
                                             vmem_limit_bytes=VMEM_LIMIT),
        name="inproj",
    )(x2, ada3, norm_w, wa, wb, wg, gb, hnw)


def _moba_kernel(qa_ref, qb_ref, k_ref, v_ref, zaa_ref, zab_ref, bown_ref, bprev_ref, o_ref,
                 kmean_f32_scr, kmean_scr, vt_scr, q2_scr, selb_scr, s_scr, m8_scr):
    step = pl.program_id(2)
    blk, d = MOBA_BLOCK, ATT_HEAD_DIM
    da = d + BF16_SUBLANES
    nb = k_ref.shape[0] // blk
    heads = qa_ref.shape[1] // d
    tile = (step, nb - 1 - step)
    n_far = (jnp.maximum(step - 1, 0), nb - 2 - step)
    near_slots = 4
    far_slots = nb - near_slots + 1

    def hsl(hd):
        return slice(hd * d, (hd + 1) * d)

    @pl.when(step == 0)
    def _():
        for n in range(nb):
            kb = k_ref[n * blk:(n + 1) * blk, :].astype(F32)
            kmean_f32_scr[n:n + 1, :] = jnp.mean(kb, axis=0, keepdims=True)
            for hd in range(heads):
                vt_scr[n, hd * da:hd * da + d, :] = v_ref[n * blk:(n + 1) * blk, hsl(hd)].T
                vt_scr[n, hd * da + d:(hd + 1) * da, :] = jnp.ones((da - d, blk), BF16)

        rest = kmean_f32_scr[...]
        for c in range(3):
            term = rest.astype(BF16)
            kmean_scr[c * nb:(c + 1) * nb, :] = term
            rest = rest - term.astype(F32)

    q2_scr[0] = qa_ref[...]
    q2_scr[1] = qb_ref[...]

    def fold(x):
        return x.reshape(blk // SUBLANES, SUBLANES, blk)

    def col_max8(s):
        x = fold(s)
        n = x.shape[0]
        while n > 1:
            n //= 2
            x = jnp.maximum(x[:n], x[n:2 * n])
        return x[0]

    rows = lax.broadcasted_iota(jnp.int32, (nb, blk), 0)

    def select_blocks(hd):
        for x in range(2):
            valid = rows < tile[x]
            terms = lax.dot_general(kmean_scr[:, hsl(hd)], q2_scr[x, :, hsl(hd)], _NT,
                                    preferred_element_type=F32)
            gate = terms[:nb] + terms[nb:2 * nb] + terms[2 * nb:]
            g = jnp.where(valid, gate, -jnp.inf)
            selb = jnp.full(gate.shape, NEG_INF, F32)
            for _ in range(MOBA_TOPK):
                mx = jnp.max(g, axis=0, keepdims=True)
                first = jnp.min(jnp.where(g == mx, rows, nb), axis=0, keepdims=True)
                pick = rows == first
                selb = jnp.where(pick, 0.0, selb)
                g = jnp.where(pick, -jnp.inf, g)
            selb_scr[x, hd] = jnp.where(valid, selb, NEG_INF)

    def near_items(hd):
        first = step == 0
        jb = tile[1] - 1
        ja = jnp.where(first, far_slots, step - 1)
        add_a = jnp.where(first, selb_scr[1, hd, pl.ds(far_slots, 1), :],
                          bprev_ref[hd] + selb_scr[0, hd, pl.ds(jnp.maximum(step - 1, 0), 1), :])
        return [(0, 0, tile[0], bown_ref[hd], None),
                (1, 1, tile[1], bown_ref[hd], None),
                (2, 1, jb, bprev_ref[hd] + selb_scr[1, hd, pl.ds(jb, 1), :], None),
                (3, jnp.where(first, 1, 0), ja, add_a, jnp.logical_not(first))]

    shared_slots = nb // 2 - 2

    def far_items(hd):
        out = []
        for t in range(far_slots):
            if t < shared_slots:
                is_a = t < n_far[0]
                x = jnp.where(is_a, 0, 1)
                j = jnp.where(is_a, t, t - n_far[0])
            else:
                is_a, x, j = None, 1, t - n_far[0]
            out.append((near_slots + t, x, j, selb_scr[x, hd, pl.ds(j, 1), :], is_a))
        return out

    def pass1(hd, items):
        m8 = [None, None]
        for slot, x, j, add, is_a in items:
            kj = k_ref[pl.ds(pl.multiple_of(j * blk, blk), blk), hsl(hd)]
            s = lax.dot_general(kj, q2_scr[x, :, hsl(hd)], _NT, preferred_element_type=F32) + add
            s_scr[hd, slot] = s
            mt = col_max8(s)
            for y in range(2):
                if is_a is None:
                    if y != x:
                        continue
                    part = mt
                else:
                    part = jnp.where(is_a, mt, -jnp.inf) if y == 0 else jnp.where(is_a, -jnp.inf, mt)
                m8[y] = part if m8[y] is None else jnp.maximum(m8[y], part)
        for x in range(2):
            m8_scr[x, hd] = jnp.broadcast_to(jnp.max(m8[x], axis=0, keepdims=True), (SUBLANES, blk))

    def pass2(hd, items):
        acc = [None, None]
        acc_shared = None
        for slot, x, j, _, is_a in items:
            p = jnp.exp2(fold(s_scr[hd, slot]) - m8_scr[x, hd])
            pb = p.reshape(blk, blk).astype(BF16)
            vt = vt_scr[j, hd * da:(hd + 1) * da, :]
            if is_a is None:
                pv = jnp.dot(vt, pb, preferred_element_type=F32)
                acc[x] = pv if acc[x] is None else acc[x] + pv
            else:
                zero = jnp.zeros_like(vt)
                lhs = jnp.concatenate([jnp.where(is_a, vt, zero), jnp.where(is_a, zero, vt)], axis=0)
                pv = jnp.dot(lhs, pb, preferred_element_type=F32)
                acc_shared = pv if acc_shared is None else acc_shared + pv
        for x, za_ref in enumerate((zaa_ref, zab_ref)):
            tot = acc[x] + acc_shared[x * da:(x + 1) * da]
            y = (tot[:d] * (1.0 / tot[d:d + 1])).T
            za = za_ref[:, hsl(hd)].astype(F32)
            o_ref[0, x, 0, :, hsl(hd)] = (y * (za * _sigmoid(za))).astype(BF16)

    items = []
    for hd in range(heads):
        select_blocks(hd)
        items.append(near_items(hd) + far_items(hd))
        pass1(hd, items[hd])
        if hd:
            pass2(hd - 1, items[hd - 1])
    pass2(heads - 1, items[heads - 1])


def _moba_call(p, bias_own, bias_prev, batch, seq):
    blk, d = MOBA_BLOCK, ATT_HEAD_DIM
    nq = seq // blk
    hg = MOBA_HEAD_GROUP
    w = hg * d
    gpt = PROJ_TN // w
    first = lambda col: (lambda b, h, i: (b * nq + i, col * gpt + h))
    last = lambda col: (lambda b, h, i: (b * nq + nq - 1 - i, col * gpt + h))
    return pl.pallas_call(
        _moba_kernel,
        grid=(batch, ATT_HEADS // hg, nq // 2),
        in_specs=[pl.BlockSpec((blk, w), first(COL_QA)),
                  pl.BlockSpec((blk, w), last(COL_QA)),
                  pl.BlockSpec((seq, w), lambda b, h, i: (b, COL_KA * gpt + h)),
                  pl.BlockSpec((seq, w), lambda b, h, i: (b, COL_VA * gpt + h)),
                  pl.BlockSpec((blk, w), first(COL_ZA)),
                  pl.BlockSpec((blk, w), last(COL_ZA)),
                  pl.BlockSpec((hg, blk, blk), lambda b, h, i: (h, 0, 0)),
                  pl.BlockSpec((hg, blk, blk), lambda b, h, i: (h, 0, 0))],
        out_specs=pl.BlockSpec((1, 2, 1, blk, w), lambda b, h, i: (b, 0, i, 0, h)),
        out_shape=jax.ShapeDtypeStruct((batch, 2, nq // 2, blk, ATT_HEADS * d), BF16),
        scratch_shapes=[pltpu.VMEM((nq, w), F32),
                        pltpu.VMEM((3 * nq, w), BF16),
                        pltpu.VMEM((nq, hg * (d + BF16_SUBLANES), blk), BF16),
                        pltpu.VMEM((2, blk, w), BF16),
                        pltpu.VMEM((2, hg, nq, blk), F32),
                        pltpu.VMEM((hg, nq + 1, blk, blk), F32),
                        pltpu.VMEM((2, hg, SUBLANES, blk), F32)],
        compiler_params=pltpu.CompilerParams(
            dimension_semantics=("arbitrary", "arbitrary", "arbitrary"),
            vmem_limit_bytes=VMEM_LIMIT),
        name="moba",
    )(p, p, p, p, p, p, bias_own, bias_prev)


def _ua_tile_row(t, nq):
    half = nq // 2
    return jnp.where(t < half, t, half + (nq - 1 - t))


def _mlstm_kernel(qm_ref, km_ref, vm_ref, zm_ref, om_ref, g_ref, gt_ref, cw_ref, cb_ref, mlw_ref,
                  o_ref, c_scr, n_scr, m_scr, tailq_scr, tailk_scr, shift_scr):
    c_idx = pl.program_id(1)
    L = qm_ref.shape[0]
    W = ML_HEADS * ML_HEAD_DIM
    pad = SUBLANES

    r_i = lax.broadcasted_iota(jnp.int32, (L, L), 0)
    c_i = lax.broadcasted_iota(jnp.int32, (L, L), 1)

    @pl.when(c_idx == 0)
    def _():
        c_scr[...] = jnp.zeros_like(c_scr)
        n_scr[...] = jnp.zeros_like(n_scr)
        m_scr[...] = jnp.zeros_like(m_scr)
        tailq_scr[...] = jnp.zeros_like(tailq_scr)
        tailk_scr[...] = jnp.zeros_like(tailk_scr)
        for k in range(CONV_WIDTH):
            shift_scr[k] = (c_i == r_i - k).astype(F32).astype(BF16)

    row8 = lax.broadcasted_iota(jnp.int32, (pad, W), 0)

    def conv_silu(src_ref, tail_scr, col0):
        x = src_ref[...]
        tail = tail_scr[...]
        y = cb_ref[:, col0:col0 + W]
        fix = jnp.zeros((pad, W), F32)
        for w in range(CONV_WIDTH):
            k = CONV_WIDTH - 1 - w
            cw = cw_ref[w:w + 1, col0:col0 + W]
            y = y + cw * jnp.dot(shift_scr[k], x, preferred_element_type=F32)
            if k:
                fix = fix + cw * jnp.where(row8 < k, pltpu.roll(tail, k, axis=0), 0.0)
        tail_scr[...] = x[L - 2 * pad:, :].astype(F32)[pad:, :]
        y = jnp.concatenate([y[:pad] + fix, y[pad:]], axis=0)
        return y * _sigmoid(y)

    qc = conv_silu(qm_ref, tailq_scr, 0)
    kc = conv_silu(km_ref, tailk_scr, W) * (ML_HEAD_DIM ** -0.5)

    causal = c_i <= r_i
    ltri = causal.astype(F32)
    utri = (r_i <= c_i).astype(F32)
    g_col = g_ref[...]
    g_row = gt_ref[...]
    b_col_all = jnp.dot(ltri, _log_sigmoid(g_col), precision=lax.Precision.HIGHEST,
                        preferred_element_type=F32)
    b_row_all = jnp.dot(_log_sigmoid(g_row), utri, precision=lax.Precision.HIGHEST,
                        preferred_element_type=F32)

    for hh in range(ML_HEADS):
        sl = slice(hh * ML_HEAD_DIM, (hh + 1) * ML_HEAD_DIM)
        q = qc[:, sl]
        k = kc[:, sl]
        v = vm_ref[:, sl]
        b_col = b_col_all[:, ML_HEADS + hh:ML_HEADS + hh + 1]
        a_col = g_col[:, hh:hh + 1] - b_col
        a_row = g_row[hh:hh + 1, :] - b_row_all[ML_HEADS + hh:ML_HEADS + hh + 1, :]
        m_prev = m_scr[hh][0:1, 0:1]

        a_mask = jnp.where(causal, a_row, -jnp.inf)
        gcol = jnp.maximum(m_prev, jnp.max(a_mask, axis=1, keepdims=True))
        d_mat = jnp.exp(a_mask - gcol)
        inter = jnp.exp(m_prev - gcol)
        g_last = jnp.max(gcol, axis=0, keepdims=True)

        qb = q.astype(BF16)
        s = lax.dot_general(qb, k.astype(BF16), _NT, preferred_element_type=F32) * d_mat
        c_old = c_scr[hh]
        n_old = n_scr[hh]
        num = (inter * jnp.dot(qb, c_old.astype(BF16), preferred_element_type=F32)
               + jnp.dot(s.astype(BF16), v, preferred_element_type=F32))
        den = inter * jnp.sum(q * n_old, axis=1, keepdims=True) + jnp.sum(s, axis=1, keepdims=True)
        h = num / jnp.maximum(jnp.abs(den), jnp.exp(-(b_col + gcol)))

        decay = jnp.exp(m_prev - g_last)
        kw = k * jnp.exp(a_col - g_last)
        c_scr[hh] = decay * c_old + lax.dot_general(kw.astype(BF16), v, _TN,
                                                    preferred_element_type=F32)
        n_scr[hh] = decay * n_old + jnp.sum(kw, axis=0, keepdims=True)
        m_scr[hh] = jnp.broadcast_to(b_col[L - 1:L, :] + g_last, m_scr.shape[1:])

        hg = _sigmoid(om_ref[:, sl].astype(F32)) * h
        hn = hg * lax.rsqrt(jnp.mean(hg * hg, axis=-1, keepdims=True) + EPS) * mlw_ref[:, sl]
        z = zm_ref[:, sl].astype(F32)
        o_ref[:, sl] = (hn * (z * _sigmoid(z))).astype(BF16)


def _mlstm_call(p, g, gt, conv_w, conv_b, ml_norm_w, batch, seq):
    T = p.shape[0]
    L = ML_CHUNK
    nc = seq // L
    W = ML_HEADS * ML_HEAD_DIM
    row = lambda b, c: b * nc + c
    pspec = lambda col: pl.BlockSpec((L, W), lambda b, c: (row(b, c), col))
    return pl.pallas_call(
        _mlstm_kernel,
        grid=(batch, nc),
        in_specs=[pspec(COL_QM), pspec(COL_KM), pspec(COL_VM), pspec(COL_ZM), pspec(COL_OM),
                  pl.BlockSpec((L, LANES), lambda b, c: (row(b, c), 0)),
                  pl.BlockSpec((GATE_COLS, L), lambda b, c: (0, row(b, c))),
                  pl.BlockSpec((CONV_WIDTH, 2 * W), lambda b, c: (0, 0)),
                  pl.BlockSpec((1, 2 * W), lambda b, c: (0, 0)),
                  pl.BlockSpec((1, W), lambda b, c: (0, 0))],
        out_specs=pl.BlockSpec((L, W), lambda b, c: (row(b, c), 0)),
        out_shape=jax.ShapeDtypeStruct((T, W), BF16),
        scratch_shapes=[pltpu.VMEM((ML_HEADS, ML_HEAD_DIM, ML_HEAD_DIM), F32),
                        pltpu.VMEM((ML_HEADS, 1, ML_HEAD_DIM), F32),
                        pltpu.VMEM((ML_HEADS, SUBLANES, LANES), F32),
                        pltpu.VMEM((SUBLANES, W), F32),
                        pltpu.VMEM((SUBLANES, W), F32),
                        pltpu.VMEM((CONV_WIDTH, L, L), BF16)],
        compiler_params=pltpu.CompilerParams(dimension_semantics=("arbitrary", "arbitrary"),
                                             vmem_limit_bytes=VMEM_LIMIT),
        name="mlstm",
    )(p, p, p, p, p, g, gt, conv_w, conv_b, ml_norm_w)


def _outp_kernel(*refs):
    ua_refs, (um_ref, ga_ref, gm_ref, x_ref, ada_ref, wa_ref, wm_ref, wo_ref, o_ref) = refs[:-9], refs[-9:]
    ua = jnp.concatenate([r[...] for r in ua_refs], axis=0)
    ya = jnp.dot(ua, wa_ref[...], preferred_element_type=F32)
    ym = jnp.dot(um_ref[...], wm_ref[...], preferred_element_type=F32)
    y = _sigmoid(ga_ref[...].astype(F32)) * ya + _sigmoid(gm_ref[...].astype(F32)) * ym
    gate = ada_ref[0][2:3, :]
    o_ref[...] = x_ref[...] + gate * jnp.dot(y.astype(BF16), wo_ref[...], preferred_element_type=F32)


def _outp_call(ua, um, p, x2, ada3, wa, wm, wo, seq):
    T, D = x2.shape
    tm = OUT_TM
    steps_per_seq = seq // tm
    blk = MOBA_BLOCK
    nq = seq // blk
    tiles_per_step = tm // blk

    def ua_spec(e):
        def index(i):
            t = (i % steps_per_seq) * tiles_per_step + e
            return ((i // steps_per_seq) * nq + _ua_tile_row(t, nq), 0)
        return pl.BlockSpec((blk, D), index)

    act = lambda col: pl.BlockSpec((tm, D), lambda i: (i, col))
    wspec = pl.BlockSpec((D, D), lambda i: (0, 0))
    return pl.pallas_call(
        _outp_kernel,
        grid=(T // tm,),
        in_specs=[ua_spec(e) for e in range(tiles_per_step)]
                 + [act(0), act(COL_GA), act(COL_GM), act(0),
                    pl.BlockSpec((1, 3, D), lambda i: (i // steps_per_seq, 0, 0)),
                    wspec, wspec, wspec],
        out_specs=pl.BlockSpec((tm, D), lambda i: (i, 0)),
        out_shape=jax.ShapeDtypeStruct((T, D), F32),
        compiler_params=pltpu.CompilerParams(dimension_semantics=("arbitrary",),
                                             vmem_limit_bytes=VMEM_LIMIT),
        name="outp",
    )(*([ua] * tiles_per_step), um, p, p, x2, ada3, wa, wm, wo)


def _rel_bias_tables(rel_bias):
    max_exact = REL_BUCKETS // 2
    rb = rel_bias.astype(F32)
    rb = (rb - rb[REL_BUCKETS - 1:, :]) * LOG2E
    key = jnp.arange(MOBA_BLOCK)[:, None]
    qry = jnp.arange(MOBA_BLOCK)[None, :]
    d_own = qry - key

    def tile(dist):
        nf = jnp.maximum(dist, 1).astype(F32)
        large = max_exact + (jnp.log(nf / max_exact) / math.log(REL_MAX_DIST / max_exact)
                             * (REL_BUCKETS - max_exact)).astype(jnp.int32)
        bucket = jnp.where(dist < max_exact, dist, jnp.minimum(large, REL_BUCKETS - 1))
        onehot = (bucket[..., None] == jnp.arange(REL_BUCKETS)).astype(F32)
        return jnp.einsum('kqb,bh->hkq', onehot, rb, precision=lax.Precision.HIGHEST)

    own = jnp.where(d_own >= 0, tile(jnp.maximum(d_own, 0)), NEG_INF)
    prev = tile(d_own + MOBA_BLOCK)
    return own, prev


def _layer(x2, c, w_ada, b_ada, norm_w, w_in_all, layer, q_norm_w, k_norm_w, rel_bias, conv_w, conv_b,
           b_igate, b_fgate, ml_norm_w, w_att_proj, w_ml_proj, w_out, batch, seq):
    D = D_MODEL
    g0 = 4 * D + 5 * D
    ada3 = _ada_call(c, w_ada, b_ada).reshape(batch, 3, D)

    wa = w_in_all[layer, :, :g0].astype(BF16)
    wb = w_in_all[layer, :, g0 + GATE_COLS:].astype(BF16)
    wg = jnp.pad(w_in_all[layer, :, g0:g0 + GATE_COLS], ((0, 0), (0, LANES - GATE_COLS))).astype(BF16)
    gb = jnp.pad(jnp.concatenate([b_igate, b_fgate]), (0, LANES - GATE_COLS)).reshape(1, LANES)
    q_scale = ATT_HEAD_DIM ** -0.5 * LOG2E
    hnw = jnp.stack([jnp.tile(q_norm_w.astype(F32), ATT_HEADS) * q_scale,
                     jnp.tile(k_norm_w.astype(F32), ATT_HEADS)])

    p, g, gt = _inproj_call(x2, ada3, norm_w.reshape(1, D), wa, wb, wg, gb, hnw, seq)
    bias_own, bias_prev = _rel_bias_tables(rel_bias)
    ua = _moba_call(p, bias_own, bias_prev, batch, seq).reshape(batch * seq, D)
    um = _mlstm_call(p, g, gt, conv_w, conv_b.reshape(1, -1), ml_norm_w.reshape(1, -1), batch, seq)
    return _outp_call(ua, um, p, x2, ada3, w_att_proj.astype(BF16), w_ml_proj.astype(BF16),
                      w_out.astype(BF16), seq)


def kernel(x, c, w_ada, b_ada, norm_w, w_in, q_norm_w, k_norm_w, rel_bias, conv_w, conv_b, b_igate,
           b_fgate, ml_norm_w, w_att_proj, w_ml_proj, w_out):
    batch, seq, D = x.shape
    assert D == D_MODEL and seq % PROJ_TM == 0 and seq % MOBA_BLOCK == 0 and seq % ML_CHUNK == 0
    x2 = x.reshape(batch * seq, D)
    for l in range(w_in.shape[0]):
        x2 = _layer(x2, c, w_ada[l], b_ada[l], norm_w[l], w_in, l, q_norm_w[l], k_norm_w[l],
                    rel_bias, conv_w[l], conv_b[l], b_igate[l], b_fgate[l], ml_norm_w[l],
                    w_att_proj[l], w_ml_proj[l], w_out[l], batch, seq)
    return x2.reshape(batch, seq, D)
```

```python
import math

import jax
import jax.numpy as jnp
from jax import lax
from jax.experimental import pallas as pl
from jax.experimental.pallas import tpu as pltpu

F32 = jnp.float32
BF16 = jnp.bfloat16

D_MODEL = 1024
ATT_HEADS = 8
ATT_HEAD_DIM = D_MODEL // ATT_HEADS
MOBA_BLOCK = 256
MOBA_TOPK = 3
REL_BUCKETS = 32
REL_MAX_DIST = 128
ML_HEADS = 4
ML_HEAD_DIM = D_MODEL // ML_HEADS
CONV_WIDTH = 4
EPS = 1e-6
NEG_INF = -1e30
LOG2E = math.log2(math.e)

LANES = 128
SUBLANES = 8
BF16_SUBLANES = 16
GATE_COLS = 2 * ML_HEADS
ML_CHUNK = 256
PROJ_TM = 1024
PROJ_TN = 1024
OUT_TM = 512
MOBA_HEAD_GROUP = 4
KMEAN_TERMS = 3
VMEM_LIMIT = 52 * 1024 * 1024

COL_QA, COL_KA, COL_VA, COL_ZA, COL_QM, COL_KM, COL_VM, COL_ZM, COL_OM, COL_GA, COL_GM = range(11)
N_COL_TILES = 11
N_HEAD_TILES = 9

_NT = (((1,), (1,)), ((), ()))
_TN = (((0,), (0,)), ((), ()))


def _sigmoid(x):
    return 1.0 / (1.0 + jnp.exp2(x * (-LOG2E)))


def _log_sigmoid(x):
    return jnp.minimum(x, 0.0) - jnp.log1p(jnp.exp(-jnp.abs(x)))


def _ada_kernel(c_ref, w_ref, b_ref, o_ref):
    o_ref[...] = jnp.dot(c_ref[...], w_ref[...], precision=lax.Precision.HIGHEST,
                         preferred_element_type=F32) + b_ref[...]


def _ada_call(c, w_ada, b_ada):
    B, D = c.shape
    return pl.pallas_call(
        _ada_kernel,
        grid=(3,),
        in_specs=[pl.BlockSpec((B, D), lambda j: (0, 0)),
                  pl.BlockSpec((D, D), lambda j: (0, j)),
                  pl.BlockSpec((1, D), lambda j: (0, j))],
        out_specs=pl.BlockSpec((B, D), lambda j: (0, j)),
        out_shape=jax.ShapeDtypeStruct((B, 3 * D), F32),
        compiler_params=pltpu.CompilerParams(dimension_semantics=("arbitrary",),
                                             vmem_limit_bytes=VMEM_LIMIT),
        name="ada",
    )(c, w_ada, b_ada.reshape(1, 3 * D))


def _inproj_kernel(x_ref, ada_ref, nw_ref, wa_ref, wb_ref, wg_ref, gb_ref, hnw_ref,
                   p_ref, g_ref, gt_ref, h_scr):
    j = pl.program_id(1)
    tn = p_ref.shape[1]

    @pl.when(j == 0)
    def _():
        x = x_ref[...]
        ada = ada_ref[0]
        y = x * lax.rsqrt(jnp.mean(x * x, axis=-1, keepdims=True) + EPS) * nw_ref[...]
        h = (y * (1.0 + ada[1:2, :]) + ada[0:1, :]).astype(BF16)
        h_scr[...] = h
        g = jnp.dot(h, wg_ref[...], preferred_element_type=F32) + gb_ref[...]
        g_ref[...] = g
        gt_ref[...] = g.T[:GATE_COLS, :]

    @pl.when(j <= COL_KA)
    def _():
        h = h_scr[...]
        w = hnw_ref[pl.ds(j, 1), :]
        pair = 2 * ATT_HEAD_DIM
        for c0 in range(0, tn, pair):
            acc = jnp.dot(h, wa_ref[:, c0:c0 + pair], preferred_element_type=F32)
            for c in range(c0, c0 + pair, ATT_HEAD_DIM):
                a = acc[:, c - c0:c - c0 + ATT_HEAD_DIM]
                r = lax.rsqrt(jnp.mean(a * a, axis=-1, keepdims=True) + EPS)
                p_ref[:, c:c + ATT_HEAD_DIM] = (a * r * w[:, c:c + ATT_HEAD_DIM]).astype(BF16)

    @pl.when(jnp.logical_and(j > COL_KA, j < N_HEAD_TILES))
    def _():
        p_ref[...] = jnp.dot(h_scr[...], wa_ref[...], preferred_element_type=F32).astype(BF16)

    @pl.when(j >= N_HEAD_TILES)
    def _():
        p_ref[...] = jnp.dot(h_scr[...], wb_ref[...], preferred_element_type=F32).astype(BF16)


def _inproj_call(x2, ada3, norm_w, wa, wb, wg, gb, hnw, seq):
    T, D = x2.shape
    tm, tn = PROJ_TM, PROJ_TN
    tiles_per_seq = seq // tm
    return pl.pallas_call(
        _inproj_kernel,
        grid=(T // tm, N_COL_TILES),
        in_specs=[pl.BlockSpec((tm, D), lambda i, j: (i, 0)),
                  pl.BlockSpec((1, 3, D), lambda i, j: (i // tiles_per_seq, 0, 0)),
                  pl.BlockSpec((1, D), lambda i, j: (0, 0)),
                  pl.BlockSpec((D, tn), lambda i, j: (0, jnp.minimum(j, N_HEAD_TILES - 1))),
                  pl.BlockSpec((D, tn), lambda i, j: (0, jnp.maximum(j - N_HEAD_TILES, 0))),
                  pl.BlockSpec((D, LANES), lambda i, j: (0, 0)),
                  pl.BlockSpec((1, LANES), lambda i, j: (0, 0)),
                  pl.BlockSpec((2, tn), lambda i, j: (0, 0))],
        out_specs=[pl.BlockSpec((tm, tn), lambda i, j: (i, j)),
                   pl.BlockSpec((tm, LANES), lambda i, j: (i, 0)),
                   pl.BlockSpec((GATE_COLS, tm), lambda i, j: (0, i))],
        out_shape=[jax.ShapeDtypeStruct((T, N_COL_TILES * tn), BF16),
                   jax.ShapeDtypeStruct((T, LANES), F32),
                   jax.ShapeDtypeStruct((GATE_COLS, T), F32)],
        scratch_shapes=[pltpu.VMEM((tm, D), BF16)],
        compiler_params=pltpu.CompilerParams(dimension_semantics=("arbitrary", "arbitrary"),
                                             vmem_limit_bytes=VMEM_LIMIT),
        name="inproj",
    )(x2, ada3, norm_w, wa, wb, wg, gb, hnw)


def _moba_kernel(qa_ref, qb_ref, k_ref, v_ref, zaa_ref, zab_ref, bown_ref, bprev_ref, o_ref,
                 kmean_f32_scr, kmean_scr, vt_scr, q2_scr, selb_scr, s_scr, m8_scr):
    step = pl.program_id(2)
    blk, d = MOBA_BLOCK, ATT_HEAD_DIM
    da = d + BF16_SUBLANES
    nb = k_ref.shape[0] // blk
    heads = qa_ref.shape[1] // d
    tile = (step, nb - 1 - step)
    n_far = (jnp.maximum(step - 1, 0), nb - 2 - step)
    near_slots = 4
    far_slots = nb - near_slots + 1

    def hsl(hd):
        return slice(hd * d, (hd + 1) * d)

    @pl.when(step == 0)
    def _():
        for n in range(nb):
            kb = k_ref[n * blk:(n + 1) * blk, :].astype(F32)
            kmean_f32_scr[n:n + 1, :] = jnp.mean(kb, axis=0, keepdims=True)
            for hd in range(heads):
                vt_scr[n, hd * da:hd * da + d, :] = v_ref[n * blk:(n + 1) * blk, hsl(hd)].T
                vt_scr[n, hd * da + d:(hd + 1) * da, :] = jnp.ones((da - d, blk), BF16)

        rest = kmean_f32_scr[...]
        for c in range(KMEAN_TERMS):
            term = rest.astype(BF16)
            kmean_scr[c * nb:(c + 1) * nb, :] = term
            rest = rest - term.astype(F32)

    q2_scr[0] = qa_ref[...]
    q2_scr[1] = qb_ref[...]

    def fold(x):
        return x.reshape(blk // SUBLANES, SUBLANES, blk)

    def col_max8(s):
        x = fold(s)
        n = x.shape[0]
        while n > 1:
            n //= 2
            x = jnp.maximum(x[:n], x[n:2 * n])
        return x[0]

    rows = lax.broadcasted_iota(jnp.int32, (nb, blk), 0)

    def select_blocks(hd):
        for x in range(2):
            valid = rows < tile[x]
            terms = lax.dot_general(kmean_scr[:, hsl(hd)], q2_scr[x, :, hsl(hd)], _NT,
                                    preferred_element_type=F32)
            gate = terms[:nb]
            for c in range(1, KMEAN_TERMS):
                gate = gate + terms[c * nb:(c + 1) * nb]
            g = jnp.where(valid, gate, -jnp.inf)
            selb = jnp.full(gate.shape, NEG_INF, F32)
            for _ in range(MOBA_TOPK):
                mx = jnp.max(g, axis=0, keepdims=True)
                first = jnp.min(jnp.where(g == mx, rows, nb), axis=0, keepdims=True)
                pick = rows == first
                selb = jnp.where(pick, 0.0, selb)
                g = jnp.where(pick, -jnp.inf, g)
            selb_scr[x, hd] = jnp.where(valid, selb, NEG_INF)

    def near_items(hd):
        first = step == 0
        jb = tile[1] - 1
        ja = jnp.where(first, far_slots, step - 1)
        add_a = jnp.where(first, selb_scr[1, hd, pl.ds(far_slots, 1), :],
                          bprev_ref[hd] + selb_scr[0, hd, pl.ds(jnp.maximum(step - 1, 0), 1), :])
        return [(0, 0, tile[0], bown_ref[hd], None),
                (1, 1, tile[1], bown_ref[hd], None),
                (2, 1, jb, bprev_ref[hd] + selb_scr[1, hd, pl.ds(jb, 1), :], None),
                (3, jnp.where(first, 1, 0), ja, add_a, jnp.logical_not(first))]

    shared_slots = nb // 2 - 2

    def far_items(hd):
        out = []
        for t in range(far_slots):
            if t < shared_slots:
                is_a = t < n_far[0]
                x = jnp.where(is_a, 0, 1)
                j = jnp.where(is_a, t, t - n_far[0])
            else:
                is_a, x, j = None, 1, t - n_far[0]
            out.append((near_slots + t, x, j, selb_scr[x, hd, pl.ds(j, 1), :], is_a))
        return out

    def pass1(hd, items):
        m8 = [None, None]
        for slot, x, j, add, is_a in items:
            kj = k_ref[pl.ds(pl.multiple_of(j * blk, blk), blk), hsl(hd)]
            s = lax.dot_general(kj, q2_scr[x, :, hsl(hd)], _NT, preferred_element_type=F32) + add
            s_scr[hd, slot] = s
            mt = col_max8(s)
            for y in range(2):
                if is_a is None:
                    if y != x:
                        continue
                    part = mt
                else:
                    part = jnp.where(is_a, mt, -jnp.inf) if y == 0 else jnp.where(is_a, -jnp.inf, mt)
                m8[y] = part if m8[y] is None else jnp.maximum(m8[y], part)
        for x in range(2):
            m8_scr[x, hd] = jnp.broadcast_to(jnp.max(m8[x], axis=0, keepdims=True), (SUBLANES, blk))

    def pass2(hd, items):
        acc = [None, None]
        acc_shared = None
        for slot, x, j, _, is_a in items:
            p = jnp.exp2(fold(s_scr[hd, slot]) - m8_scr[x, hd])
            pb = p.reshape(blk, blk).astype(BF16)
            vt = vt_scr[j, hd * da:(hd + 1) * da, :]
            if is_a is None:
                pv = jnp.dot(vt, pb, preferred_element_type=F32)
                acc[x] = pv if acc[x] is None else acc[x] + pv
            else:
                zero = jnp.zeros_like(vt)
                lhs = jnp.concatenate([jnp.where(is_a, vt, zero), jnp.where(is_a, zero, vt)], axis=0)
                pv = jnp.dot(lhs, pb, preferred_element_type=F32)
                acc_shared = pv if acc_shared is None else acc_shared + pv
        for x, za_ref in enumerate((zaa_ref, zab_ref)):
            tot = acc[x] + acc_shared[x * da:(x + 1) * da]
            y = (tot[:d] * (1.0 / tot[d:d + 1])).T
            za = za_ref[:, hsl(hd)].astype(F32)
            o_ref[0, x, 0, :, hsl(hd)] = (y * (za * _sigmoid(za))).astype(BF16)

    items = []
    for hd in range(heads):
        select_blocks(hd)
        items.append(near_items(hd) + far_items(hd))
        pass1(hd, items[hd])
        if hd:
            pass2(hd - 1, items[hd - 1])
    pass2(heads - 1, items[heads - 1])


def _moba_call(p, bias_own, bias_prev, batch, seq):
    blk, d = MOBA_BLOCK, ATT_HEAD_DIM
    nq = seq // blk
    hg = MOBA_HEAD_GROUP
    w = hg * d
    gpt = PROJ_TN // w
    first = lambda col: (lambda b, h, i: (b * nq + i, col * gpt + h))
    last = lambda col: (lambda b, h, i: (b * nq + nq - 1 - i, col * gpt + h))
    return pl.pallas_call(
        _moba_kernel,
        grid=(batch, ATT_HEADS // hg, nq // 2),
        in_specs=[pl.BlockSpec((blk, w), first(COL_QA)),
                  pl.BlockSpec((blk, w), last(COL_QA)),
                  pl.BlockSpec((seq, w), lambda b, h, i: (b, COL_KA * gpt + h)),
                  pl.BlockSpec((seq, w), lambda b, h, i: (b, COL_VA * gpt + h)),
                  pl.BlockSpec((blk, w), first(COL_ZA)),
                  pl.BlockSpec((blk, w), last(COL_ZA)),
                  pl.BlockSpec((hg, blk, blk), lambda b, h, i: (h, 0, 0)),
                  pl.BlockSpec((hg, blk, blk), lambda b, h, i: (h, 0, 0))],
        out_specs=pl.BlockSpec((1, 2, 1, blk, w), lambda b, h, i: (b, 0, i, 0, h)),
        out_shape=jax.ShapeDtypeStruct((batch, 2, nq // 2, blk, ATT_HEADS * d), BF16),
        scratch_shapes=[pltpu.VMEM((nq, w), F32),
                        pltpu.VMEM((KMEAN_TERMS * nq, w), BF16),
                        pltpu.VMEM((nq, hg * (d + BF16_SUBLANES), blk), BF16),
                        pltpu.VMEM((2, blk, w), BF16),
                        pltpu.VMEM((2, hg, nq, blk), F32),
                        pltpu.VMEM((hg, nq + 1, blk, blk), F32),
                        pltpu.VMEM((2, hg, SUBLANES, blk), F32)],
        compiler_params=pltpu.CompilerParams(
            dimension_semantics=("arbitrary", "arbitrary", "arbitrary"),
            vmem_limit_bytes=VMEM_LIMIT),
        name="moba",
    )(p, p, p, p, p, p, bias_own, bias_prev)


def _ua_tile_row(t, nq):
    half = nq // 2
    return jnp.where(t < half, t, half + (nq - 1 - t))


def _mlstm_kernel(qm_ref, km_ref, vm_ref, zm_ref, om_ref, g_ref, gt_ref, cw_ref, cb_ref, mlw_ref,
                  o_ref, c_scr, n_scr, m_scr, tailq_scr, tailk_scr, shift_scr):
    c_idx = pl.program_id(1)
    L = qm_ref.shape[0]
    W = ML_HEADS * ML_HEAD_DIM
    pad = SUBLANES

    r_i = lax.broadcasted_iota(jnp.int32, (L, L), 0)
    c_i = lax.broadcasted_iota(jnp.int32, (L, L), 1)

    @pl.when(c_idx == 0)
    def _():
        c_scr[...] = jnp.zeros_like(c_scr)
        n_scr[...] = jnp.zeros_like(n_scr)
        m_scr[...] = jnp.zeros_like(m_scr)
        tailq_scr[...] = jnp.zeros_like(tailq_scr)
        tailk_scr[...] = jnp.zeros_like(tailk_scr)
        for k in range(CONV_WIDTH):
            shift_scr[k] = (c_i == r_i - k).astype(F32).astype(BF16)

    row8 = lax.broadcasted_iota(jnp.int32, (pad, W), 0)

    def conv_silu(src_ref, tail_scr, col0):
        x = src_ref[...]
        tail = tail_scr[...]
        y = cb_ref[:, col0:col0 + W]
        fix = jnp.zeros((pad, W), F32)
        for w in range(CONV_WIDTH):
            k = CONV_WIDTH - 1 - w
            cw = cw_ref[w:w + 1, col0:col0 + W]
            y = y + cw * jnp.dot(shift_scr[k], x, preferred_element_type=F32)
            if k:
                fix = fix + cw * jnp.where(row8 < k, pltpu.roll(tail, k, axis=0), 0.0)
        tail_scr[...] = x[L - 2 * pad:, :].astype(F32)[pad:, :]
        y = jnp.concatenate([y[:pad] + fix, y[pad:]], axis=0)
        return y * _sigmoid(y)

    qc = conv_silu(qm_ref, tailq_scr, 0)
    kc = conv_silu(km_ref, tailk_scr, W) * (ML_HEAD_DIM ** -0.5)

    causal = c_i <= r_i
    ltri = causal.astype(F32)
    utri = (r_i <= c_i).astype(F32)
    g_col = g_ref[...]
    g_row = gt_ref[...]
    b_col_all = jnp.dot(ltri, _log_sigmoid(g_col), precision=lax.Precision.HIGHEST,
                        preferred_element_type=F32)
    b_row_all = jnp.dot(_log_sigmoid(g_row), utri, precision=lax.Precision.HIGHEST,
                        preferred_element_type=F32)

    for hh in range(ML_HEADS):
        sl = slice(hh * ML_HEAD_DIM, (hh + 1) * ML_HEAD_DIM)
        q = qc[:, sl]
        k = kc[:, sl]
        v = vm_ref[:, sl]
        b_col = b_col_all[:, ML_HEADS + hh:ML_HEADS + hh + 1]
        a_col = g_col[:, hh:hh + 1] - b_col
        a_row = g_row[hh:hh + 1, :] - b_row_all[ML_HEADS + hh:ML_HEADS + hh + 1, :]
        m_prev = m_scr[hh][0:1, 0:1]

        a_mask = jnp.where(causal, a_row, -jnp.inf)
        gcol = jnp.maximum(m_prev, jnp.max(a_mask, axis=1, keepdims=True))
        d_mat = jnp.exp(a_mask - gcol)
        inter = jnp.exp(m_prev - gcol)
        g_last = jnp.max(gcol, axis=0, keepdims=True)

        qb = q.astype(BF16)
        s = lax.dot_general(qb, k.astype(BF16), _NT, preferred_element_type=F32) * d_mat
        c_old = c_scr[hh]
        n_old = n_scr[hh]
        num = (inter * jnp.dot(qb, c_old.astype(BF16), preferred_element_type=F32)
               + jnp.dot(s.astype(BF16), v, preferred_element_type=F32))
        den = inter * jnp.sum(q * n_old, axis=1, keepdims=True) + jnp.sum(s, axis=1, keepdims=True)
        h = num / jnp.maximum(jnp.abs(den), jnp.exp(-(b_col + gcol)))

        decay = jnp.exp(m_prev - g_last)
        kw = k * jnp.exp(a_col - g_last)
        c_scr[hh] = decay * c_old + lax.dot_general(kw.astype(BF16), v, _TN,
                                                    preferred_element_type=F32)
        n_scr[hh] = decay * n_old + jnp.sum(kw, axis=0, keepdims=True)
        m_scr[hh] = jnp.broadcast_to(b_col[L - 1:L, :] + g_last, m_scr.shape[1:])

        hg = _sigmoid(om_ref[:, sl].astype(F32)) * h
        hn = hg * lax.rsqrt(jnp.mean(hg * hg, axis=-1, keepdims=True) + EPS) * mlw_ref[:, sl]
        z = zm_ref[:, sl].astype(F32)
        o_ref[:, sl] = (hn * (z * _sigmoid(z))).astype(BF16)


def _mlstm_call(p, g, gt, conv_w, conv_b, ml_norm_w, batch, seq):
    T = p.shape[0]
    L = ML_CHUNK
    nc = seq // L
    W = ML_HEADS * ML_HEAD_DIM
    row = lambda b, c: b * nc + c
    pspec = lambda col: pl.BlockSpec((L, W), lambda b, c: (row(b, c), col))
    return pl.pallas_call(
        _mlstm_kernel,
        grid=(batch, nc),
        in_specs=[pspec(COL_QM), pspec(COL_KM), pspec(COL_VM), pspec(COL_ZM), pspec(COL_OM),
                  pl.BlockSpec((L, LANES), lambda b, c: (row(b, c), 0)),
                  pl.BlockSpec((GATE_COLS, L), lambda b, c: (0, row(b, c))),
                  pl.BlockSpec((CONV_WIDTH, 2 * W), lambda b, c: (0, 0)),
                  pl.BlockSpec((1, 2 * W), lambda b, c: (0, 0)),
                  pl.BlockSpec((1, W), lambda b, c: (0, 0))],
        out_specs=pl.BlockSpec((L, W), lambda b, c: (row(b, c), 0)),
        out_shape=jax.ShapeDtypeStruct((T, W), BF16),
        scratch_shapes=[pltpu.VMEM((ML_HEADS, ML_HEAD_DIM, ML_HEAD_DIM), F32),
                        pltpu.VMEM((ML_HEADS, 1, ML_HEAD_DIM), F32),
                        pltpu.VMEM((ML_HEADS, SUBLANES, LANES), F32),
                        pltpu.VMEM((SUBLANES, W), F32),
                        pltpu.VMEM((SUBLANES, W), F32),
                        pltpu.VMEM((CONV_WIDTH, L, L), BF16)],
        compiler_params=pltpu.CompilerParams(dimension_semantics=("arbitrary", "arbitrary"),
                                             vmem_limit_bytes=VMEM_LIMIT),
        name="mlstm",
    )(p, p, p, p, p, g, gt, conv_w, conv_b, ml_norm_w)


def _outp_kernel(*refs):
    ua_refs, (um_ref, ga_ref, gm_ref, x_ref, ada_ref, wa_ref, wm_ref, wo_ref, o_ref) = refs[:-9], refs[-9:]
    ua = jnp.concatenate([r[...] for r in ua_refs], axis=0)
    ya = jnp.dot(ua, wa_ref[...], preferred_element_type=F32)
    ym = jnp.dot(um_ref[...], wm_ref[...], preferred_element_type=F32)
    y = _sigmoid(ga_ref[...].astype(F32)) * ya + _sigmoid(gm_ref[...].astype(F32)) * ym
    gate = ada_ref[0][2:3, :]
    o_ref[...] = x_ref[...] + gate * jnp.dot(y.astype(BF16), wo_ref[...], preferred_element_type=F32)


def _outp_call(ua, um, p, x2, ada3, wa, wm, wo, seq):
    T, D = x2.shape
    tm = OUT_TM
    steps_per_seq = seq // tm
    blk = MOBA_BLOCK
    nq = seq // blk
    tiles_per_step = tm // blk

    def ua_spec(e):
        def index(i):
            t = (i % steps_per_seq) * tiles_per_step + e
            return ((i // steps_per_seq) * nq + _ua_tile_row(t, nq), 0)
        return pl.BlockSpec((blk, D), index)

    act = lambda col: pl.BlockSpec((tm, D), lambda i: (i, col))
    wspec = pl.BlockSpec((D, D), lambda i: (0, 0))
    return pl.pallas_call(
        _outp_kernel,
        grid=(T // tm,),
        in_specs=[ua_spec(e) for e in range(tiles_per_step)]
                 + [act(0), act(COL_GA), act(COL_GM), act(0),
                    pl.BlockSpec((1, 3, D), lambda i: (i // steps_per_seq, 0, 0)),
                    wspec, wspec, wspec],
        out_specs=pl.BlockSpec((tm, D), lambda i: (i, 0)),
        out_shape=jax.ShapeDtypeStruct((T, D), F32),
        compiler_params=pltpu.CompilerParams(dimension_semantics=("arbitrary",),
                                             vmem_limit_bytes=VMEM_LIMIT),
        name="outp",
    )(*([ua] * tiles_per_step), um, p, p, x2, ada3, wa, wm, wo)


def _rel_bias_tables(rel_bias):
    max_exact = REL_BUCKETS // 2
    rb = rel_bias.astype(F32)
    rb = (rb - rb[REL_BUCKETS - 1:, :]) * LOG2E
    key = jnp.arange(MOBA_BLOCK)[:, None]
    qry = jnp.arange(MOBA_BLOCK)[None, :]
    d_own = qry - key

    def tile(dist):
        nf = jnp.maximum(dist, 1).astype(F32)
        large = max_exact + (jnp.log(nf / max_exact) / math.log(REL_MAX_DIST / max_exact)
                             * (REL_BUCKETS - max_exact)).astype(jnp.int32)
        bucket = jnp.where(dist < max_exact, dist, jnp.minimum(large, REL_BUCKETS - 1))
        onehot = (bucket[..., None] == jnp.arange(REL_BUCKETS)).astype(F32)
        return jnp.einsum('kqb,bh->hkq', onehot, rb, precision=lax.Precision.HIGHEST)

    own = jnp.where(d_own >= 0, tile(jnp.maximum(d_own, 0)), NEG_INF)
    prev = tile(d_own + MOBA_BLOCK)
    return own, prev


def _layer(x2, c, w_ada, b_ada, norm_w, w_in_all, layer, q_norm_w, k_norm_w, rel_bias, conv_w, conv_b,
           b_igate, b_fgate, ml_norm_w, w_att_proj, w_ml_proj, w_out, batch, seq):
    D = D_MODEL
    g0 = 4 * D + 5 * D
    ada3 = _ada_call(c, w_ada, b_ada).reshape(batch, 3, D)

    wa = w_in_all[layer, :, :g0].astype(BF16)
    wb = w_in_all[layer, :, g0 + GATE_COLS:].astype(BF16)
    wg = jnp.pad(w_in_all[layer, :, g0:g0 + GATE_COLS], ((0, 0), (0, LANES - GATE_COLS))).astype(BF16)
    gb = jnp.pad(jnp.concatenate([b_igate, b_fgate]), (0, LANES - GATE_COLS)).reshape(1, LANES)
    q_scale = ATT_HEAD_DIM ** -0.5 * LOG2E
    hnw = jnp.stack([jnp.tile(q_norm_w.astype(F32), ATT_HEADS) * q_scale,
                     jnp.tile(k_norm_w.astype(F32), ATT_HEADS)])

    p, g, gt = _inproj_call(x2, ada3, norm_w.reshape(1, D), wa, wb, wg, gb, hnw, seq)
    bias_own, bias_prev = _rel_bias_tables(rel_bias)
    ua = _moba_call(p, bias_own, bias_prev, batch, seq).reshape(batch * seq, D)
    um = _mlstm_call(p, g, gt, conv_w, conv_b.reshape(1, -1), ml_norm_w.reshape(1, -1), batch, seq)
    return _outp_call(ua, um, p, x2, ada3, w_att_proj.astype(BF16), w_ml_proj.astype(BF16),
                      w_out.astype(BF16), seq)


def kernel(x, c, w_ada, b_ada, norm_w, w_in, q_norm_w, k_norm_w, rel_bias, conv_w, conv_b, b_igate,
           b_fgate, ml_norm_w, w_att_proj, w_ml_proj, w_out):
    batch, seq, D = x.shape
    assert D == D_MODEL and seq % PROJ_TM == 0 and seq % MOBA_BLOCK == 0 and seq % ML_CHUNK == 0
    x2 = x.reshape(batch * seq, D)
    for l in range(w_in.shape[0]):
        x2 = _layer(x2, c, w_ada[l], b_ada[l], norm_w[l], w_in, l, q_norm_w[l], k_norm_w[l],
                    rel_bias, conv_w[l], conv_b[l], b_igate[l], b_fgate[l], ml_norm_w[l],
                    w_att_proj[l], w_ml_proj[l], w_out[l], batch, seq)
    return x2.reshape(batch, seq, D)
```

```python
import math

import jax
import jax.numpy as jnp
from jax import lax
from jax.experimental import pallas as pl
from jax.experimental.pallas import tpu as pltpu

F32 = jnp.float32
BF16 = jnp.bfloat16

D_MODEL = 1024
ATT_HEADS = 8
ATT_HEAD_DIM = D_MODEL // ATT_HEADS
MOBA_BLOCK = 256
MOBA_TOPK = 3
REL_BUCKETS = 32
REL_MAX_DIST = 128
ML_HEADS = 4
ML_HEAD_DIM = D_MODEL // ML_HEADS
CONV_WIDTH = 4
EPS = 1e-6
NEG_INF = -1e30
LOG2E = math.log2(math.e)

LANES = 128
SUBLANES = 8
BF16_SUBLANES = 16
GATE_COLS = 2 * ML_HEADS
ML_CHUNK = 256
PROJ_TM = 1024
PROJ_TN = 1024
OUT_TM = 512
MOBA_HEAD_GROUP = 4
KMEAN_TERMS = 3
VMEM_LIMIT = 52 * 1024 * 1024

COL_QA, COL_KA, COL_VA, COL_ZA, COL_QM, COL_KM, COL_VM, COL_ZM, COL_OM, COL_GA, COL_GM = range(11)
N_COL_TILES = 11
N_HEAD_TILES = 9

_NT = (((1,), (1,)), ((), ()))
_TN = (((0,), (0,)), ((), ()))


def _sigmoid(x):
    return 1.0 / (1.0 + jnp.exp2(x * (-LOG2E)))


def _log_sigmoid(x):
    return jnp.minimum(x, 0.0) - jnp.log1p(jnp.exp(-jnp.abs(x)))


def _ada_kernel(c_ref, w_ref, b_ref, o_ref):
    o_ref[...] = jnp.dot(c_ref[...], w_ref[...], precision=lax.Precision.HIGHEST,
                         preferred_element_type=F32) + b_ref[...]


def _ada_call(c, w_ada, b_ada):
    B, D = c.shape
    return pl.pallas_call(
        _ada_kernel,
        grid=(3,),
        in_specs=[pl.BlockSpec((B, D), lambda j: (0, 0)),
                  pl.BlockSpec((D, D), lambda j: (0, j)),
                  pl.BlockSpec((1, D), lambda j: (0, j))],
        out_specs=pl.BlockSpec((B, D), lambda j: (0, j)),
        out_shape=jax.ShapeDtypeStruct((B, 3 * D), F32),
        compiler_params=pltpu.CompilerParams(dimension_semantics=("arbitrary",),
                                             vmem_limit_bytes=VMEM_LIMIT),
        name="ada",
    )(c, w_ada, b_ada.reshape(1, 3 * D))


def _inproj_kernel(x_ref, ada_ref, nw_ref, wa_ref, wb_ref, wg_ref, gb_ref, hnw_ref,
                   p_ref, g_ref, gt_ref, h_scr):
    j = pl.program_id(1)
    tn = p_ref.shape[1]

    @pl.when(j == 0)
    def _():
        x = x_ref[...]
        ada = ada_ref[0]
        y = x * lax.rsqrt(jnp.mean(x * x, axis=-1, keepdims=True) + EPS) * nw_ref[...]
        h = (y * (1.0 + ada[1:2, :]) + ada[0:1, :]).astype(BF16)
        h_scr[...] = h
        g = jnp.dot(h, wg_ref[...], preferred_element_type=F32) + gb_ref[...]
        g_ref[...] = g
        gt_ref[...] = g.T[:GATE_COLS, :]

    @pl.when(j <= COL_KA)
    def _():
        h = h_scr[...]
        w = hnw_ref[pl.ds(j, 1), :]
        pair = 2 * ATT_HEAD_DIM
        for c0 in range(0, tn, pair):
            acc = jnp.dot(h, wa_ref[:, c0:c0 + pair], preferred_element_type=F32)
            for c in range(c0, c0 + pair, ATT_HEAD_DIM):
                a = acc[:, c - c0:c - c0 + ATT_HEAD_DIM]
                r = lax.rsqrt(jnp.mean(a * a, axis=-1, keepdims=True) + EPS)
                p_ref[:, c:c + ATT_HEAD_DIM] = (a * r * w[:, c:c + ATT_HEAD_DIM]).astype(BF16)

    @pl.when(jnp.logical_and(j > COL_KA, j < N_HEAD_TILES))
    def _():
        p_ref[...] = jnp.dot(h_scr[...], wa_ref[...], preferred_element_type=F32).astype(BF16)

    @pl.when(j >= N_HEAD_TILES)
    def _():
        p_ref[...] = jnp.dot(h_scr[...], wb_ref[...], preferred_element_type=F32).astype(BF16)


def _inproj_call(x2, ada3, norm_w, wa, wb, wg, gb, hnw, seq):
    T, D = x2.shape
    tm, tn = PROJ_TM, PROJ_TN
    tiles_per_seq = seq // tm
    return pl.pallas_call(
        _inproj_kernel,
        grid=(T // tm, N_COL_TILES),
        in_specs=[pl.BlockSpec((tm, D), lambda i, j: (i, 0)),
                  pl.BlockSpec((1, 3, D), lambda i, j: (i // tiles_per_seq, 0, 0)),
                  pl.BlockSpec((1, D), lambda i, j: (0, 0)),
                  pl.BlockSpec((D, tn), lambda i, j: (0, jnp.minimum(j, N_HEAD_TILES - 1))),
                  pl.BlockSpec((D, tn), lambda i, j: (0, jnp.maximum(j - N_HEAD_TILES, 0))),
                  pl.BlockSpec((D, LANES), lambda i, j: (0, 0)),
                  pl.BlockSpec((1, LANES), lambda i, j: (0, 0)),
                  pl.BlockSpec((2, tn), lambda i, j: (0, 0))],
        out_specs=[pl.BlockSpec((tm, tn), lambda i, j: (i, j)),
                   pl.BlockSpec((tm, LANES), lambda i, j: (i, 0)),
                   pl.BlockSpec((GATE_COLS, tm), lambda i, j: (0, i))],
        out_shape=[jax.ShapeDtypeStruct((T, N_COL_TILES * tn), BF16),
                   jax.ShapeDtypeStruct((T, LANES), F32),
                   jax.ShapeDtypeStruct((GATE_COLS, T), F32)],
        scratch_shapes=[pltpu.VMEM((tm, D), BF16)],
        compiler_params=pltpu.CompilerParams(dimension_semantics=("arbitrary", "arbitrary"),
                                             vmem_limit_bytes=VMEM_LIMIT),
        name="inproj",
    )(x2, ada3, norm_w, wa, wb, wg, gb, hnw)


def _moba_kernel(qa_ref, qb_ref, k_ref, v_ref, zaa_ref, zab_ref, bown_ref, bprev_ref, o_ref,
                 kmean_f32_scr, kmean_scr, vt_scr, q2_scr, selb_scr, s_scr, m8_scr):
    step = pl.program_id(2)
    blk, d = MOBA_BLOCK, ATT_HEAD_DIM
    da = d + BF16_SUBLANES
    nb = k_ref.shape[0] // blk
    heads = qa_ref.shape[1] // d
    tile = (step, nb - 1 - step)
    n_far = (jnp.maximum(step - 1, 0), nb - 2 - step)
    near_slots = 4
    far_slots = nb - near_slots + 1

    def hsl(hd):
        return slice(hd * d, (hd + 1) * d)

    @pl.when(step == 0)
    def _():
        for n in range(nb):
            kb = k_ref[n * blk:(n + 1) * blk, :].astype(F32)
            kmean_f32_scr[n:n + 1, :] = jnp.mean(kb, axis=0, keepdims=True)
            for hd in range(heads):
                vt_scr[n, hd * da:hd * da + d, :] = v_ref[n * blk:(n + 1) * blk, hsl(hd)].T
                vt_scr[n, hd * da + d:(hd + 1) * da, :] = jnp.ones((da - d, blk), BF16)

        rest = kmean_f32_scr[...]
        for c in range(KMEAN_TERMS):
            term = rest.astype(BF16)
            kmean_scr[c * nb:(c + 1) * nb, :] = term
            rest = rest - term.astype(F32)

    q2_scr[0] = qa_ref[...]
    q2_scr[1] = qb_ref[...]

    def fold(x):
        return x.reshape(blk // SUBLANES, SUBLANES, blk)

    def col_max8(s):
        x = fold(s)
        n = x.shape[0]
        while n > 1:
            n //= 2
            x = jnp.maximum(x[:n], x[n:2 * n])
        return x[0]

    rows = lax.broadcasted_iota(jnp.int32, (nb, blk), 0)

    def select_blocks(hd):
        for x in range(2):
            valid = rows < tile[x]
            terms = lax.dot_general(kmean_scr[:, hsl(hd)], q2_scr[x, :, hsl(hd)], _NT,
                                    preferred_element_type=F32)
            gate = terms[:nb]
            for c in range(1, KMEAN_TERMS):
                gate = gate + terms[c * nb:(c + 1) * nb]
            g = jnp.where(valid, gate, -jnp.inf)
            selb = jnp.full(gate.shape, NEG_INF, F32)
            for _ in range(MOBA_TOPK):
                mx = jnp.max(g, axis=0, keepdims=True)
                first = jnp.min(jnp.where(g == mx, rows, nb), axis=0, keepdims=True)
                pick = rows == first
                selb = jnp.where(pick, 0.0, selb)
                g = jnp.where(pick, -jnp.inf, g)
            selb_scr[x, hd] = jnp.where(valid, selb, NEG_INF)

    def near_items(hd):
        first = step == 0
        jb = tile[1] - 1
        ja = jnp.where(first, far_slots, step - 1)
        add_a = jnp.where(first, selb_scr[1, hd, pl.ds(far_slots, 1), :],
                          bprev_ref[hd] + selb_scr[0, hd, pl.ds(jnp.maximum(step - 1, 0), 1), :])
        return [(0, 0, tile[0], bown_ref[hd], None),
                (1, 1, tile[1], bown_ref[hd], None),
                (2, 1, jb, bprev_ref[hd] + selb_scr[1, hd, pl.ds(jb, 1), :], None),
                (3, jnp.where(first, 1, 0), ja, add_a, jnp.logical_not(first))]

    shared_slots = nb // 2 - 2

    def far_items(hd):
        out = []
        for t in range(far_slots):
            if t < shared_slots:
                is_a = t < n_far[0]
                x = jnp.where(is_a, 0, 1)
                j = jnp.where(is_a, t, t - n_far[0])
            else:
                is_a, x, j = None, 1, t - n_far[0]
            out.append((near_slots + t, x, j, selb_scr[x, hd, pl.ds(j, 1), :], is_a))
        return out

    def pass1(hd, items):
        m8 = [None, None]
        for slot, x, j, add, is_a in items:
            kj = k_ref[pl.ds(pl.multiple_of(j * blk, blk), blk), hsl(hd)]
            s = lax.dot_general(kj, q2_scr[x, :, hsl(hd)], _NT, preferred_element_type=F32) + add
            s_scr[hd, slot] = s
            mt = col_max8(s)
            for y in range(2):
                if is_a is None:
                    if y != x:
                        continue
                    part = mt
                else:
                    part = jnp.where(is_a, mt, -jnp.inf) if y == 0 else jnp.where(is_a, -jnp.inf, mt)
                m8[y] = part if m8[y] is None else jnp.maximum(m8[y], part)
        for x in range(2):
            m8_scr[x, hd] = jnp.broadcast_to(jnp.max(m8[x], axis=0, keepdims=True), (SUBLANES, blk))

    def pass2(hd, items):
        acc = [None, None]
        acc_shared = None
        for slot, x, j, _, is_a in items:
            p = jnp.exp2(fold(s_scr[hd, slot]) - m8_scr[x, hd])
            pb = p.reshape(blk, blk).astype(BF16)
            vt = vt_scr[j, hd * da:(hd + 1) * da, :]
            if is_a is None:
                pv = jnp.dot(vt, pb, preferred_element_type=F32)
                acc[x] = pv if acc[x] is None else acc[x] + pv
            else:
                zero = jnp.zeros_like(vt)
                lhs = jnp.concatenate([jnp.where(is_a, vt, zero), jnp.where(is_a, zero, vt)], axis=0)
                pv = jnp.dot(lhs, pb, preferred_element_type=F32)
                acc_shared = pv if acc_shared is None else acc_shared + pv
        for x, za_ref in enumerate((zaa_ref, zab_ref)):
            tot = acc[x] + acc_shared[x * da:(x + 1) * da]
            y = (tot[:d] * (1.0 / tot[d:d + 1])).T
            za = za_ref[:, hsl(hd)].astype(F32)
            o_ref[0, x, 0, :, hsl(hd)] = (y * (za * _sigmoid(za))).astype(BF16)

    items = []
    for hd in range(heads):
        select_blocks(hd)
        items.append(near_items(hd) + far_items(hd))
        pass1(hd, items[hd])
        if hd:
            pass2(hd - 1, items[hd - 1])
    pass2(heads - 1, items[heads - 1])


def _moba_call(p, bias_own, bias_prev, batch, seq):
    blk, d = MOBA_BLOCK, ATT_HEAD_DIM
    nq = seq // blk
    hg = MOBA_HEAD_GROUP
    w = hg * d
    gpt = PROJ_TN // w
    first = lambda col: (lambda b, h, i: (b * nq + i, col * gpt + h))
    last = lambda col: (lambda b, h, i: (b * nq + nq - 1 - i, col * gpt + h))
    return pl.pallas_call(
        _moba_kernel,
        grid=(batch, ATT_HEADS // hg, nq // 2),
        in_specs=[pl.BlockSpec((blk, w), first(COL_QA)),
                  pl.BlockSpec((blk, w), last(COL_QA)),
                  pl.BlockSpec((seq, w), lambda b, h, i: (b, COL_KA * gpt + h)),
                  pl.BlockSpec((seq, w), lambda b, h, i: (b, COL_VA * gpt + h)),
                  pl.BlockSpec((blk, w), first(COL_ZA)),
                  pl.BlockSpec((blk, w), last(COL_ZA)),
                  pl.BlockSpec((hg, blk, blk), lambda b, h, i: (h, 0, 0)),
                  pl.BlockSpec((hg, blk, blk), lambda b, h, i: (h, 0, 0))],
        out_specs=pl.BlockSpec((1, 2, 1, blk, w), lambda b, h, i: (b, 0, i, 0, h)),
        out_shape=jax.ShapeDtypeStruct((batch, 2, nq // 2, blk, ATT_HEADS * d), BF16),
        scratch_shapes=[pltpu.VMEM((nq, w), F32),
                        pltpu.VMEM((KMEAN_TERMS * nq, w), BF16),
                        pltpu.VMEM((nq, hg * (d + BF16_SUBLANES), blk), BF16),
                        pltpu.VMEM((2, blk, w), BF16),
                        pltpu.VMEM((2, hg, nq, blk), F32),
                        pltpu.VMEM((hg, nq + 1, blk, blk), F32),
                        pltpu.VMEM((2, hg, SUBLANES, blk), F32)],
        compiler_params=pltpu.CompilerParams(
            dimension_semantics=("arbitrary", "arbitrary", "arbitrary"),
            vmem_limit_bytes=VMEM_LIMIT),
        name="moba",
    )(p, p, p, p, p, p, bias_own, bias_prev)


def _ua_tile_row(t, nq):
    half = nq // 2
    return jnp.where(t < half, t, half + (nq - 1 - t))


def _mlstm_kernel(qm_ref, km_ref, vm_ref, zm_ref, om_ref, g_ref, gt_ref, cw_ref, cb_ref, mlw_ref,
                  o_ref, c_scr, n_scr, m_scr, tailq_scr, tailk_scr, shift_scr):
    c_idx = pl.program_id(1)
    L = qm_ref.shape[0]
    W = ML_HEADS * ML_HEAD_DIM
    pad = SUBLANES

    r_i = lax.broadcasted_iota(jnp.int32, (L, L), 0)
    c_i = lax.broadcasted_iota(jnp.int32, (L, L), 1)

    @pl.when(c_idx == 0)
    def _():
        c_scr[...] = jnp.zeros_like(c_scr)
        n_scr[...] = jnp.zeros_like(n_scr)
        m_scr[...] = jnp.zeros_like(m_scr)
        tailq_scr[...] = jnp.zeros_like(tailq_scr)
        tailk_scr[...] = jnp.zeros_like(tailk_scr)
        for k in range(CONV_WIDTH):
            shift_scr[k] = (c_i == r_i - k).astype(F32).astype(BF16)

    row8 = lax.broadcasted_iota(jnp.int32, (pad, W), 0)

    def conv_silu(src_ref, tail_scr, col0):
        x = src_ref[...]
        tail = tail_scr[...]
        y = cb_ref[:, col0:col0 + W]
        fix = jnp.zeros((pad, W), F32)
        for w in range(CONV_WIDTH):
            k = CONV_WIDTH - 1 - w
            cw = cw_ref[w:w + 1, col0:col0 + W]
            y = y + cw * jnp.dot(shift_scr[k], x, preferred_element_type=F32)
            if k:
                fix = fix + cw * jnp.where(row8 < k, pltpu.roll(tail, k, axis=0), 0.0)
        tail_scr[...] = x[L - 2 * pad:, :].astype(F32)[pad:, :]
        y = jnp.concatenate([y[:pad] + fix, y[pad:]], axis=0)
        return y * _sigmoid(y)

    qc = conv_silu(qm_ref, tailq_scr, 0)
    kc = conv_silu(km_ref, tailk_scr, W) * (ML_HEAD_DIM ** -0.5)

    causal = c_i <= r_i
    ltri = causal.astype(F32)
    utri = (r_i <= c_i).astype(F32)
    g_col = g_ref[...]
    g_row = gt_ref[...]
    b_col_all = jnp.dot(ltri, _log_sigmoid(g_col), precision=lax.Precision.HIGHEST,
                        preferred_element_type=F32)
    b_row_all = jnp.dot(_log_sigmoid(g_row), utri, precision=lax.Precision.HIGHEST,
                        preferred_element_type=F32)

    for hh in range(ML_HEADS):
        sl = slice(hh * ML_HEAD_DIM, (hh + 1) * ML_HEAD_DIM)
        q = qc[:, sl]
        k = kc[:, sl]
        v = vm_ref[:, sl]
        b_col = b_col_all[:, ML_HEADS + hh:ML_HEADS + hh + 1]
        a_col = g_col[:, hh:hh + 1] - b_col
        a_row = g_row[hh:hh + 1, :] - b_row_all[ML_HEADS + hh:ML_HEADS + hh + 1, :]
        m_prev = m_scr[hh][0:1, 0:1]

        a_mask = jnp.where(causal, a_row, -jnp.inf)
        gcol = jnp.maximum(m_prev, jnp.max(a_mask, axis=1, keepdims=True))
        d_mat = jnp.exp(a_mask - gcol)
        inter = jnp.exp(m_prev - gcol)
        g_last = jnp.max(gcol, axis=0, keepdims=True)

        qb = q.astype(BF16)
        s = lax.dot_general(qb, k.astype(BF16), _NT, preferred_element_type=F32) * d_mat
        c_old = c_scr[hh]
        n_old = n_scr[hh]
        num = (inter * jnp.dot(qb, c_old.astype(BF16), preferred_element_type=F32)
               + jnp.dot(s.astype(BF16), v, preferred_element_type=F32))
        den = inter * jnp.sum(q * n_old, axis=1, keepdims=True) + jnp.sum(s, axis=1, keepdims=True)
        h = num / jnp.maximum(jnp.abs(den), jnp.exp(-(b_col + gcol)))

        decay = jnp.exp(m_prev - g_last)
        kw = k * jnp.exp(a_col - g_last)
        c_scr[hh] = decay * c_old + lax.dot_general(kw.astype(BF16), v, _TN,
                                                    preferred_element_type=F32)
        n_scr[hh] = decay * n_old + jnp.sum(kw, axis=0, keepdims=True)
        m_scr[hh] = jnp.broadcast_to(b_col[L - 1:L, :] + g_last, m_scr.shape[1:])

        hg = _sigmoid(om_ref[:, sl].astype(F32)) * h
        hn = hg * lax.rsqrt(jnp.mean(hg * hg, axis=-1, keepdims=True) + EPS) * mlw_ref[:, sl]
        z = zm_ref[:, sl].astype(F32)
        o_ref[:, sl] = (hn * (z * _sigmoid(z))).astype(BF16)


def _mlstm_call(p, g, gt, conv_w, conv_b, ml_norm_w, batch, seq):
    T = p.shape[0]
    L = ML_CHUNK
    nc = seq // L
    W = ML_HEADS * ML_HEAD_DIM
    row = lambda b, c: b * nc + c
    pspec = lambda col: pl.BlockSpec((L, W), lambda b, c: (row(b, c), col))
    return pl.pallas_call(
        _mlstm_kernel,
        grid=(batch, nc),
        in_specs=[pspec(COL_QM), pspec(COL_KM), pspec(COL_VM), pspec(COL_ZM), pspec(COL_OM),
                  pl.BlockSpec((L, LANES), lambda b, c: (row(b, c), 0)),
                  pl.BlockSpec((GATE_COLS, L), lambda b, c: (0, row(b, c))),
                  pl.BlockSpec((CONV_WIDTH, 2 * W), lambda b, c: (0, 0)),
                  pl.BlockSpec((1, 2 * W), lambda b, c: (0, 0)),
                  pl.BlockSpec((1, W), lambda b, c: (0, 0))],
        out_specs=pl.BlockSpec((L, W), lambda b, c: (row(b, c), 0)),
        out_shape=jax.ShapeDtypeStruct((T, W), BF16),
        scratch_shapes=[pltpu.VMEM((ML_HEADS, ML_HEAD_DIM, ML_HEAD_DIM), F32),
                        pltpu.VMEM((ML_HEADS, 1, ML_HEAD_DIM), F32),
                        pltpu.VMEM((ML_HEADS, SUBLANES, LANES), F32),
                        pltpu.VMEM((SUBLANES, W), F32),
                        pltpu.VMEM((SUBLANES, W), F32),
                        pltpu.VMEM((CONV_WIDTH, L, L), BF16)],
        compiler_params=pltpu.CompilerParams(dimension_semantics=("arbitrary", "arbitrary"),
                                             vmem_limit_bytes=VMEM_LIMIT),
        name="mlstm",
    )(p, p, p, p, p, g, gt, conv_w, conv_b, ml_norm_w)


def _outp_kernel(*refs):
    ua_refs, (um_ref, ga_ref, gm_ref, x_ref, ada_ref, wa_ref, wm_ref, wo_ref, o_ref) = refs[:-9], refs[-9:]
    ua = jnp.concatenate([r[...] for r in ua_refs], axis=0)
    ya = jnp.dot(ua, wa_ref[...], preferred_element_type=F32)
    ym = jnp.dot(um_ref[...], wm_ref[...], preferred_element_type=F32)
    y = _sigmoid(ga_ref[...].astype(F32)) * ya + _sigmoid(gm_ref[...].astype(F32)) * ym
    gate = ada_ref[0][2:3, :]
    o_ref[...] = x_ref[...] + gate * jnp.dot(y.astype(BF16), wo_ref[...], preferred_element_type=F32)


def _outp_call(ua, um, p, x2, ada3, wa, wm, wo, seq):
    T, D = x2.shape
    tm = OUT_TM
    steps_per_seq = seq // tm
    blk = MOBA_BLOCK
    nq = seq // blk
    tiles_per_step = tm // blk

    def ua_spec(e):
        def index(i):
            t = (i % steps_per_seq) * tiles_per_step + e
            return ((i // steps_per_seq) * nq + _ua_tile_row(t, nq), 0)
        return pl.BlockSpec((blk, D), index)

    act = lambda col: pl.BlockSpec((tm, D), lambda i: (i, col))
    wspec = pl.BlockSpec((D, D), lambda i: (0, 0))
    return pl.pallas_call(
        _outp_kernel,
        grid=(T // tm,),
        in_specs=[ua_spec(e) for e in range(tiles_per_step)]
                 + [act(0), act(COL_GA), act(COL_GM), act(0),
                    pl.BlockSpec((1, 3, D), lambda i: (i // steps_per_seq, 0, 0)),
                    wspec, wspec, wspec],
        out_specs=pl.BlockSpec((tm, D), lambda i: (i, 0)),
        out_shape=jax.ShapeDtypeStruct((T, D), F32),
        compiler_params=pltpu.CompilerParams(dimension_semantics=("arbitrary",),
                                             vmem_limit_bytes=VMEM_LIMIT),
        name="outp",
    )(*([ua] * tiles_per_step), um, p, p, x2, ada3, wa, wm, wo)


def _rel_bias_tables(rel_bias):
    max_exact = REL_BUCKETS // 2
    rb = rel_bias.astype(F32)
    rb = (rb - rb[REL_BUCKETS - 1:, :]) * LOG2E
    key = jnp.arange(MOBA_BLOCK)[:, None]
    qry = jnp.arange(MOBA_BLOCK)[None, :]
    d_own = qry - key

    def tile(dist):
        nf = jnp.maximum(dist, 1).astype(F32)
        large = max_exact + (jnp.log(nf / max_exact) / math.log(REL_MAX_DIST / max_exact)
                             * (REL_BUCKETS - max_exact)).astype(jnp.int32)
        bucket = jnp.where(dist < max_exact, dist, jnp.minimum(large, REL_BUCKETS - 1))
        onehot = (bucket[..., None] == jnp.arange(REL_BUCKETS)).astype(F32)
        return jnp.einsum('kqb,bh->hkq', onehot, rb, precision=lax.Precision.HIGHEST)

    own = jnp.where(d_own >= 0, tile(jnp.maximum(d_own, 0)), NEG_INF)
    prev = tile(d_own + MOBA_BLOCK)
    return own, prev


def _layer(x2, c, w_ada, b_ada, norm_w, w_in_all, layer, q_norm_w, k_norm_w, rel_bias, conv_w, conv_b,
           b_igate, b_fgate, ml_norm_w, w_att_proj, w_ml_proj, w_out, batch, seq):
    D = D_MODEL
    g0 = 4 * D + 5 * D
    ada3 = _ada_call(c, w_ada, b_ada).reshape(batch, 3, D)

    w_bf = w_in_all[layer].astype(BF16)
    wa = w_bf[:, :g0]
    wb = w_bf[:, g0 + GATE_COLS:]
    wg = jnp.pad(w_bf[:, g0:g0 + GATE_COLS], ((0, 0), (0, LANES - GATE_COLS)))
    gb = jnp.pad(jnp.concatenate([b_igate, b_fgate]), (0, LANES - GATE_COLS)).reshape(1, LANES)
    q_scale = ATT_HEAD_DIM ** -0.5 * LOG2E
    hnw = jnp.stack([jnp.tile(q_norm_w.astype(F32), ATT_HEADS) * q_scale,
                     jnp.tile(k_norm_w.astype(F32), ATT_HEADS)])

    p, g, gt = _inproj_call(x2, ada3, norm_w.reshape(1, D), wa, wb, wg, gb, hnw, seq)
    bias_own, bias_prev = _rel_bias_tables(rel_bias)
    ua = _moba_call(p, bias_own, bias_prev, batch, seq).reshape(batch * seq, D)
    um = _mlstm_call(p, g, gt, conv_w, conv_b.reshape(1, -1), ml_norm_w.reshape(1, -1), batch, seq)
    return _outp_call(ua, um, p, x2, ada3, w_att_proj.astype(BF16), w_ml_proj.astype(BF16),
                      w_out.astype(BF16), seq)


def kernel(x, c, w_ada, b_ada, norm_w, w_in, q_norm_w, k_norm_w, rel_bias, conv_w, conv_b, b_igate,
           b_fgate, ml_norm_w, w_att_proj, w_ml_proj, w_out):
    batch, seq, D = x.shape
    assert D == D_MODEL and seq % PROJ_TM == 0 and seq % MOBA_BLOCK == 0 and seq % ML_CHUNK == 0
    x2 = x.reshape(batch * seq, D)
    for l in range(w_in.shape[0]):
        x2 = _layer(x2, c, w_ada[l], b_ada[l], norm_w[l], w_in, l, q_norm_w[l], k_norm_w[l],
                    rel_bias, conv_w[l], conv_b[l], b_igate[l], b_fgate[l], ml_norm_w[l],
                    w_att_proj[l], w_ml_proj[l], w_out[l], batch, seq)
    return x2.reshape(batch, seq, D)
```

```python
import math

import jax
import jax.numpy as jnp
from jax import lax
from jax.experimental import pallas as pl
from jax.experimental.pallas import tpu as pltpu

F32 = jnp.float32
BF16 = jnp.bfloat16

D_MODEL = 1024
ATT_HEADS = 8
ATT_HEAD_DIM = D_MODEL // ATT_HEADS
MOBA_BLOCK = 256
MOBA_TOPK = 3
REL_BUCKETS = 32
REL_MAX_DIST = 128
ML_HEADS = 4
ML_HEAD_DIM = D_MODEL // ML_HEADS
CONV_WIDTH = 4
EPS = 1e-6
NEG_INF = -1e30
LOG2E = math.log2(math.e)

LANES = 128
SUBLANES = 8
BF16_SUBLANES = 16
GATE_COLS = 2 * ML_HEADS
ML_CHUNK = 256
PROJ_TM = 1024
PROJ_TN = 1024
PROJ_STEP_TILES = 2
OUT_TM = 512
MOBA_HEAD_GROUP = 4
KMEAN_TERMS = 3
VMEM_LIMIT = 52 * 1024 * 1024

COL_QA, COL_KA, COL_VA, COL_ZA, COL_QM, COL_KM, COL_VM, COL_ZM, COL_OM, COL_GA, COL_GM = range(11)
N_COL_TILES = 11
N_HEAD_TILES = 9

_NT = (((1,), (1,)), ((), ()))
_TN = (((0,), (0,)), ((), ()))


def _sigmoid(x):
    return 1.0 / (1.0 + jnp.exp2(x * (-LOG2E)))


def _log_sigmoid(x):
    return jnp.minimum(x, 0.0) - jnp.log1p(jnp.exp(-jnp.abs(x)))


def _ada_kernel(c_ref, w_ref, b_ref, o_ref):
    o_ref[...] = jnp.dot(c_ref[...], w_ref[...], precision=lax.Precision.HIGHEST,
                         preferred_element_type=F32) + b_ref[...]


def _ada_call(c, w_ada, b_ada):
    B, D = c.shape
    return pl.pallas_call(
        _ada_kernel,
        grid=(3,),
        in_specs=[pl.BlockSpec((B, D), lambda j: (0, 0)),
                  pl.BlockSpec((D, D), lambda j: (0, j)),
                  pl.BlockSpec((1, D), lambda j: (0, j))],
        out_specs=pl.BlockSpec((B, D), lambda j: (0, j)),
        out_shape=jax.ShapeDtypeStruct((B, 3 * D), F32),
        compiler_params=pltpu.CompilerParams(dimension_semantics=("arbitrary",),
                                             vmem_limit_bytes=VMEM_LIMIT),
        name="ada",
    )(c, w_ada, b_ada.reshape(1, 3 * D))


def _inproj_kernel(x_ref, ada_ref, nw_ref, wa_ref, wb_ref, wg_ref, gb_ref, hnw_ref,
                   p_ref, g_ref, gt_ref, h_scr):
    j = pl.program_id(1)
    tn = PROJ_TN
    wa_steps = N_HEAD_TILES // PROJ_STEP_TILES

    def project(w):
        return jnp.dot(h_scr[...], w, preferred_element_type=F32).astype(BF16)

    @pl.when(j == 0)
    def _():
        x = x_ref[...]
        ada = ada_ref[0]
        y = x * lax.rsqrt(jnp.mean(x * x, axis=-1, keepdims=True) + EPS) * nw_ref[...]
        h = (y * (1.0 + ada[1:2, :]) + ada[0:1, :]).astype(BF16)
        h_scr[...] = h
        g = jnp.dot(h, wg_ref[...], preferred_element_type=F32) + gb_ref[...]
        g_ref[...] = g
        gt_ref[...] = g.T[:GATE_COLS, :]

    @pl.when(j == 0)
    def _():
        h = h_scr[...]
        pair = 2 * ATT_HEAD_DIM
        for t in (COL_QA, COL_KA):
            w = hnw_ref[t:t + 1, :]
            for c0 in range(0, tn, pair):
                acc = jnp.dot(h, wa_ref[:, t * tn + c0:t * tn + c0 + pair], preferred_element_type=F32)
                for c in range(c0, c0 + pair, ATT_HEAD_DIM):
                    a = acc[:, c - c0:c - c0 + ATT_HEAD_DIM]
                    r = lax.rsqrt(jnp.mean(a * a, axis=-1, keepdims=True) + EPS)
                    p_ref[:, t * tn + c:t * tn + c + ATT_HEAD_DIM] = (
                        a * r * w[:, c:c + ATT_HEAD_DIM]).astype(BF16)

    @pl.when(jnp.logical_and(j > 0, j < wa_steps))
    def _():
        p_ref[...] = project(wa_ref[...])

    @pl.when(j == wa_steps)
    def _():
        p_ref[:, :tn] = project(wa_ref[:, :tn])
        p_ref[:, tn:] = project(wb_ref[:, :tn])

    @pl.when(j == wa_steps + 1)
    def _():
        p_ref[:, :tn] = project(wb_ref[:, tn:])
        p_ref[:, tn:] = jnp.zeros((p_ref.shape[0], tn), BF16)


def _inproj_call(x2, ada3, norm_w, wa, wb, wg, gb, hnw, seq):
    T, D = x2.shape
    tm, tn = PROJ_TM, PROJ_TN
    ts = PROJ_STEP_TILES * tn
    tiles_per_seq = seq // tm
    wa_steps = N_HEAD_TILES // PROJ_STEP_TILES
    n_steps = wa_steps + 2
    assert PROJ_STEP_TILES == 2 and N_HEAD_TILES % 2 == 1 and N_COL_TILES - N_HEAD_TILES == 2
    return pl.pallas_call(
        _inproj_kernel,
        grid=(T // tm, n_steps),
        in_specs=[pl.BlockSpec((tm, D), lambda i, j: (i, 0)),
                  pl.BlockSpec((1, 3, D), lambda i, j: (i // tiles_per_seq, 0, 0)),
                  pl.BlockSpec((1, D), lambda i, j: (0, 0)),
                  pl.BlockSpec((D, ts), lambda i, j: (0, jnp.minimum(j, wa_steps))),
                  pl.BlockSpec((D, ts), lambda i, j: (0, 0)),
                  pl.BlockSpec((D, LANES), lambda i, j: (0, 0)),
                  pl.BlockSpec((1, LANES), lambda i, j: (0, 0)),
                  pl.BlockSpec((2, tn), lambda i, j: (0, 0))],
        out_specs=[pl.BlockSpec((tm, ts), lambda i, j: (i, j)),
                   pl.BlockSpec((tm, LANES), lambda i, j: (i, 0)),
                   pl.BlockSpec((GATE_COLS, tm), lambda i, j: (0, i))],
        out_shape=[jax.ShapeDtypeStruct((T, n_steps * ts), BF16),
                   jax.ShapeDtypeStruct((T, LANES), F32),
                   jax.ShapeDtypeStruct((GATE_COLS, T), F32)],
        scratch_shapes=[pltpu.VMEM((tm, D), BF16)],
        compiler_params=pltpu.CompilerParams(dimension_semantics=("arbitrary", "arbitrary"),
                                             vmem_limit_bytes=VMEM_LIMIT),
        name="inproj",
    )(x2, ada3, norm_w, wa, wb, wg, gb, hnw)


def _moba_kernel(qa_ref, qb_ref, k_ref, v_ref, zaa_ref, zab_ref, bown_ref, bprev_ref, o_ref,
                 kmean_f32_scr, kmean_scr, vt_scr, q2_scr, selb_scr, s_scr, m8_scr):
    step = pl.program_id(2)
    blk, d = MOBA_BLOCK, ATT_HEAD_DIM
    da = d + BF16_SUBLANES
    nb = k_ref.shape[0] // blk
    heads = qa_ref.shape[1] // d
    tile = (step, nb - 1 - step)
    n_far = (jnp.maximum(step - 1, 0), nb - 2 - step)
    near_slots = 4
    far_slots = nb - near_slots + 1

    def hsl(hd):
        return slice(hd * d, (hd + 1) * d)

    @pl.when(step == 0)
    def _():
        for n in range(nb):
            kb = k_ref[n * blk:(n + 1) * blk, :].astype(F32)
            kmean_f32_scr[n:n + 1, :] = jnp.mean(kb, axis=0, keepdims=True)
            for hd in range(heads):
                vt_scr[n, hd * da:hd * da + d, :] = v_ref[n * blk:(n + 1) * blk, hsl(hd)].T
                vt_scr[n, hd * da + d:(hd + 1) * da, :] = jnp.ones((da - d, blk), BF16)

        rest = kmean_f32_scr[...]
        for c in range(KMEAN_TERMS):
            term = rest.astype(BF16)
            kmean_scr[c * nb:(c + 1) * nb, :] = term
            rest = rest - term.astype(F32)

    q2_scr[0] = qa_ref[...]
    q2_scr[1] = qb_ref[...]

    def fold(x):
        return x.reshape(blk // SUBLANES, SUBLANES, blk)

    def col_max8(s):
        x = fold(s)
        n = x.shape[0]
        while n > 1:
            n //= 2
            x = jnp.maximum(x[:n], x[n:2 * n])
        return x[0]

    rows = lax.broadcasted_iota(jnp.int32, (nb, blk), 0)

    def select_blocks(hd):
        for x in range(2):
            valid = rows < tile[x]
            terms = lax.dot_general(kmean_scr[:, hsl(hd)], q2_scr[x, :, hsl(hd)], _NT,
                                    preferred_element_type=F32)
            gate = terms[:nb]
            for c in range(1, KMEAN_TERMS):
                gate = gate + terms[c * nb:(c + 1) * nb]
            g = jnp.where(valid, gate, -jnp.inf)
            selb = jnp.full(gate.shape, NEG_INF, F32)
            for _ in range(MOBA_TOPK):
                mx = jnp.max(g, axis=0, keepdims=True)
                first = jnp.min(jnp.where(g == mx, rows, nb), axis=0, keepdims=True)
                pick = rows == first
                selb = jnp.where(pick, 0.0, selb)
                g = jnp.where(pick, -jnp.inf, g)
            selb_scr[x, hd] = jnp.where(valid, selb, NEG_INF)

    def near_items(hd):
        first = step == 0
        jb = tile[1] - 1
        ja = jnp.where(first, far_slots, step - 1)
        add_a = jnp.where(first, selb_scr[1, hd, pl.ds(far_slots, 1), :],
                          bprev_ref[hd] + selb_scr[0, hd, pl.ds(jnp.maximum(step - 1, 0), 1), :])
        return [(0, 0, tile[0], bown_ref[hd], None),
                (1, 1, tile[1], bown_ref[hd], None),
                (2, 1, jb, bprev_ref[hd] + selb_scr[1, hd, pl.ds(jb, 1), :], None),
                (3, jnp.where(first, 1, 0), ja, add_a, jnp.logical_not(first))]

    shared_slots = nb // 2 - 2

    def far_items(hd):
        out = []
        for t in range(far_slots):
            if t < shared_slots:
                is_a = t < n_far[0]
                x = jnp.where(is_a, 0, 1)
                j = jnp.where(is_a, t, t - n_far[0])
            else:
                is_a, x, j = None, 1, t - n_far[0]
            out.append((near_slots + t, x, j, selb_scr[x, hd, pl.ds(j, 1), :], is_a))
        return out

    def pass1(hd, items):
        m8 = [None, None]
        for slot, x, j, add, is_a in items:
            kj = k_ref[pl.ds(pl.multiple_of(j * blk, blk), blk), hsl(hd)]
            s = lax.dot_general(kj, q2_scr[x, :, hsl(hd)], _NT, preferred_element_type=F32) + add
            s_scr[hd, slot] = s
            mt = col_max8(s)
            for y in range(2):
                if is_a is None:
                    if y != x:
                        continue
                    part = mt
                else:
                    part = jnp.where(is_a, mt, -jnp.inf) if y == 0 else jnp.where(is_a, -jnp.inf, mt)
                m8[y] = part if m8[y] is None else jnp.maximum(m8[y], part)
        for x in range(2):
            m8_scr[x, hd] = jnp.broadcast_to(jnp.max(m8[x], axis=0, keepdims=True), (SUBLANES, blk))

    def pass2(hd, items):
        acc = [None, None]
        acc_shared = None
        for slot, x, j, _, is_a in items:
            p = jnp.exp2(fold(s_scr[hd, slot]) - m8_scr[x, hd])
            pb = p.reshape(blk, blk).astype(BF16)
            vt = vt_scr[j, hd * da:(hd + 1) * da, :]
            if is_a is None:
                pv = jnp.dot(vt, pb, preferred_element_type=F32)
                acc[x] = pv if acc[x] is None else acc[x] + pv
            else:
                zero = jnp.zeros_like(vt)
                lhs = jnp.concatenate([jnp.where(is_a, vt, zero), jnp.where(is_a, zero, vt)], axis=0)
                pv = jnp.dot(lhs, pb, preferred_element_type=F32)
                acc_shared = pv if acc_shared is None else acc_shared + pv
        for x, za_ref in enumerate((zaa_ref, zab_ref)):
            tot = acc[x] + acc_shared[x * da:(x + 1) * da]
            y = (tot[:d] * (1.0 / tot[d:d + 1])).T
            za = za_ref[:, hsl(hd)].astype(F32)
            o_ref[0, x, 0, :, hsl(hd)] = (y * (za * _sigmoid(za))).astype(BF16)

    items = []
    for hd in range(heads):
        select_blocks(hd)
        items.append(near_items(hd) + far_items(hd))
        pass1(hd, items[hd])
        if hd:
            pass2(hd - 1, items[hd - 1])
    pass2(heads - 1, items[heads - 1])


def _moba_call(p, bias_own, bias_prev, batch, seq):
    blk, d = MOBA_BLOCK, ATT_HEAD_DIM
    nq = seq // blk
    hg = MOBA_HEAD_GROUP
    w = hg * d
    gpt = PROJ_TN // w
    first = lambda col: (lambda b, h, i: (b * nq + i, col * gpt + h))
    last = lambda col: (lambda b, h, i: (b * nq + nq - 1 - i, col * gpt + h))
    return pl.pallas_call(
        _moba_kernel,
        grid=(batch, ATT_HEADS // hg, nq // 2),
        in_specs=[pl.BlockSpec((blk, w), first(COL_QA)),
                  pl.BlockSpec((blk, w), last(COL_QA)),
                  pl.BlockSpec((seq, w), lambda b, h, i: (b, COL_KA * gpt + h)),
                  pl.BlockSpec((seq, w), lambda b, h, i: (b, COL_VA * gpt + h)),
                  pl.BlockSpec((blk, w), first(COL_ZA)),
                  pl.BlockSpec((blk, w), last(COL_ZA)),
                  pl.BlockSpec((hg, blk, blk), lambda b, h, i: (h, 0, 0)),
                  pl.BlockSpec((hg, blk, blk), lambda b, h, i: (h, 0, 0))],
        out_specs=pl.BlockSpec((1, 2, 1, blk, w), lambda b, h, i: (b, 0, i, 0, h)),
        out_shape=jax.ShapeDtypeStruct((batch, 2, nq // 2, blk, ATT_HEADS * d), BF16),
        scratch_shapes=[pltpu.VMEM((nq, w), F32),
                        pltpu.VMEM((KMEAN_TERMS * nq, w), BF16),
                        pltpu.VMEM((nq, hg * (d + BF16_SUBLANES), blk), BF16),
                        pltpu.VMEM((2, blk, w), BF16),
                        pltpu.VMEM((2, hg, nq, blk), F32),
                        pltpu.VMEM((hg, nq + 1, blk, blk), F32),
                        pltpu.VMEM((2, hg, SUBLANES, blk), F32)],
        compiler_params=pltpu.CompilerParams(
            dimension_semantics=("arbitrary", "arbitrary", "arbitrary"),
            vmem_limit_bytes=VMEM_LIMIT),
        name="moba",
    )(p, p, p, p, p, p, bias_own, bias_prev)


def _ua_tile_row(t, nq):
    half = nq // 2
    return jnp.where(t < half, t, half + (nq - 1 - t))


def _mlstm_kernel(qm_ref, km_ref, vm_ref, zm_ref, om_ref, g_ref, gt_ref, cw_ref, cb_ref, mlw_ref,
                  o_ref, c_scr, n_scr, m_scr, tailq_scr, tailk_scr, shift_scr):
    c_idx = pl.program_id(1)
    L = qm_ref.shape[0]
    W = ML_HEADS * ML_HEAD_DIM
    pad = SUBLANES

    r_i = lax.broadcasted_iota(jnp.int32, (L, L), 0)
    c_i = lax.broadcasted_iota(jnp.int32, (L, L), 1)

    @pl.when(c_idx == 0)
    def _():
        c_scr[...] = jnp.zeros_like(c_scr)
        n_scr[...] = jnp.zeros_like(n_scr)
        m_scr[...] = jnp.zeros_like(m_scr)
        tailq_scr[...] = jnp.zeros_like(tailq_scr)
        tailk_scr[...] = jnp.zeros_like(tailk_scr)
        for k in range(CONV_WIDTH):
            shift_scr[k] = (c_i == r_i - k).astype(F32).astype(BF16)

    row8 = lax.broadcasted_iota(jnp.int32, (pad, W), 0)

    def conv_silu(src_ref, tail_scr, col0):
        x = src_ref[...]
        tail = tail_scr[...]
        y = cb_ref[:, col0:col0 + W]
        fix = jnp.zeros((pad, W), F32)
        for w in range(CONV_WIDTH):
            k = CONV_WIDTH - 1 - w
            cw = cw_ref[w:w + 1, col0:col0 + W]
            y = y + cw * jnp.dot(shift_scr[k], x, preferred_element_type=F32)
            if k:
                fix = fix + cw * jnp.where(row8 < k, pltpu.roll(tail, k, axis=0), 0.0)
        tail_scr[...] = x[L - 2 * pad:, :].astype(F32)[pad:, :]
        y = jnp.concatenate([y[:pad] + fix, y[pad:]], axis=0)
        return y * _sigmoid(y)

    qc = conv_silu(qm_ref, tailq_scr, 0)
    kc = conv_silu(km_ref, tailk_scr, W) * (ML_HEAD_DIM ** -0.5)

    causal = c_i <= r_i
    ltri = causal.astype(F32)
    utri = (r_i <= c_i).astype(F32)
    g_col = g_ref[...]
    g_row = gt_ref[...]
    b_col_all = jnp.dot(ltri, _log_sigmoid(g_col), precision=lax.Precision.HIGHEST,
                        preferred_element_type=F32)
    b_row_all = jnp.dot(_log_sigmoid(g_row), utri, precision=lax.Precision.HIGHEST,
                        preferred_element_type=F32)

    for hh in range(ML_HEADS):
        sl = slice(hh * ML_HEAD_DIM, (hh + 1) * ML_HEAD_DIM)
        q = qc[:, sl]
        k = kc[:, sl]
        v = vm_ref[:, sl]
        b_col = b_col_all[:, ML_HEADS + hh:ML_HEADS + hh + 1]
        a_col = g_col[:, hh:hh + 1] - b_col
        a_row = g_row[hh:hh + 1, :] - b_row_all[ML_HEADS + hh:ML_HEADS + hh + 1, :]
        m_prev = m_scr[hh][0:1, 0:1]

        a_mask = jnp.where(causal, a_row, -jnp.inf)
        gcol = jnp.maximum(m_prev, jnp.max(a_mask, axis=1, keepdims=True))
        d_mat = jnp.exp(a_mask - gcol)
        inter = jnp.exp(m_prev - gcol)
        g_last = jnp.max(gcol, axis=0, keepdims=True)

        qb = q.astype(BF16)
        s = lax.dot_general(qb, k.astype(BF16), _NT, preferred_element_type=F32) * d_mat
        c_old = c_scr[hh]
        n_old = n_scr[hh]
        num = (inter * jnp.dot(qb, c_old.astype(BF16), preferred_element_type=F32)
               + jnp.dot(s.astype(BF16), v, preferred_element_type=F32))
        den = inter * jnp.sum(q * n_old, axis=1, keepdims=True) + jnp.sum(s, axis=1, keepdims=True)
        h = num / jnp.maximum(jnp.abs(den), jnp.exp(-(b_col + gcol)))

        decay = jnp.exp(m_prev - g_last)
        kw = k * jnp.exp(a_col - g_last)
        c_scr[hh] = decay * c_old + lax.dot_general(kw.astype(BF16), v, _TN,
                                                    preferred_element_type=F32)
        n_scr[hh] = decay * n_old + jnp.sum(kw, axis=0, keepdims=True)
        m_scr[hh] = jnp.broadcast_to(b_col[L - 1:L, :] + g_last, m_scr.shape[1:])

        hg = _sigmoid(om_ref[:, sl].astype(F32)) * h
        hn = hg * lax.rsqrt(jnp.mean(hg * hg, axis=-1, keepdims=True) + EPS) * mlw_ref[:, sl]
        z = zm_ref[:, sl].astype(F32)
        o_ref[:, sl] = (hn * (z * _sigmoid(z))).astype(BF16)


def _mlstm_call(p, g, gt, conv_w, conv_b, ml_norm_w, batch, seq):
    T = p.shape[0]
    L = ML_CHUNK
    nc = seq // L
    W = ML_HEADS * ML_HEAD_DIM
    row = lambda b, c: b * nc + c
    pspec = lambda col: pl.BlockSpec((L, W), lambda b, c: (row(b, c), col))
    return pl.pallas_call(
        _mlstm_kernel,
        grid=(batch, nc),
        in_specs=[pspec(COL_QM), pspec(COL_KM), pspec(COL_VM), pspec(COL_ZM), pspec(COL_OM),
                  pl.BlockSpec((L, LANES), lambda b, c: (row(b, c), 0)),
                  pl.BlockSpec((GATE_COLS, L), lambda b, c: (0, row(b, c))),
                  pl.BlockSpec((CONV_WIDTH, 2 * W), lambda b, c: (0, 0)),
                  pl.BlockSpec((1, 2 * W), lambda b, c: (0, 0)),
                  pl.BlockSpec((1, W), lambda b, c: (0, 0))],
        out_specs=pl.BlockSpec((L, W), lambda b, c: (row(b, c), 0)),
        out_shape=jax.ShapeDtypeStruct((T, W), BF16),
        scratch_shapes=[pltpu.VMEM((ML_HEADS, ML_HEAD_DIM, ML_HEAD_DIM), F32),
                        pltpu.VMEM((ML_HEADS, 1, ML_HEAD_DIM), F32),
                        pltpu.VMEM((ML_HEADS, SUBLANES, LANES), F32),
                        pltpu.VMEM((SUBLANES, W), F32),
                        pltpu.VMEM((SUBLANES, W), F32),
                        pltpu.VMEM((CONV_WIDTH, L, L), BF16)],
        compiler_params=pltpu.CompilerParams(dimension_semantics=("arbitrary", "arbitrary"),
                                             vmem_limit_bytes=VMEM_LIMIT),
        name="mlstm",
    )(p, p, p, p, p, g, gt, conv_w, conv_b, ml_norm_w)


def _outp_kernel(*refs):
    ua_refs, (um_ref, ga_ref, gm_ref, x_ref, ada_ref, wa_ref, wm_ref, wo_ref, o_ref) = refs[:-9], refs[-9:]
    ua = jnp.concatenate([r[...] for r in ua_refs], axis=0)
    ya = jnp.dot(ua, wa_ref[...], preferred_element_type=F32)
    ym = jnp.dot(um_ref[...], wm_ref[...], preferred_element_type=F32)
    y = _sigmoid(ga_ref[...].astype(F32)) * ya + _sigmoid(gm_ref[...].astype(F32)) * ym
    gate = ada_ref[0][2:3, :]
    o_ref[...] = x_ref[...] + gate * jnp.dot(y.astype(BF16), wo_ref[...], preferred_element_type=F32)


def _outp_call(ua, um, p, x2, ada3, wa, wm, wo, seq):
    T, D = x2.shape
    tm = OUT_TM
    steps_per_seq = seq // tm
    blk = MOBA_BLOCK
    nq = seq // blk
    tiles_per_step = tm // blk

    def ua_spec(e):
        def index(i):
            t = (i % steps_per_seq) * tiles_per_step + e
            return ((i // steps_per_seq) * nq + _ua_tile_row(t, nq), 0)
        return pl.BlockSpec((blk, D), index)

    act = lambda col: pl.BlockSpec((tm, D), lambda i: (i, col))
    wspec = pl.BlockSpec((D, D), lambda i: (0, 0))
    return pl.pallas_call(
        _outp_kernel,
        grid=(T // tm,),
        in_specs=[ua_spec(e) for e in range(tiles_per_step)]
                 + [act(0), act(COL_GA), act(COL_GM), act(0),
                    pl.BlockSpec((1, 3, D), lambda i: (i // steps_per_seq, 0, 0)),
                    wspec, wspec, wspec],
        out_specs=pl.BlockSpec((tm, D), lambda i: (i, 0)),
        out_shape=jax.ShapeDtypeStruct((T, D), F32),
        compiler_params=pltpu.CompilerParams(dimension_semantics=("arbitrary",),
                                             vmem_limit_bytes=VMEM_LIMIT),
        name="outp",
    )(*([ua] * tiles_per_step), um, p, p, x2, ada3, wa, wm, wo)


def _rel_bias_tables(rel_bias):
    max_exact = REL_BUCKETS // 2
    rb = rel_bias.astype(F32)
    rb = (rb - rb[REL_BUCKETS - 1:, :]) * LOG2E
    key = jnp.arange(MOBA_BLOCK)[:, None]
    qry = jnp.arange(MOBA_BLOCK)[None, :]
    d_own = qry - key

    def tile(dist):
        nf = jnp.maximum(dist, 1).astype(F32)
        large = max_exact + (jnp.log(nf / max_exact) / math.log(REL_MAX_DIST / max_exact)
                             * (REL_BUCKETS - max_exact)).astype(jnp.int32)
        bucket = jnp.where(dist < max_exact, dist, jnp.minimum(large, REL_BUCKETS - 1))
        onehot = (bucket[..., None] == jnp.arange(REL_BUCKETS)).astype(F32)
        return jnp.einsum('kqb,bh->hkq', onehot, rb, precision=lax.Precision.HIGHEST)

    own = jnp.where(d_own >= 0, tile(jnp.maximum(d_own, 0)), NEG_INF)
    prev = tile(d_own + MOBA_BLOCK)
    return own, prev


def _layer(x2, c, w_ada, b_ada, norm_w, w_in_all, layer, q_norm_w, k_norm_w, rel_bias, conv_w, conv_b,
           b_igate, b_fgate, ml_norm_w, w_att_proj, w_ml_proj, w_out, batch, seq):
    D = D_MODEL
    g0 = 4 * D + 5 * D
    ada3 = _ada_call(c, w_ada, b_ada).reshape(batch, 3, D)

    wa = w_in_all[layer, :, :g0].astype(BF16)
    wb = w_in_all[layer, :, g0 + GATE_COLS:].astype(BF16)
    wg = jnp.pad(w_in_all[layer, :, g0:g0 + GATE_COLS], ((0, 0), (0, LANES - GATE_COLS))).astype(BF16)
    gb = jnp.pad(jnp.concatenate([b_igate, b_fgate]), (0, LANES - GATE_COLS)).reshape(1, LANES)
    q_scale = ATT_HEAD_DIM ** -0.5 * LOG2E
    hnw = jnp.stack([jnp.tile(q_norm_w.astype(F32), ATT_HEADS) * q_scale,
                     jnp.tile(k_norm_w.astype(F32), ATT_HEADS)])

    p, g, gt = _inproj_call(x2, ada3, norm_w.reshape(1, D), wa, wb, wg, gb, hnw, seq)
    bias_own, bias_prev = _rel_bias_tables(rel_bias)
    ua = _moba_call(p, bias_own, bias_prev, batch, seq).reshape(batch * seq, D)
    um = _mlstm_call(p, g, gt, conv_w, conv_b.reshape(1, -1), ml_norm_w.reshape(1, -1), batch, seq)
    return _outp_call(ua, um, p, x2, ada3, w_att_proj.astype(BF16), w_ml_proj.astype(BF16),
                      w_out.astype(BF16), seq)


def kernel(x, c, w_ada, b_ada, norm_w, w_in, q_norm_w, k_norm_w, rel_bias, conv_w, conv_b, b_igate,
           b_fgate, ml_norm_w, w_att_proj, w_ml_proj, w_out):
    batch, seq, D = x.shape
    assert D == D_MODEL and seq % PROJ_TM == 0 and seq % MOBA_BLOCK == 0 and seq % ML_CHUNK == 0
    x2 = x.reshape(batch * seq, D)
    for l in range(w_in.shape[0]):
        x2 = _layer(x2, c, w_ada[l], b_ada[l], norm_w[l], w_in, l, q_norm_w[l], k_norm_w[l],
                    rel_bias, conv_w[l], conv_b[l], b_igate[l], b_fgate[l], ml_norm_w[l],
                    w_att_proj[l], w_ml_proj[l], w_out[l], batch, seq)
    return x2.reshape(batch, seq, D)
```

```python
import math

import jax
import jax.numpy as jnp
from jax import lax
from jax.experimental import pallas as pl
from jax.experimental.pallas import tpu as pltpu

F32 = jnp.float32
BF16 = jnp.bfloat16

D_MODEL = 1024
ATT_HEADS = 8
ATT_HEAD_DIM = D_MODEL // ATT_HEADS
MOBA_BLOCK = 256
MOBA_TOPK = 3
REL_BUCKETS = 32
REL_MAX_DIST = 128
ML_HEADS = 4
ML_HEAD_DIM = D_MODEL // ML_HEADS
CONV_WIDTH = 4
EPS = 1e-6
NEG_INF = -1e30
LOG2E = math.log2(math.e)

LANES = 128
SUBLANES = 8
BF16_SUBLANES = 16
GATE_COLS = 2 * ML_HEADS
ML_CHUNK = 256
PROJ_TM = 1024
PROJ_TN = 1024
PROJ_STEP_TILES = 2
WA_RING = 3
OUT_TM = 512
MOBA_HEAD_GROUP = 4
KMEAN_TERMS = 3
VMEM_LIMIT = 52 * 1024 * 1024

COL_QA, COL_KA, COL_VA, COL_ZA, COL_QM, COL_KM, COL_VM, COL_ZM, COL_OM, COL_GA, COL_GM = range(11)
N_COL_TILES = 11
N_HEAD_TILES = 9

_NT = (((1,), (1,)), ((), ()))
_TN = (((0,), (0,)), ((), ()))


def _sigmoid(x):
    return 1.0 / (1.0 + jnp.exp2(x * (-LOG2E)))


def _log_sigmoid(x):
    return jnp.minimum(x, 0.0) - jnp.log1p(jnp.exp(-jnp.abs(x)))


def _ada_kernel(c_ref, w_ref, b_ref, o_ref):
    o_ref[...] = jnp.dot(c_ref[...], w_ref[...], precision=lax.Precision.HIGHEST,
                         preferred_element_type=F32) + b_ref[...]


def _ada_call(c, w_ada, b_ada):
    B, D = c.shape
    return pl.pallas_call(
        _ada_kernel,
        grid=(3,),
        in_specs=[pl.BlockSpec((B, D), lambda j: (0, 0)),
                  pl.BlockSpec((D, D), lambda j: (0, j)),
                  pl.BlockSpec((1, D), lambda j: (0, j))],
        out_specs=pl.BlockSpec((B, D), lambda j: (0, j)),
        out_shape=jax.ShapeDtypeStruct((B, 3 * D), F32),
        compiler_params=pltpu.CompilerParams(dimension_semantics=("arbitrary",),
                                             vmem_limit_bytes=VMEM_LIMIT),
        name="ada",
    )(c, w_ada, b_ada.reshape(1, 3 * D))


def _inproj_kernel(x_ref, ada_ref, nw_ref, wa_hbm, wb_ref, wg_ref, gb_ref, hnw_ref,
                   p_ref, g_ref, gt_ref, h_scr, wa_ring, wa_sems):
    i = pl.program_id(0)
    j = pl.program_id(1)
    n_i = pl.num_programs(0)
    n_j = pl.num_programs(1)
    tn = PROJ_TN
    ts = PROJ_STEP_TILES * tn
    wa_steps = N_HEAD_TILES // PROJ_STEP_TILES
    wa_chunks = wa_steps + 1

    def wa_copy(ii, jj):
        slot = (ii * wa_chunks + jj) % WA_RING
        return pltpu.make_async_copy(wa_hbm.at[:, pl.ds(pl.multiple_of(jj * ts, ts), ts)],
                                     wa_ring.at[slot], wa_sems.at[slot])

    step = i * n_j + j

    @pl.when(step == 0)
    def _():
        for jj in range(WA_RING - 1):
            wa_copy(0, jj).start()

    ahead = step + WA_RING - 1
    i_ahead, j_ahead = ahead // n_j, ahead % n_j

    @pl.when(jnp.logical_and(j_ahead < wa_chunks, i_ahead < n_i))
    def _():
        wa_copy(i_ahead, j_ahead).start()

    @pl.when(j < wa_chunks)
    def _():
        wa_copy(i, j).wait()

    wa_ref = wa_ring.at[(i * wa_chunks + j) % WA_RING]

    def project(w):
        return jnp.dot(h_scr[...], w, preferred_element_type=F32).astype(BF16)

    @pl.when(j == 0)
    def _():
        x = x_ref[...]
        ada = ada_ref[0]
        y = x * lax.rsqrt(jnp.mean(x * x, axis=-1, keepdims=True) + EPS) * nw_ref[...]
        h = (y * (1.0 + ada[1:2, :]) + ada[0:1, :]).astype(BF16)
        h_scr[...] = h
        g = jnp.dot(h, wg_ref[...], preferred_element_type=F32) + gb_ref[...]
        g_ref[...] = g
        gt_ref[...] = g.T[:GATE_COLS, :]

    @pl.when(j == 0)
    def _():
        h = h_scr[...]
        pair = 2 * ATT_HEAD_DIM
        for t in (COL_QA, COL_KA):
            w = hnw_ref[t:t + 1, :]
            for c0 in range(0, tn, pair):
                acc = jnp.dot(h, wa_ref[:, t * tn + c0:t * tn + c0 + pair], preferred_element_type=F32)
                for c in range(c0, c0 + pair, ATT_HEAD_DIM):
                    a = acc[:, c - c0:c - c0 + ATT_HEAD_DIM]
                    r = lax.rsqrt(jnp.mean(a * a, axis=-1, keepdims=True) + EPS)
                    p_ref[:, t * tn + c:t * tn + c + ATT_HEAD_DIM] = (
                        a * r * w[:, c:c + ATT_HEAD_DIM]).astype(BF16)

    @pl.when(jnp.logical_and(j > 0, j < wa_steps))
    def _():
        p_ref[...] = project(wa_ref[...])

    @pl.when(j == wa_steps)
    def _():
        p_ref[:, :tn] = project(wa_ref[:, :tn])
        p_ref[:, tn:] = project(wb_ref[:, :tn])

    @pl.when(j == wa_steps + 1)
    def _():
        p_ref[:, :tn] = project(wb_ref[:, tn:])
        p_ref[:, tn:] = jnp.zeros((p_ref.shape[0], tn), BF16)


def _inproj_call(x2, ada3, norm_w, wa, wb, wg, gb, hnw, seq):
    T, D = x2.shape
    tm, tn = PROJ_TM, PROJ_TN
    ts = PROJ_STEP_TILES * tn
    tiles_per_seq = seq // tm
    wa_steps = N_HEAD_TILES // PROJ_STEP_TILES
    n_steps = wa_steps + 2
    assert PROJ_STEP_TILES == 2 and N_HEAD_TILES % 2 == 1 and N_COL_TILES - N_HEAD_TILES == 2
    return pl.pallas_call(
        _inproj_kernel,
        grid=(T // tm, n_steps),
        in_specs=[pl.BlockSpec((tm, D), lambda i, j: (i, 0)),
                  pl.BlockSpec((1, 3, D), lambda i, j: (i // tiles_per_seq, 0, 0)),
                  pl.BlockSpec((1, D), lambda i, j: (0, 0)),
                  pl.BlockSpec(memory_space=pl.ANY),
                  pl.BlockSpec((D, ts), lambda i, j: (0, 0), pipeline_mode=pl.Buffered(1)),
                  pl.BlockSpec((D, LANES), lambda i, j: (0, 0)),
                  pl.BlockSpec((1, LANES), lambda i, j: (0, 0)),
                  pl.BlockSpec((2, tn), lambda i, j: (0, 0))],
        out_specs=[pl.BlockSpec((tm, ts), lambda i, j: (i, j)),
                   pl.BlockSpec((tm, LANES), lambda i, j: (i, 0)),
                   pl.BlockSpec((GATE_COLS, tm), lambda i, j: (0, i))],
        out_shape=[jax.ShapeDtypeStruct((T, n_steps * ts), BF16),
                   jax.ShapeDtypeStruct((T, LANES), F32),
                   jax.ShapeDtypeStruct((GATE_COLS, T), F32)],
        scratch_shapes=[pltpu.VMEM((tm, D), BF16),
                        pltpu.VMEM((WA_RING, D, ts), BF16),
                        pltpu.SemaphoreType.DMA((WA_RING,))],
        compiler_params=pltpu.CompilerParams(dimension_semantics=("arbitrary", "arbitrary"),
                                             vmem_limit_bytes=VMEM_LIMIT),
        name="inproj",
    )(x2, ada3, norm_w, wa, wb, wg, gb, hnw)


def _moba_kernel(qa_ref, qb_ref, k_ref, v_ref, zaa_ref, zab_ref, bown_ref, bprev_ref, o_ref,
                 kmean_f32_scr, kmean_scr, vt_scr, q2_scr, selb_scr, s_scr, m8_scr):
    step = pl.program_id(2)
    blk, d = MOBA_BLOCK, ATT_HEAD_DIM
    da = d + BF16_SUBLANES
    nb = k_ref.shape[0] // blk
    heads = qa_ref.shape[1] // d
    tile = (step, nb - 1 - step)
    n_far = (jnp.maximum(step - 1, 0), nb - 2 - step)
    near_slots = 4
    far_slots = nb - near_slots + 1

    def hsl(hd):
        return slice(hd * d, (hd + 1) * d)

    @pl.when(step == 0)
    def _():
        for n in range(nb):
            kb = k_ref[n * blk:(n + 1) * blk, :].astype(F32)
            kmean_f32_scr[n:n + 1, :] = jnp.mean(kb, axis=0, keepdims=True)
            for hd in range(heads):
                vt_scr[n, hd * da:hd * da + d, :] = v_ref[n * blk:(n + 1) * blk, hsl(hd)].T
                vt_scr[n, hd * da + d:(hd + 1) * da, :] = jnp.ones((da - d, blk), BF16)

        rest = kmean_f32_scr[...]
        for c in range(KMEAN_TERMS):
            term = rest.astype(BF16)
            kmean_scr[c * nb:(c + 1) * nb, :] = term
            rest = rest - term.astype(F32)

    q2_scr[0] = qa_ref[...]
    q2_scr[1] = qb_ref[...]

    def fold(x):
        return x.reshape(blk // SUBLANES, SUBLANES, blk)

    def col_max8(s):
        x = fold(s)
        n = x.shape[0]
        while n > 1:
            n //= 2
            x = jnp.maximum(x[:n], x[n:2 * n])
        return x[0]

    rows = lax.broadcasted_iota(jnp.int32, (nb, blk), 0)

    def select_blocks(hd):
        for x in range(2):
            valid = rows < tile[x]
            terms = lax.dot_general(kmean_scr[:, hsl(hd)], q2_scr[x, :, hsl(hd)], _NT,
                                    preferred_element_type=F32)
            gate = terms[:nb]
            for c in range(1, KMEAN_TERMS):
                gate = gate + terms[c * nb:(c + 1) * nb]
            g = jnp.where(valid, gate, -jnp.inf)
            selb = jnp.full(gate.shape, NEG_INF, F32)
            for _ in range(MOBA_TOPK):
                mx = jnp.max(g, axis=0, keepdims=True)
                first = jnp.min(jnp.where(g == mx, rows, nb), axis=0, keepdims=True)
                pick = rows == first
                selb = jnp.where(pick, 0.0, selb)
                g = jnp.where(pick, -jnp.inf, g)
            selb_scr[x, hd] = jnp.where(valid, selb, NEG_INF)

    def near_items(hd):
        first = step == 0
        jb = tile[1] - 1
        ja = jnp.where(first, far_slots, step - 1)
        add_a = jnp.where(first, selb_scr[1, hd, pl.ds(far_slots, 1), :],
                          bprev_ref[hd] + selb_scr[0, hd, pl.ds(jnp.maximum(step - 1, 0), 1), :])
        return [(0, 0, tile[0], bown_ref[hd], None),
                (1, 1, tile[1], bown_ref[hd], None),
                (2, 1, jb, bprev_ref[hd] + selb_scr[1, hd, pl.ds(jb, 1), :], None),
                (3, jnp.where(first, 1, 0), ja, add_a, jnp.logical_not(first))]

    shared_slots = nb // 2 - 2

    def far_items(hd):
        out = []
        for t in range(far_slots):
            if t < shared_slots:
                is_a = t < n_far[0]
                x = jnp.where(is_a, 0, 1)
                j = jnp.where(is_a, t, t - n_far[0])
            else:
                is_a, x, j = None, 1, t - n_far[0]
            out.append((near_slots + t, x, j, selb_scr[x, hd, pl.ds(j, 1), :], is_a))
        return out

    def pass1(hd, items):
        m8 = [None, None]
        for slot, x, j, add, is_a in items:
            kj = k_ref[pl.ds(pl.multiple_of(j * blk, blk), blk), hsl(hd)]
            s = lax.dot_general(kj, q2_scr[x, :, hsl(hd)], _NT, preferred_element_type=F32) + add
            s_scr[hd, slot] = s
            mt = col_max8(s)
            for y in range(2):
                if is_a is None:
                    if y != x:
                        continue
                    part = mt
                else:
                    part = jnp.where(is_a, mt, -jnp.inf) if y == 0 else jnp.where(is_a, -jnp.inf, mt)
                m8[y] = part if m8[y] is None else jnp.maximum(m8[y], part)
        for x in range(2):
            m8_scr[x, hd] = jnp.broadcast_to(jnp.max(m8[x], axis=0, keepdims=True), (SUBLANES, blk))

    def pass2(hd, items):
        acc = [None, None]
        acc_shared = None
        for slot, x, j, _, is_a in items:
            p = jnp.exp2(fold(s_scr[hd, slot]) - m8_scr[x, hd])
            pb = p.reshape(blk, blk).astype(BF16)
            vt = vt_scr[j, hd * da:(hd + 1) * da, :]
            if is_a is None:
                pv = jnp.dot(vt, pb, preferred_element_type=F32)
                acc[x] = pv if acc[x] is None else acc[x] + pv
            else:
                zero = jnp.zeros_like(vt)
                lhs = jnp.concatenate([jnp.where(is_a, vt, zero), jnp.where(is_a, zero, vt)], axis=0)
                pv = jnp.dot(lhs, pb, preferred_element_type=F32)
                acc_shared = pv if acc_shared is None else acc_shared + pv
        for x, za_ref in enumerate((zaa_ref, zab_ref)):
            tot = acc[x] + acc_shared[x * da:(x + 1) * da]
            y = (tot[:d] * (1.0 / tot[d:d + 1])).T
            za = za_ref[:, hsl(hd)].astype(F32)
            o_ref[0, x, 0, :, hsl(hd)] = (y * (za * _sigmoid(za))).astype(BF16)

    items = []
    for hd in range(heads):
        select_blocks(hd)
        items.append(near_items(hd) + far_items(hd))
        pass1(hd, items[hd])
        if hd:
            pass2(hd - 1, items[hd - 1])
    pass2(heads - 1, items[heads - 1])


def _moba_call(p, bias_own, bias_prev, batch, seq):
    blk, d = MOBA_BLOCK, ATT_HEAD_DIM
    nq = seq // blk
    hg = MOBA_HEAD_GROUP
    w = hg * d
    gpt = PROJ_TN // w
    first = lambda col: (lambda b, h, i: (b * nq + i, col * gpt + h))
    last = lambda col: (lambda b, h, i: (b * nq + nq - 1 - i, col * gpt + h))
    return pl.pallas_call(
        _moba_kernel,
        grid=(batch, ATT_HEADS // hg, nq // 2),
        in_specs=[pl.BlockSpec((blk, w), first(COL_QA)),
                  pl.BlockSpec((blk, w), last(COL_QA)),
                  pl.BlockSpec((seq, w), lambda b, h, i: (b, COL_KA * gpt + h)),
                  pl.BlockSpec((seq, w), lambda b, h, i: (b, COL_VA * gpt + h)),
                  pl.BlockSpec((blk, w), first(COL_ZA)),
                  pl.BlockSpec((blk, w), last(COL_ZA)),
                  pl.BlockSpec((hg, blk, blk), lambda b, h, i: (h, 0, 0)),
                  pl.BlockSpec((hg, blk, blk), lambda b, h, i: (h, 0, 0))],
        out_specs=pl.BlockSpec((1, 2, 1, blk, w), lambda b, h, i: (b, 0, i, 0, h)),
        out_shape=jax.ShapeDtypeStruct((batch, 2, nq // 2, blk, ATT_HEADS * d), BF16),
        scratch_shapes=[pltpu.VMEM((nq, w), F32),
                        pltpu.VMEM((KMEAN_TERMS * nq, w), BF16),
                        pltpu.VMEM((nq, hg * (d + BF16_SUBLANES), blk), BF16),
                        pltpu.VMEM((2, blk, w), BF16),
                        pltpu.VMEM((2, hg, nq, blk), F32),
                        pltpu.VMEM((hg, nq + 1, blk, blk), F32),
                        pltpu.VMEM((2, hg, SUBLANES, blk), F32)],
        compiler_params=pltpu.CompilerParams(
            dimension_semantics=("arbitrary", "arbitrary", "arbitrary"),
            vmem_limit_bytes=VMEM_LIMIT),
        name="moba",
    )(p, p, p, p, p, p, bias_own, bias_prev)


def _ua_tile_row(t, nq):
    half = nq // 2
    return jnp.where(t < half, t, half + (nq - 1 - t))


def _mlstm_kernel(qm_ref, km_ref, vm_ref, zm_ref, om_ref, g_ref, gt_ref, cw_ref, cb_ref, mlw_ref,
                  o_ref, c_scr, n_scr, m_scr, tailq_scr, tailk_scr, shift_scr):
    c_idx = pl.program_id(1)
    L = qm_ref.shape[0]
    W = ML_HEADS * ML_HEAD_DIM
    pad = SUBLANES

    r_i = lax.broadcasted_iota(jnp.int32, (L, L), 0)
    c_i = lax.broadcasted_iota(jnp.int32, (L, L), 1)

    @pl.when(c_idx == 0)
    def _():
        c_scr[...] = jnp.zeros_like(c_scr)
        n_scr[...] = jnp.zeros_like(n_scr)
        m_scr[...] = jnp.zeros_like(m_scr)
        tailq_scr[...] = jnp.zeros_like(tailq_scr)
        tailk_scr[...] = jnp.zeros_like(tailk_scr)
        for k in range(CONV_WIDTH):
            shift_scr[k] = (c_i == r_i - k).astype(F32).astype(BF16)

    row8 = lax.broadcasted_iota(jnp.int32, (pad, W), 0)

    def conv_silu(src_ref, tail_scr, col0):
        x = src_ref[...]
        tail = tail_scr[...]
        y = cb_ref[:, col0:col0 + W]
        fix = jnp.zeros((pad, W), F32)
        for w in range(CONV_WIDTH):
            k = CONV_WIDTH - 1 - w
            cw = cw_ref[w:w + 1, col0:col0 + W]
            y = y + cw * jnp.dot(shift_scr[k], x, preferred_element_type=F32)
            if k:
                fix = fix + cw * jnp.where(row8 < k, pltpu.roll(tail, k, axis=0), 0.0)
        tail_scr[...] = x[L - 2 * pad:, :].astype(F32)[pad:, :]
        y = jnp.concatenate([y[:pad] + fix, y[pad:]], axis=0)
        return y * _sigmoid(y)

    qc = conv_silu(qm_ref, tailq_scr, 0)
    kc = conv_silu(km_ref, tailk_scr, W) * (ML_HEAD_DIM ** -0.5)

    causal = c_i <= r_i
    ltri = causal.astype(F32)
    utri = (r_i <= c_i).astype(F32)
    g_col = g_ref[...]
    g_row = gt_ref[...]
    b_col_all = jnp.dot(ltri, _log_sigmoid(g_col), precision=lax.Precision.HIGHEST,
                        preferred_element_type=F32)
    b_row_all = jnp.dot(_log_sigmoid(g_row), utri, precision=lax.Precision.HIGHEST,
                        preferred_element_type=F32)

    for hh in range(ML_HEADS):
        sl = slice(hh * ML_HEAD_DIM, (hh + 1) * ML_HEAD_DIM)
        q = qc[:, sl]
        k = kc[:, sl]
        v = vm_ref[:, sl]
        b_col = b_col_all[:, ML_HEADS + hh:ML_HEADS + hh + 1]
        a_col = g_col[:, hh:hh + 1] - b_col
        a_row = g_row[hh:hh + 1, :] - b_row_all[ML_HEADS + hh:ML_HEADS + hh + 1, :]
        m_prev = m_scr[hh][0:1, 0:1]

        a_mask = jnp.where(causal, a_row, -jnp.inf)
        gcol = jnp.maximum(m_prev, jnp.max(a_mask, axis=1, keepdims=True))
        d_mat = jnp.exp(a_mask - gcol)
        inter = jnp.exp(m_prev - gcol)
        g_last = jnp.max(gcol, axis=0, keepdims=True)

        qb = q.astype(BF16)
        s = lax.dot_general(qb, k.astype(BF16), _NT, preferred_element_type=F32) * d_mat
        c_old = c_scr[hh]
        n_old = n_scr[hh]
        num = (inter * jnp.dot(qb, c_old.astype(BF16), preferred_element_type=F32)
               + jnp.dot(s.astype(BF16), v, preferred_element_type=F32))
        den = inter * jnp.sum(q * n_old, axis=1, keepdims=True) + jnp.sum(s, axis=1, keepdims=True)
        h = num / jnp.maximum(jnp.abs(den), jnp.exp(-(b_col + gcol)))

        decay = jnp.exp(m_prev - g_last)
        kw = k * jnp.exp(a_col - g_last)
        c_scr[hh] = decay * c_old + lax.dot_general(kw.astype(BF16), v, _TN,
                                                    preferred_element_type=F32)
        n_scr[hh] = decay * n_old + jnp.sum(kw, axis=0, keepdims=True)
        m_scr[hh] = jnp.broadcast_to(b_col[L - 1:L, :] + g_last, m_scr.shape[1:])

        hg = _sigmoid(om_ref[:, sl].astype(F32)) * h
        hn = hg * lax.rsqrt(jnp.mean(hg * hg, axis=-1, keepdims=True) + EPS) * mlw_ref[:, sl]
        z = zm_ref[:, sl].astype(F32)
        o_ref[:, sl] = (hn * (z * _sigmoid(z))).astype(BF16)


def _mlstm_call(p, g, gt, conv_w, conv_b, ml_norm_w, batch, seq):
    T = p.shape[0]
    L = ML_CHUNK
    nc = seq // L
    W = ML_HEADS * ML_HEAD_DIM
    row = lambda b, c: b * nc + c
    pspec = lambda col: pl.BlockSpec((L, W), lambda b, c: (row(b, c), col))
    return pl.pallas_call(
        _mlstm_kernel,
        grid=(batch, nc),
        in_specs=[pspec(COL_QM), pspec(COL_KM), pspec(COL_VM), pspec(COL_ZM), pspec(COL_OM),
                  pl.BlockSpec((L, LANES), lambda b, c: (row(b, c), 0)),
                  pl.BlockSpec((GATE_COLS, L), lambda b, c: (0, row(b, c))),
                  pl.BlockSpec((CONV_WIDTH, 2 * W), lambda b, c: (0, 0)),
                  pl.BlockSpec((1, 2 * W), lambda b, c: (0, 0)),
                  pl.BlockSpec((1, W), lambda b, c: (0, 0))],
        out_specs=pl.BlockSpec((L, W), lambda b, c: (row(b, c), 0)),
        out_shape=jax.ShapeDtypeStruct((T, W), BF16),
        scratch_shapes=[pltpu.VMEM((ML_HEADS, ML_HEAD_DIM, ML_HEAD_DIM), F32),
                        pltpu.VMEM((ML_HEADS, 1, ML_HEAD_DIM), F32),
                        pltpu.VMEM((ML_HEADS, SUBLANES, LANES), F32),
                        pltpu.VMEM((SUBLANES, W), F32),
                        pltpu.VMEM((SUBLANES, W), F32),
                        pltpu.VMEM((CONV_WIDTH, L, L), BF16)],
        compiler_params=pltpu.CompilerParams(dimension_semantics=("arbitrary", "arbitrary"),
                                             vmem_limit_bytes=VMEM_LIMIT),
        name="mlstm",
    )(p, p, p, p, p, g, gt, conv_w, conv_b, ml_norm_w)


def _outp_kernel(*refs):
    ua_refs, (um_ref, ga_ref, gm_ref, x_ref, ada_ref, wa_ref, wm_ref, wo_ref, o_ref) = refs[:-9], refs[-9:]
    ua = jnp.concatenate([r[...] for r in ua_refs], axis=0)
    ya = jnp.dot(ua, wa_ref[...], preferred_element_type=F32)
    ym = jnp.dot(um_ref[...], wm_ref[...], preferred_element_type=F32)
    y = _sigmoid(ga_ref[...].astype(F32)) * ya + _sigmoid(gm_ref[...].astype(F32)) * ym
    gate = ada_ref[0][2:3, :]
    o_ref[...] = x_ref[...] + gate * jnp.dot(y.astype(BF16), wo_ref[...], preferred_element_type=F32)


def _outp_call(ua, um, p, x2, ada3, wa, wm, wo, seq):
    T, D = x2.shape
    tm = OUT_TM
    steps_per_seq = seq // tm
    blk = MOBA_BLOCK
    nq = seq // blk
    tiles_per_step = tm // blk

    def ua_spec(e):
        def index(i):
            t = (i % steps_per_seq) * tiles_per_step + e
            return ((i // steps_per_seq) * nq + _ua_tile_row(t, nq), 0)
        return pl.BlockSpec((blk, D), index)

    act = lambda col: pl.BlockSpec((tm, D), lambda i: (i, col))
    wspec = pl.BlockSpec((D, D), lambda i: (0, 0))
    return pl.pallas_call(
        _outp_kernel,
        grid=(T // tm,),
        in_specs=[ua_spec(e) for e in range(tiles_per_step)]
                 + [act(0), act(COL_GA), act(COL_GM), act(0),
                    pl.BlockSpec((1, 3, D), lambda i: (i // steps_per_seq, 0, 0)),
                    wspec, wspec, wspec],
        out_specs=pl.BlockSpec((tm, D), lambda i: (i, 0)),
        out_shape=jax.ShapeDtypeStruct((T, D), F32),
        compiler_params=pltpu.CompilerParams(dimension_semantics=("arbitrary",),
                                             vmem_limit_bytes=VMEM_LIMIT),
        name="outp",
    )(*([ua] * tiles_per_step), um, p, p, x2, ada3, wa, wm, wo)


def _rel_bias_tables(rel_bias):
    max_exact = REL_BUCKETS // 2
    rb = rel_bias.astype(F32)
    rb = (rb - rb[REL_BUCKETS - 1:, :]) * LOG2E
    key = jnp.arange(MOBA_BLOCK)[:, None]
    qry = jnp.arange(MOBA_BLOCK)[None, :]
    d_own = qry - key

    def tile(dist):
        nf = jnp.maximum(dist, 1).astype(F32)
        large = max_exact + (jnp.log(nf / max_exact) / math.log(REL_MAX_DIST / max_exact)
                             * (REL_BUCKETS - max_exact)).astype(jnp.int32)
        bucket = jnp.where(dist < max_exact, dist, jnp.minimum(large, REL_BUCKETS - 1))
        onehot = (bucket[..., None] == jnp.arange(REL_BUCKETS)).astype(F32)
        return jnp.einsum('kqb,bh->hkq', onehot, rb, precision=lax.Precision.HIGHEST)

    own = jnp.where(d_own >= 0, tile(jnp.maximum(d_own, 0)), NEG_INF)
    prev = tile(d_own + MOBA_BLOCK)
    return own, prev


def _layer(x2, c, w_ada, b_ada, norm_w, w_in_all, layer, q_norm_w, k_norm_w, rel_bias, conv_w, conv_b,
           b_igate, b_fgate, ml_norm_w, w_att_proj, w_ml_proj, w_out, batch, seq):
    D = D_MODEL
    g0 = 4 * D + 5 * D
    ada3 = _ada_call(c, w_ada, b_ada).reshape(batch, 3, D)

    wa = w_in_all[layer, :, :g0].astype(BF16)
    wa = jnp.pad(wa, ((0, 0), (0, -g0 % (PROJ_STEP_TILES * PROJ_TN))))
    wb = w_in_all[layer, :, g0 + GATE_COLS:].astype(BF16)
    wg = jnp.pad(w_in_all[layer, :, g0:g0 + GATE_COLS], ((0, 0), (0, LANES - GATE_COLS))).astype(BF16)
    gb = jnp.pad(jnp.concatenate([b_igate, b_fgate]), (0, LANES - GATE_COLS)).reshape(1, LANES)
    q_scale = ATT_HEAD_DIM ** -0.5 * LOG2E
    hnw = jnp.stack([jnp.tile(q_norm_w.astype(F32), ATT_HEADS) * q_scale,
                     jnp.tile(k_norm_w.astype(F32), ATT_HEADS)])

    p, g, gt = _inproj_call(x2, ada3, norm_w.reshape(1, D), wa, wb, wg, gb, hnw, seq)
    bias_own, bias_prev = _rel_bias_tables(rel_bias)
    ua = _moba_call(p, bias_own, bias_prev, batch, seq).reshape(batch * seq, D)
    um = _mlstm_call(p, g, gt, conv_w, conv_b.reshape(1, -1), ml_norm_w.reshape(1, -1), batch, seq)
    return _outp_call(ua, um, p, x2, ada3, w_att_proj.astype(BF16), w_ml_proj.astype(BF16),
                      w_out.astype(BF16), seq)


def kernel(x, c, w_ada, b_ada, norm_w, w_in, q_norm_w, k_norm_w, rel_bias, conv_w, conv_b, b_igate,
           b_fgate, ml_norm_w, w_att_proj, w_ml_proj, w_out):
    batch, seq, D = x.shape
    assert D == D_MODEL and seq % PROJ_TM == 0 and seq % MOBA_BLOCK == 0 and seq % ML_CHUNK == 0
    x2 = x.reshape(batch * seq, D)
    for l in range(w_in.shape[0]):
        x2 = _layer(x2, c, w_ada[l], b_ada[l], norm_w[l], w_in, l, q_norm_w[l], k_norm_w[l],
                    rel_bias, conv_w[l], conv_b[l], b_igate[l], b_fgate[l], ml_norm_w[l],
                    w_att_proj[l], w_ml_proj[l], w_out[l], batch, seq)
    return x2.reshape(batch, seq, D)
```

```python
import math

import jax
import jax.numpy as jnp
from jax import lax
from jax.experimental import pallas as pl
from jax.experimental.pallas import tpu as pltpu

F32 = jnp.float32
BF16 = jnp.bfloat16

D_MODEL = 1024
ATT_HEADS = 8
ATT_HEAD_DIM = D_MODEL // ATT_HEADS
MOBA_BLOCK = 256
MOBA_TOPK = 3
REL_BUCKETS = 32
REL_MAX_DIST = 128
ML_HEADS = 4
ML_HEAD_DIM = D_MODEL // ML_HEADS
CONV_WIDTH = 4
EPS = 1e-6
NEG_INF = -1e30
LOG2E = math.log2(math.e)

LANES = 128
SUBLANES = 8
BF16_SUBLANES = 16
GATE_COLS = 2 * ML_HEADS
ML_CHUNK = 256
PROJ_TM = 1024
PROJ_TN = 1024
PROJ_STEP_TILES = 2
WA_RING = 3
OUT_TM = 512
MOBA_HEAD_GROUP = 4
KMEAN_TERMS = 3
VMEM_LIMIT = 52 * 1024 * 1024

COL_QA, COL_KA, COL_VA, COL_ZA, COL_QM, COL_KM, COL_VM, COL_ZM, COL_OM, COL_GA, COL_GM = range(11)
N_COL_TILES = 11
N_HEAD_TILES = 9

_NT = (((1,), (1,)), ((), ()))
_TN = (((0,), (0,)), ((), ()))


def _sigmoid(x):
    return 1.0 / (1.0 + jnp.exp2(x * (-LOG2E)))


def _log_sigmoid(x):
    return jnp.minimum(x, 0.0) - jnp.log1p(jnp.exp(-jnp.abs(x)))


def _ada_kernel(c_ref, w_ref, b_ref, o_ref):
    o_ref[...] = jnp.dot(c_ref[...], w_ref[...], precision=lax.Precision.HIGHEST,
                         preferred_element_type=F32) + b_ref[...]


def _ada_call(c, w_ada, b_ada):
    B, D = c.shape
    return pl.pallas_call(
        _ada_kernel,
        grid=(3,),
        in_specs=[pl.BlockSpec((B, D), lambda j: (0, 0)),
                  pl.BlockSpec((D, D), lambda j: (0, j)),
                  pl.BlockSpec((1, D), lambda j: (0, j))],
        out_specs=pl.BlockSpec((B, D), lambda j: (0, j)),
        out_shape=jax.ShapeDtypeStruct((B, 3 * D), F32),
        compiler_params=pltpu.CompilerParams(dimension_semantics=("arbitrary",),
                                             vmem_limit_bytes=VMEM_LIMIT),
        name="ada",
    )(c, w_ada, b_ada.reshape(1, 3 * D))


def _inproj_kernel(x_hbm, ada_ref, nw_ref, wa_hbm, wb_ref, wg_ref, gb_ref, hnw_ref,
                   p_ref, g_ref, gt_ref, h_scr, wa_ring, wa_sems, x_ring, x_sems):
    i = pl.program_id(0)
    j = pl.program_id(1)
    n_i = pl.num_programs(0)
    n_j = pl.num_programs(1)
    tn = PROJ_TN
    ts = PROJ_STEP_TILES * tn
    wa_steps = N_HEAD_TILES // PROJ_STEP_TILES
    wa_chunks = wa_steps + 1

    def wa_copy(ii, jj):
        slot = (ii * wa_chunks + jj) % WA_RING
        return pltpu.make_async_copy(wa_hbm.at[:, pl.ds(pl.multiple_of(jj * ts, ts), ts)],
                                     wa_ring.at[slot], wa_sems.at[slot])

    step = i * n_j + j

    @pl.when(step == 0)
    def _():
        for jj in range(WA_RING - 1):
            wa_copy(0, jj).start()

    ahead = step + WA_RING - 1
    i_ahead, j_ahead = ahead // n_j, ahead % n_j

    @pl.when(jnp.logical_and(j_ahead < wa_chunks, i_ahead < n_i))
    def _():
        wa_copy(i_ahead, j_ahead).start()

    @pl.when(j < wa_chunks)
    def _():
        wa_copy(i, j).wait()

    wa_ref = wa_ring.at[(i * wa_chunks + j) % WA_RING]

    tm = x_ring.shape[1]

    def x_copy(ii):
        slot = ii % 2
        return pltpu.make_async_copy(x_hbm.at[pl.ds(pl.multiple_of(ii * tm, tm), tm), :],
                                     x_ring.at[slot], x_sems.at[slot])

    @pl.when(step == 0)
    def _():
        x_copy(0).start()

    @pl.when(jnp.logical_and(j == 1, i + 1 < n_i))
    def _():
        x_copy(i + 1).start()

    def project(w):
        return jnp.dot(h_scr[...], w, preferred_element_type=F32).astype(BF16)

    @pl.when(j == 0)
    def _():
        x_copy(i).wait()
        x = x_ring[i % 2]
        ada = ada_ref[0]
        y = x * lax.rsqrt(jnp.mean(x * x, axis=-1, keepdims=True) + EPS) * nw_ref[...]
        h = (y * (1.0 + ada[1:2, :]) + ada[0:1, :]).astype(BF16)
        h_scr[...] = h
        g = jnp.dot(h, wg_ref[...], preferred_element_type=F32) + gb_ref[...]
        g_ref[...] = g
        gt_ref[...] = g.T[:GATE_COLS, :]

    @pl.when(j == 0)
    def _():
        h = h_scr[...]
        pair = 2 * ATT_HEAD_DIM
        for t in (COL_QA, COL_KA):
            w = hnw_ref[t:t + 1, :]
            for c0 in range(0, tn, pair):
                acc = jnp.dot(h, wa_ref[:, t * tn + c0:t * tn + c0 + pair], preferred_element_type=F32)
                for c in range(c0, c0 + pair, ATT_HEAD_DIM):
                    a = acc[:, c - c0:c - c0 + ATT_HEAD_DIM]
                    r = lax.rsqrt(jnp.mean(a * a, axis=-1, keepdims=True) + EPS)
                    p_ref[:, t * tn + c:t * tn + c + ATT_HEAD_DIM] = (
                        a * r * w[:, c:c + ATT_HEAD_DIM]).astype(BF16)

    @pl.when(jnp.logical_and(j > 0, j < wa_steps))
    def _():
        p_ref[...] = project(wa_ref[...])

    @pl.when(j == wa_steps)
    def _():
        p_ref[:, :tn] = project(wa_ref[:, :tn])
        p_ref[:, tn:] = project(wb_ref[:, :tn])

    @pl.when(j == wa_steps + 1)
    def _():
        p_ref[:, :tn] = project(wb_ref[:, tn:])
        p_ref[:, tn:] = jnp.zeros((p_ref.shape[0], tn), BF16)


def _inproj_call(x2, ada3, norm_w, wa, wb, wg, gb, hnw, seq):
    T, D = x2.shape
    tm, tn = PROJ_TM, PROJ_TN
    ts = PROJ_STEP_TILES * tn
    tiles_per_seq = seq // tm
    wa_steps = N_HEAD_TILES // PROJ_STEP_TILES
    n_steps = wa_steps + 2
    assert PROJ_STEP_TILES == 2 and N_HEAD_TILES % 2 == 1 and N_COL_TILES - N_HEAD_TILES == 2
    return pl.pallas_call(
        _inproj_kernel,
        grid=(T // tm, n_steps),
        in_specs=[pl.BlockSpec(memory_space=pl.ANY),
                  pl.BlockSpec((1, 3, D), lambda i, j: (i // tiles_per_seq, 0, 0)),
                  pl.BlockSpec((1, D), lambda i, j: (0, 0)),
                  pl.BlockSpec(memory_space=pl.ANY),
                  pl.BlockSpec((D, ts), lambda i, j: (0, 0), pipeline_mode=pl.Buffered(1)),
                  pl.BlockSpec((D, LANES), lambda i, j: (0, 0)),
                  pl.BlockSpec((1, LANES), lambda i, j: (0, 0)),
                  pl.BlockSpec((2, tn), lambda i, j: (0, 0))],
        out_specs=[pl.BlockSpec((tm, ts), lambda i, j: (i, j)),
                   pl.BlockSpec((tm, LANES), lambda i, j: (i, 0)),
                   pl.BlockSpec((GATE_COLS, tm), lambda i, j: (0, i))],
        out_shape=[jax.ShapeDtypeStruct((T, n_steps * ts), BF16),
                   jax.ShapeDtypeStruct((T, LANES), F32),
                   jax.ShapeDtypeStruct((GATE_COLS, T), F32)],
        scratch_shapes=[pltpu.VMEM((tm, D), BF16),
                        pltpu.VMEM((WA_RING, D, ts), BF16),
                        pltpu.SemaphoreType.DMA((WA_RING,)),
                        pltpu.VMEM((2, tm, D), F32),
                        pltpu.SemaphoreType.DMA((2,))],
        compiler_params=pltpu.CompilerParams(dimension_semantics=("arbitrary", "arbitrary"),
                                             vmem_limit_bytes=VMEM_LIMIT),
        name="inproj",
    )(x2, ada3, norm_w, wa, wb, wg, gb, hnw)


def _moba_kernel(qa_ref, qb_ref, k_ref, v_ref, zaa_ref, zab_ref, bown_ref, bprev_ref, o_ref,
                 kmean_f32_scr, kmean_scr, vt_scr, q2_scr, selb_scr, s_scr, m8_scr):
    step = pl.program_id(2)
    blk, d = MOBA_BLOCK, ATT_HEAD_DIM
    da = d + BF16_SUBLANES
    nb = k_ref.shape[0] // blk
    heads = qa_ref.shape[1] // d
    tile = (step, nb - 1 - step)
    n_far = (jnp.maximum(step - 1, 0), nb - 2 - step)
    near_slots = 4
    far_slots = nb - near_slots + 1

    def hsl(hd):
        return slice(hd * d, (hd + 1) * d)

    @pl.when(step == 0)
    def _():
        for n in range(nb):
            kb = k_ref[n * blk:(n + 1) * blk, :].astype(F32)
            kmean_f32_scr[n:n + 1, :] = jnp.mean(kb, axis=0, keepdims=True)
            for hd in range(heads):
                vt_scr[n, hd * da:hd * da + d, :] = v_ref[n * blk:(n + 1) * blk, hsl(hd)].T
                vt_scr[n, hd * da + d:(hd + 1) * da, :] = jnp.ones((da - d, blk), BF16)

        rest = kmean_f32_scr[...]
        for c in range(KMEAN_TERMS):
            term = rest.astype(BF16)
            kmean_scr[c * nb:(c + 1) * nb, :] = term
            rest = rest - term.astype(F32)

    q2_scr[0] = qa_ref[...]
    q2_scr[1] = qb_ref[...]

    def fold(x):
        return x.reshape(blk // SUBLANES, SUBLANES, blk)

    def col_max8(s):
        x = fold(s)
        n = x.shape[0]
        while n > 1:
            n //= 2
            x = jnp.maximum(x[:n], x[n:2 * n])
        return x[0]

    rows = lax.broadcasted_iota(jnp.int32, (nb, blk), 0)

    def select_blocks(hd):
        for x in range(2):
            valid = rows < tile[x]
            terms = lax.dot_general(kmean_scr[:, hsl(hd)], q2_scr[x, :, hsl(hd)], _NT,
                                    preferred_element_type=F32)
            gate = terms[:nb]
            for c in range(1, KMEAN_TERMS):
                gate = gate + terms[c * nb:(c + 1) * nb]
            g = jnp.where(valid, gate, -jnp.inf)
            selb = jnp.full(gate.shape, NEG_INF, F32)
            for _ in range(MOBA_TOPK):
                mx = jnp.max(g, axis=0, keepdims=True)
                first = jnp.min(jnp.where(g == mx, rows, nb), axis=0, keepdims=True)
                pick = rows == first
                selb = jnp.where(pick, 0.0, selb)
                g = jnp.where(pick, -jnp.inf, g)
            selb_scr[x, hd] = jnp.where(valid, selb, NEG_INF)

    def near_items(hd):
        first = step == 0
        jb = tile[1] - 1
        ja = jnp.where(first, far_slots, step - 1)
        add_a = jnp.where(first, selb_scr[1, hd, pl.ds(far_slots, 1), :],
                          bprev_ref[hd] + selb_scr[0, hd, pl.ds(jnp.maximum(step - 1, 0), 1), :])
        return [(0, 0, tile[0], bown_ref[hd], None),
                (1, 1, tile[1], bown_ref[hd], None),
                (2, 1, jb, bprev_ref[hd] + selb_scr[1, hd, pl.ds(jb, 1), :], None),
                (3, jnp.where(first, 1, 0), ja, add_a, jnp.logical_not(first))]

    shared_slots = nb // 2 - 2

    def far_items(hd):
        out = []
        for t in range(far_slots):
            if t < shared_slots:
                is_a = t < n_far[0]
                x = jnp.where(is_a, 0, 1)
                j = jnp.where(is_a, t, t - n_far[0])
            else:
                is_a, x, j = None, 1, t - n_far[0]
            out.append((near_slots + t, x, j, selb_scr[x, hd, pl.ds(j, 1), :], is_a))
        return out

    def pass1(hd, items):
        m8 = [None, None]
        for slot, x, j, add, is_a in items:
            kj = k_ref[pl.ds(pl.multiple_of(j * blk, blk), blk), hsl(hd)]
            s = lax.dot_general(kj, q2_scr[x, :, hsl(hd)], _NT, preferred_element_type=F32) + add
            s_scr[hd, slot] = s
            mt = col_max8(s)
            for y in range(2):
                if is_a is None:
                    if y != x:
                        continue
                    part = mt
                else:
                    part = jnp.where(is_a, mt, -jnp.inf) if y == 0 else jnp.where(is_a, -jnp.inf, mt)
                m8[y] = part if m8[y] is None else jnp.maximum(m8[y], part)
        for x in range(2):
            m8_scr[x, hd] = jnp.broadcast_to(jnp.max(m8[x], axis=0, keepdims=True), (SUBLANES, blk))

    def pass2(hd, items):
        acc = [None, None]
        acc_shared = None
        for slot, x, j, _, is_a in items:
            p = jnp.exp2(fold(s_scr[hd, slot]) - m8_scr[x, hd])
            pb = p.reshape(blk, blk).astype(BF16)
            vt = vt_scr[j, hd * da:(hd + 1) * da, :]
            if is_a is None:
                pv = jnp.dot(vt, pb, preferred_element_type=F32)
                acc[x] = pv if acc[x] is None else acc[x] + pv
            else:
                zero = jnp.zeros_like(vt)
                lhs = jnp.concatenate([jnp.where(is_a, vt, zero), jnp.where(is_a, zero, vt)], axis=0)
                pv = jnp.dot(lhs, pb, preferred_element_type=F32)
                acc_shared = pv if acc_shared is None else acc_shared + pv
        for x, za_ref in enumerate((zaa_ref, zab_ref)):
            tot = acc[x] + acc_shared[x * da:(x + 1) * da]
            y = (tot[:d] * (1.0 / tot[d:d + 1])).T
            za = za_ref[:, hsl(hd)].astype(F32)
            o_ref[0, x, 0, :, hsl(hd)] = (y * (za * _sigmoid(za))).astype(BF16)

    items = []
    for hd in range(heads):
        select_blocks(hd)
        items.append(near_items(hd) + far_items(hd))
        pass1(hd, items[hd])
        if hd:
            pass2(hd - 1, items[hd - 1])
    pass2(heads - 1, items[heads - 1])


def _moba_call(p, bias_own, bias_prev, batch, seq):
    blk, d = MOBA_BLOCK, ATT_HEAD_DIM
    nq = seq // blk
    hg = MOBA_HEAD_GROUP
    w = hg * d
    gpt = PROJ_TN // w
    first = lambda col: (lambda b, h, i: (b * nq + i, col * gpt + h))
    last = lambda col: (lambda b, h, i: (b * nq + nq - 1 - i, col * gpt + h))
    return pl.pallas_call(
        _moba_kernel,
        grid=(batch, ATT_HEADS // hg, nq // 2),
        in_specs=[pl.BlockSpec((blk, w), first(COL_QA)),
                  pl.BlockSpec((blk, w), last(COL_QA)),
                  pl.BlockSpec((seq, w), lambda b, h, i: (b, COL_KA * gpt + h)),
                  pl.BlockSpec((seq, w), lambda b, h, i: (b, COL_VA * gpt + h)),
                  pl.BlockSpec((blk, w), first(COL_ZA)),
                  pl.BlockSpec((blk, w), last(COL_ZA)),
                  pl.BlockSpec((hg, blk, blk), lambda b, h, i: (h, 0, 0)),
                  pl.BlockSpec((hg, blk, blk), lambda b, h, i: (h, 0, 0))],
        out_specs=pl.BlockSpec((1, 2, 1, blk, w), lambda b, h, i: (b, 0, i, 0, h)),
        out_shape=jax.ShapeDtypeStruct((batch, 2, nq // 2, blk, ATT_HEADS * d), BF16),
        scratch_shapes=[pltpu.VMEM((nq, w), F32),
                        pltpu.VMEM((KMEAN_TERMS * nq, w), BF16),
                        pltpu.VMEM((nq, hg * (d + BF16_SUBLANES), blk), BF16),
                        pltpu.VMEM((2, blk, w), BF16),
                        pltpu.VMEM((2, hg, nq, blk), F32),
                        pltpu.VMEM((hg, nq + 1, blk, blk), F32),
                        pltpu.VMEM((2, hg, SUBLANES, blk), F32)],
        compiler_params=pltpu.CompilerParams(
            dimension_semantics=("arbitrary", "arbitrary", "arbitrary"),
            vmem_limit_bytes=VMEM_LIMIT),
        name="moba",
    )(p, p, p, p, p, p, bias_own, bias_prev)


def _ua_tile_row(t, nq):
    half = nq // 2
    return jnp.where(t < half, t, half + (nq - 1 - t))


def _mlstm_kernel(qm_ref, km_ref, vm_ref, zm_ref, om_ref, g_ref, gt_ref, cw_ref, cb_ref, mlw_ref,
                  o_ref, c_scr, n_scr, m_scr, tailq_scr, tailk_scr, shift_scr):
    c_idx = pl.program_id(1)
    L = qm_ref.shape[0]
    W = ML_HEADS * ML_HEAD_DIM
    pad = SUBLANES

    r_i = lax.broadcasted_iota(jnp.int32, (L, L), 0)
    c_i = lax.broadcasted_iota(jnp.int32, (L, L), 1)

    @pl.when(c_idx == 0)
    def _():
        c_scr[...] = jnp.zeros_like(c_scr)
        n_scr[...] = jnp.zeros_like(n_scr)
        m_scr[...] = jnp.zeros_like(m_scr)
        tailq_scr[...] = jnp.zeros_like(tailq_scr)
        tailk_scr[...] = jnp.zeros_like(tailk_scr)
        for k in range(CONV_WIDTH):
            shift_scr[k] = (c_i == r_i - k).astype(F32).astype(BF16)

    row8 = lax.broadcasted_iota(jnp.int32, (pad, W), 0)

    def conv_silu(src_ref, tail_scr, col0):
        x = src_ref[...]
        tail = tail_scr[...]
        y = cb_ref[:, col0:col0 + W]
        fix = jnp.zeros((pad, W), F32)
        for w in range(CONV_WIDTH):
            k = CONV_WIDTH - 1 - w
            cw = cw_ref[w:w + 1, col0:col0 + W]
            y = y + cw * jnp.dot(shift_scr[k], x, preferred_element_type=F32)
            if k:
                fix = fix + cw * jnp.where(row8 < k, pltpu.roll(tail, k, axis=0), 0.0)
        tail_scr[...] = x[L - 2 * pad:, :].astype(F32)[pad:, :]
        y = jnp.concatenate([y[:pad] + fix, y[pad:]], axis=0)
        return y * _sigmoid(y)

    qc = conv_silu(qm_ref, tailq_scr, 0)
    kc = conv_silu(km_ref, tailk_scr, W) * (ML_HEAD_DIM ** -0.5)

    causal = c_i <= r_i
    ltri = causal.astype(F32)
    utri = (r_i <= c_i).astype(F32)
    g_col = g_ref[...]
    g_row = gt_ref[...]
    b_col_all = jnp.dot(ltri, _log_sigmoid(g_col), precision=lax.Precision.HIGHEST,
                        preferred_element_type=F32)
    b_row_all = jnp.dot(_log_sigmoid(g_row), utri, precision=lax.Precision.HIGHEST,
                        preferred_element_type=F32)

    for hh in range(ML_HEADS):
        sl = slice(hh * ML_HEAD_DIM, (hh + 1) * ML_HEAD_DIM)
        q = qc[:, sl]
        k = kc[:, sl]
        v = vm_ref[:, sl]
        b_col = b_col_all[:, ML_HEADS + hh:ML_HEADS + hh + 1]
        a_col = g_col[:, hh:hh + 1] - b_col
        a_row = g_row[hh:hh + 1, :] - b_row_all[ML_HEADS + hh:ML_HEADS + hh + 1, :]
        m_prev = m_scr[hh][0:1, 0:1]

        a_mask = jnp.where(causal, a_row, -jnp.inf)
        gcol = jnp.maximum(m_prev, jnp.max(a_mask, axis=1, keepdims=True))
        d_mat = jnp.exp(a_mask - gcol)
        inter = jnp.exp(m_prev - gcol)
        g_last = jnp.max(gcol, axis=0, keepdims=True)

        qb = q.astype(BF16)
        s = lax.dot_general(qb, k.astype(BF16), _NT, preferred_element_type=F32) * d_mat
        c_old = c_scr[hh]
        n_old = n_scr[hh]
        num = (inter * jnp.dot(qb, c_old.astype(BF16), preferred_element_type=F32)
               + jnp.dot(s.astype(BF16), v, preferred_element_type=F32))
        den = inter * jnp.sum(q * n_old, axis=1, keepdims=True) + jnp.sum(s, axis=1, keepdims=True)
        h = num / jnp.maximum(jnp.abs(den), jnp.exp(-(b_col + gcol)))

        decay = jnp.exp(m_prev - g_last)
        kw = k * jnp.exp(a_col - g_last)
        c_scr[hh] = decay * c_old + lax.dot_general(kw.astype(BF16), v, _TN,
                                                    preferred_element_type=F32)
        n_scr[hh] = decay * n_old + jnp.sum(kw, axis=0, keepdims=True)
        m_scr[hh] = jnp.broadcast_to(b_col[L - 1:L, :] + g_last, m_scr.shape[1:])

        hg = _sigmoid(om_ref[:, sl].astype(F32)) * h
        hn = hg * lax.rsqrt(jnp.mean(hg * hg, axis=-1, keepdims=True) + EPS) * mlw_ref[:, sl]
        z = zm_ref[:, sl].astype(F32)
        o_ref[:, sl] = (hn * (z * _sigmoid(z))).astype(BF16)


def _mlstm_call(p, g, gt, conv_w, conv_b, ml_norm_w, batch, seq):
    T = p.shape[0]
    L = ML_CHUNK
    nc = seq // L
    W = ML_HEADS * ML_HEAD_DIM
    row = lambda b, c: b * nc + c
    pspec = lambda col: pl.BlockSpec((L, W), lambda b, c: (row(b, c), col))
    return pl.pallas_call(
        _mlstm_kernel,
        grid=(batch, nc),
        in_specs=[pspec(COL_QM), pspec(COL_KM), pspec(COL_VM), pspec(COL_ZM), pspec(COL_OM),
                  pl.BlockSpec((L, LANES), lambda b, c: (row(b, c), 0)),
                  pl.BlockSpec((GATE_COLS, L), lambda b, c: (0, row(b, c))),
                  pl.BlockSpec((CONV_WIDTH, 2 * W), lambda b, c: (0, 0)),
                  pl.BlockSpec((1, 2 * W), lambda b, c: (0, 0)),
                  pl.BlockSpec((1, W), lambda b, c: (0, 0))],
        out_specs=pl.BlockSpec((L, W), lambda b, c: (row(b, c), 0)),
        out_shape=jax.ShapeDtypeStruct((T, W), BF16),
        scratch_shapes=[pltpu.VMEM((ML_HEADS, ML_HEAD_DIM, ML_HEAD_DIM), F32),
                        pltpu.VMEM((ML_HEADS, 1, ML_HEAD_DIM), F32),
                        pltpu.VMEM((ML_HEADS, SUBLANES, LANES), F32),
                        pltpu.VMEM((SUBLANES, W), F32),
                        pltpu.VMEM((SUBLANES, W), F32),
                        pltpu.VMEM((CONV_WIDTH, L, L), BF16)],
        compiler_params=pltpu.CompilerParams(dimension_semantics=("arbitrary", "arbitrary"),
                                             vmem_limit_bytes=VMEM_LIMIT),
        name="mlstm",
    )(p, p, p, p, p, g, gt, conv_w, conv_b, ml_norm_w)


def _outp_kernel(*refs):
    ua_refs, (um_ref, ga_ref, gm_ref, x_ref, ada_ref, wa_ref, wm_ref, wo_ref, o_ref) = refs[:-9], refs[-9:]
    ua = jnp.concatenate([r[...] for r in ua_refs], axis=0)
    ya = jnp.dot(ua, wa_ref[...], preferred_element_type=F32)
    ym = jnp.dot(um_ref[...], wm_ref[...], preferred_element_type=F32)
    y = _sigmoid(ga_ref[...].astype(F32)) * ya + _sigmoid(gm_ref[...].astype(F32)) * ym
    gate = ada_ref[0][2:3, :]
    o_ref[...] = x_ref[...] + gate * jnp.dot(y.astype(BF16), wo_ref[...], preferred_element_type=F32)


def _outp_call(ua, um, p, x2, ada3, wa, wm, wo, seq):
    T, D = x2.shape
    tm = OUT_TM
    steps_per_seq = seq // tm
    blk = MOBA_BLOCK
    nq = seq // blk
    tiles_per_step = tm // blk

    def ua_spec(e):
        def index(i):
            t = (i % steps_per_seq) * tiles_per_step + e
            return ((i // steps_per_seq) * nq + _ua_tile_row(t, nq), 0)
        return pl.BlockSpec((blk, D), index)

    act = lambda col: pl.BlockSpec((tm, D), lambda i: (i, col))
    wspec = pl.BlockSpec((D, D), lambda i: (0, 0))
    return pl.pallas_call(
        _outp_kernel,
        grid=(T // tm,),
        in_specs=[ua_spec(e) for e in range(tiles_per_step)]
                 + [act(0), act(COL_GA), act(COL_GM), act(0),
                    pl.BlockSpec((1, 3, D), lambda i: (i // steps_per_seq, 0, 0)),
                    wspec, wspec, wspec],
        out_specs=pl.BlockSpec((tm, D), lambda i: (i, 0)),
        out_shape=jax.ShapeDtypeStruct((T, D), F32),
        compiler_params=pltpu.CompilerParams(dimension_semantics=("arbitrary",),
                                             vmem_limit_bytes=VMEM_LIMIT),
        name="outp",
    )(*([ua] * tiles_per_step), um, p, p, x2, ada3, wa, wm, wo)


def _rel_bias_tables(rel_bias):
    max_exact = REL_BUCKETS // 2
    rb = rel_bias.astype(F32)
    rb = (rb - rb[REL_BUCKETS - 1:, :]) * LOG2E
    key = jnp.arange(MOBA_BLOCK)[:, None]
    qry = jnp.arange(MOBA_BLOCK)[None, :]
    d_own = qry - key

    def tile(dist):
        nf = jnp.maximum(dist, 1).astype(F32)
        large = max_exact + (jnp.log(nf / max_exact) / math.log(REL_MAX_DIST / max_exact)
                             * (REL_BUCKETS - max_exact)).astype(jnp.int32)
        bucket = jnp.where(dist < max_exact, dist, jnp.minimum(large, REL_BUCKETS - 1))
        onehot = (bucket[..., None] == jnp.arange(REL_BUCKETS)).astype(F32)
        return jnp.einsum('kqb,bh->hkq', onehot, rb, precision=lax.Precision.HIGHEST)

    own = jnp.where(d_own >= 0, tile(jnp.maximum(d_own, 0)), NEG_INF)
    prev = tile(d_own + MOBA_BLOCK)
    return own, prev


def _layer(x2, c, w_ada, b_ada, norm_w, w_in_all, layer, q_norm_w, k_norm_w, rel_bias, conv_w, conv_b,
           b_igate, b_fgate, ml_norm_w, w_att_proj, w_ml_proj, w_out, batch, seq):
    D = D_MODEL
    g0 = 4 * D + 5 * D
    ada3 = _ada_call(c, w_ada, b_ada).reshape(batch, 3, D)

    wa = w_in_all[layer, :, :g0].astype(BF16)
    wa = jnp.pad(wa, ((0, 0), (0, -g0 % (PROJ_STEP_TILES * PROJ_TN))))
    wb = w_in_all[layer, :, g0 + GATE_COLS:].astype(BF16)
    wg = jnp.pad(w_in_all[layer, :, g0:g0 + GATE_COLS], ((0, 0), (0, LANES - GATE_COLS))).astype(BF16)
    gb = jnp.pad(jnp.concatenate([b_igate, b_fgate]), (0, LANES - GATE_COLS)).reshape(1, LANES)
    q_scale = ATT_HEAD_DIM ** -0.5 * LOG2E
    hnw = jnp.stack([jnp.tile(q_norm_w.astype(F32), ATT_HEADS) * q_scale,
                     jnp.tile(k_norm_w.astype(F32), ATT_HEADS)])

    p, g, gt = _inproj_call(x2, ada3, norm_w.reshape(1, D), wa, wb, wg, gb, hnw, seq)
    bias_own, bias_prev = _rel_bias_tables(rel_bias)
    ua = _moba_call(p, bias_own, bias_prev, batch, seq).reshape(batch * seq, D)
    um = _mlstm_call(p, g, gt, conv_w, conv_b.reshape(1, -1), ml_norm_w.reshape(1, -1), batch, seq)
    return _outp_call(ua, um, p, x2, ada3, w_att_proj.astype(BF16), w_ml_proj.astype(BF16),
                      w_out.astype(BF16), seq)


def kernel(x, c, w_ada, b_ada, norm_w, w_in, q_norm_w, k_norm_w, rel_bias, conv_w, conv_b, b_igate,
           b_fgate, ml_norm_w, w_att_proj, w_ml_proj, w_out):
    batch, seq, D = x.shape
    assert D == D_MODEL and seq % PROJ_TM == 0 and seq % MOBA_BLOCK == 0 and seq % ML_CHUNK == 0
    x2 = x.reshape(batch * seq, D)
    for l in range(w_in.shape[0]):
        x2 = _layer(x2, c, w_ada[l], b_ada[l], norm_w[l], w_in, l, q_norm_w[l], k_norm_w[l],
                    rel_bias, conv_w[l], conv_b[l], b_igate[l], b_fgate[l], ml_norm_w[l],
                    w_att_proj[l], w_ml_proj[l], w_out[l], batch, seq)
    return x2.reshape(batch, seq, D)
```

```python
import math

import jax
import jax.numpy as jnp
from jax import lax
from jax.experimental import pallas as pl
from jax.experimental.pallas import tpu as pltpu

F32 = jnp.float32
BF16 = jnp.bfloat16

D_MODEL = 1024
ATT_HEADS = 8
ATT_HEAD_DIM = D_MODEL // ATT_HEADS
MOBA_BLOCK = 256
MOBA_TOPK = 3
REL_BUCKETS = 32
REL_MAX_DIST = 128
ML_HEADS = 4
ML_HEAD_DIM = D_MODEL // ML_HEADS
CONV_WIDTH = 4
EPS = 1e-6
NEG_INF = -1e30
LOG2E = math.log2(math.e)

LANES = 128
SUBLANES = 8
BF16_SUBLANES = 16
GATE_COLS = 2 * ML_HEADS
ML_CHUNK = 256
PROJ_TM = 1024
PROJ_TN = 1024
PROJ_STEP_TILES = 2
WA_RING = 3
OUT_TM = 512
MOBA_HEAD_GROUP = 4
KMEAN_TERMS = 3
VMEM_LIMIT = 52 * 1024 * 1024

COL_QA, COL_KA, COL_VA, COL_ZA, COL_QM, COL_KM, COL_VM, COL_ZM, COL_OM, COL_GA, COL_GM = range(11)
N_COL_TILES = 11
N_HEAD_TILES = 9

_NT = (((1,), (1,)), ((), ()))
_TN = (((0,), (0,)), ((), ()))


def _sigmoid(x):
    return 1.0 / (1.0 + jnp.exp2(x * (-LOG2E)))


def _log_sigmoid(x):
    return jnp.minimum(x, 0.0) - jnp.log1p(jnp.exp(-jnp.abs(x)))


def _ada_kernel(c_ref, w_ref, b_ref, o_ref):
    o_ref[...] = jnp.dot(c_ref[...], w_ref[...], precision=lax.Precision.HIGHEST,
                         preferred_element_type=F32) + b_ref[...]


def _ada_call(c, w_ada, b_ada):
    B, D = c.shape
    return pl.pallas_call(
        _ada_kernel,
        grid=(3,),
        in_specs=[pl.BlockSpec((B, D), lambda j: (0, 0)),
                  pl.BlockSpec((D, D), lambda j: (0, j)),
                  pl.BlockSpec((1, D), lambda j: (0, j))],
        out_specs=pl.BlockSpec((B, D), lambda j: (0, j)),
        out_shape=jax.ShapeDtypeStruct((B, 3 * D), F32),
        compiler_params=pltpu.CompilerParams(dimension_semantics=("arbitrary",),
                                             vmem_limit_bytes=VMEM_LIMIT),
        name="ada",
    )(c, w_ada, b_ada.reshape(1, 3 * D))


def _inproj_kernel(x_hbm, ada_ref, nw_ref, wa_hbm, wb_ref, wg_ref, gb_ref, hnw_ref,
                   p_ref, g_ref, gt_ref, h_scr, wa_ring, wa_sems, x_ring, x_sems):
    i = pl.program_id(0)
    j = pl.program_id(1)
    n_i = pl.num_programs(0)
    n_j = pl.num_programs(1)
    tn = PROJ_TN
    ts = PROJ_STEP_TILES * tn
    wa_steps = N_HEAD_TILES // PROJ_STEP_TILES
    wa_chunks = wa_steps + 1

    def wa_copy(ii, jj):
        slot = (ii * wa_chunks + jj) % WA_RING
        return pltpu.make_async_copy(wa_hbm.at[:, pl.ds(pl.multiple_of(jj * ts, ts), ts)],
                                     wa_ring.at[slot], wa_sems.at[slot])

    step = i * n_j + j

    @pl.when(step == 0)
    def _():
        for jj in range(WA_RING - 1):
            wa_copy(0, jj).start()

    ahead = step + WA_RING - 1
    i_ahead, j_ahead = ahead // n_j, ahead % n_j

    @pl.when(jnp.logical_and(j_ahead < wa_chunks, i_ahead < n_i))
    def _():
        wa_copy(i_ahead, j_ahead).start()

    @pl.when(j < wa_chunks)
    def _():
        wa_copy(i, j).wait()

    wa_ref = wa_ring.at[(i * wa_chunks + j) % WA_RING]

    tm = x_ring.shape[1]

    def x_copy(ii):
        slot = ii % 2
        return pltpu.make_async_copy(x_hbm.at[pl.ds(pl.multiple_of(ii * tm, tm), tm), :],
                                     x_ring.at[slot], x_sems.at[slot])

    @pl.when(step == 0)
    def _():
        x_copy(0).start()

    @pl.when(jnp.logical_and(j == 1, i + 1 < n_i))
    def _():
        x_copy(i + 1).start(priority=1)

    def project(w):
        return jnp.dot(h_scr[...], w, preferred_element_type=F32).astype(BF16)

    @pl.when(j == 0)
    def _():
        x_copy(i).wait()
        x = x_ring[i % 2]
        ada = ada_ref[0]
        y = x * lax.rsqrt(jnp.mean(x * x, axis=-1, keepdims=True) + EPS) * nw_ref[...]
        h = (y * (1.0 + ada[1:2, :]) + ada[0:1, :]).astype(BF16)
        h_scr[...] = h
        g = jnp.dot(h, wg_ref[...], preferred_element_type=F32) + gb_ref[...]
        g_ref[...] = g
        gt_ref[...] = g.T[:GATE_COLS, :]

    @pl.when(j == 0)
    def _():
        h = h_scr[...]
        pair = 2 * ATT_HEAD_DIM
        for t in (COL_QA, COL_KA):
            w = hnw_ref[t:t + 1, :]
            for c0 in range(0, tn, pair):
                acc = jnp.dot(h, wa_ref[:, t * tn + c0:t * tn + c0 + pair], preferred_element_type=F32)
                for c in range(c0, c0 + pair, ATT_HEAD_DIM):
                    a = acc[:, c - c0:c - c0 + ATT_HEAD_DIM]
                    r = lax.rsqrt(jnp.mean(a * a, axis=-1, keepdims=True) + EPS)
                    p_ref[:, t * tn + c:t * tn + c + ATT_HEAD_DIM] = (
                        a * r * w[:, c:c + ATT_HEAD_DIM]).astype(BF16)

    @pl.when(jnp.logical_and(j > 0, j < wa_steps))
    def _():
        p_ref[...] = project(wa_ref[...])

    @pl.when(j == wa_steps)
    def _():
        p_ref[:, :tn] = project(wa_ref[:, :tn])
        p_ref[:, tn:] = project(wb_ref[:, :tn])

    @pl.when(j == wa_steps + 1)
    def _():
        p_ref[:, :tn] = project(wb_ref[:, tn:])
        p_ref[:, tn:] = jnp.zeros((p_ref.shape[0], tn), BF16)


def _inproj_call(x2, ada3, norm_w, wa, wb, wg, gb, hnw, seq):
    T, D = x2.shape
    tm, tn = PROJ_TM, PROJ_TN
    ts = PROJ_STEP_TILES * tn
    tiles_per_seq = seq // tm
    wa_steps = N_HEAD_TILES // PROJ_STEP_TILES
    n_steps = wa_steps + 2
    assert PROJ_STEP_TILES == 2 and N_HEAD_TILES % 2 == 1 and N_COL_TILES - N_HEAD_TILES == 2
    return pl.pallas_call(
        _inproj_kernel,
        grid=(T // tm, n_steps),
        in_specs=[pl.BlockSpec(memory_space=pl.ANY),
                  pl.BlockSpec((1, 3, D), lambda i, j: (i // tiles_per_seq, 0, 0)),
                  pl.BlockSpec((1, D), lambda i, j: (0, 0)),
                  pl.BlockSpec(memory_space=pl.ANY),
                  pl.BlockSpec((D, ts), lambda i, j: (0, 0), pipeline_mode=pl.Buffered(1)),
                  pl.BlockSpec((D, LANES), lambda i, j: (0, 0)),
                  pl.BlockSpec((1, LANES), lambda i, j: (0, 0)),
                  pl.BlockSpec((2, tn), lambda i, j: (0, 0))],
        out_specs=[pl.BlockSpec((tm, ts), lambda i, j: (i, j)),
                   pl.BlockSpec((tm, LANES), lambda i, j: (i, 0)),
                   pl.BlockSpec((GATE_COLS, tm), lambda i, j: (0, i))],
        out_shape=[jax.ShapeDtypeStruct((T, n_steps * ts), BF16),
                   jax.ShapeDtypeStruct((T, LANES), F32),
                   jax.ShapeDtypeStruct((GATE_COLS, T), F32)],
        scratch_shapes=[pltpu.VMEM((tm, D), BF16),
                        pltpu.VMEM((WA_RING, D, ts), BF16),
                        pltpu.SemaphoreType.DMA((WA_RING,)),
                        pltpu.VMEM((2, tm, D), F32),
                        pltpu.SemaphoreType.DMA((2,))],
        compiler_params=pltpu.CompilerParams(dimension_semantics=("arbitrary", "arbitrary"),
                                             vmem_limit_bytes=VMEM_LIMIT),
        name="inproj",
    )(x2, ada3, norm_w, wa, wb, wg, gb, hnw)


def _moba_kernel(qa_ref, qb_ref, k_ref, v_ref, zaa_ref, zab_ref, bown_ref, bprev_ref, o_ref,
                 kmean_f32_scr, kmean_scr, vt_scr, q2_scr, selb_scr, s_scr, m8_scr):
    step = pl.program_id(2)
    blk, d = MOBA_BLOCK, ATT_HEAD_DIM
    da = d + BF16_SUBLANES
    nb = k_ref.shape[0] // blk
    heads = qa_ref.shape[1] // d
    tile = (step, nb - 1 - step)
    n_far = (jnp.maximum(step - 1, 0), nb - 2 - step)
    near_slots = 4
    far_slots = nb - near_slots + 1

    def hsl(hd):
        return slice(hd * d, (hd + 1) * d)

    @pl.when(step == 0)
    def _():
        for n in range(nb):
            kb = k_ref[n * blk:(n + 1) * blk, :].astype(F32)
            kmean_f32_scr[n:n + 1, :] = jnp.mean(kb, axis=0, keepdims=True)
            for hd in range(heads):
                vt_scr[n, hd * da:hd * da + d, :] = v_ref[n * blk:(n + 1) * blk, hsl(hd)].T
                vt_scr[n, hd * da + d:(hd + 1) * da, :] = jnp.ones((da - d, blk), BF16)

        rest = kmean_f32_scr[...]
        for c in range(KMEAN_TERMS):
            term = rest.astype(BF16)
            kmean_scr[c * nb:(c + 1) * nb, :] = term
            rest = rest - term.astype(F32)

    q2_scr[0] = qa_ref[...]
    q2_scr[1] = qb_ref[...]

    def fold(x):
        return x.reshape(blk // SUBLANES, SUBLANES, blk)

    def col_max8(s):
        x = fold(s)
        n = x.shape[0]
        while n > 1:
            n //= 2
            x = jnp.maximum(x[:n], x[n:2 * n])
        return x[0]

    rows = lax.broadcasted_iota(jnp.int32, (nb, blk), 0)

    def select_blocks(hd):
        for x in range(2):
            valid = rows < tile[x]
            terms = lax.dot_general(kmean_scr[:, hsl(hd)], q2_scr[x, :, hsl(hd)], _NT,
                                    preferred_element_type=F32)
            gate = terms[:nb]
            for c in range(1, KMEAN_TERMS):
                gate = gate + terms[c * nb:(c + 1) * nb]
            g = jnp.where(valid, gate, -jnp.inf)
            selb = jnp.full(gate.shape, NEG_INF, F32)
            for _ in range(MOBA_TOPK):
                mx = jnp.max(g, axis=0, keepdims=True)
                first = jnp.min(jnp.where(g == mx, rows, nb), axis=0, keepdims=True)
                pick = rows == first
                selb = jnp.where(pick, 0.0, selb)
                g = jnp.where(pick, -jnp.inf, g)
            selb_scr[x, hd] = jnp.where(valid, selb, NEG_INF)

    def near_items(hd):
        first = step == 0
        jb = tile[1] - 1
        ja = jnp.where(first, far_slots, step - 1)
        add_a = jnp.where(first, selb_scr[1, hd, pl.ds(far_slots, 1), :],
                          bprev_ref[hd] + selb_scr[0, hd, pl.ds(jnp.maximum(step - 1, 0), 1), :])
        return [(0, 0, tile[0], bown_ref[hd], None),
                (1, 1, tile[1], bown_ref[hd], None),
                (2, 1, jb, bprev_ref[hd] + selb_scr[1, hd, pl.ds(jb, 1), :], None),
                (3, jnp.where(first, 1, 0), ja, add_a, jnp.logical_not(first))]

    shared_slots = nb // 2 - 2

    def far_items(hd):
        out = []
        for t in range(far_slots):
            if t < shared_slots:
                is_a = t < n_far[0]
                x = jnp.where(is_a, 0, 1)
                j = jnp.where(is_a, t, t - n_far[0])
            else:
                is_a, x, j = None, 1, t - n_far[0]
            out.append((near_slots + t, x, j, selb_scr[x, hd, pl.ds(j, 1), :], is_a))
        return out

    def pass1(hd, items):
        m8 = [None, None]
        for slot, x, j, add, is_a in items:
            kj = k_ref[pl.ds(pl.multiple_of(j * blk, blk), blk), hsl(hd)]
            s = lax.dot_general(kj, q2_scr[x, :, hsl(hd)], _NT, preferred_element_type=F32) + add
            s_scr[hd, slot] = s
            mt = col_max8(s)
            for y in range(2):
                if is_a is None:
                    if y != x:
                        continue
                    part = mt
                else:
                    part = jnp.where(is_a, mt, -jnp.inf) if y == 0 else jnp.where(is_a, -jnp.inf, mt)
                m8[y] = part if m8[y] is None else jnp.maximum(m8[y], part)
        for x in range(2):
            m8_scr[x, hd] = jnp.broadcast_to(jnp.max(m8[x], axis=0, keepdims=True), (SUBLANES, blk))

    def pass2(hd, items):
        acc = [None, None]
        acc_shared = None
        for slot, x, j, _, is_a in items:
            p = jnp.exp2(fold(s_scr[hd, slot]) - m8_scr[x, hd])
            pb = p.reshape(blk, blk).astype(BF16)
            vt = vt_scr[j, hd * da:(hd + 1) * da, :]
            if is_a is None:
                pv = jnp.dot(vt, pb, preferred_element_type=F32)
                acc[x] = pv if acc[x] is None else acc[x] + pv
            else:
                zero = jnp.zeros_like(vt)
                lhs = jnp.concatenate([jnp.where(is_a, vt, zero), jnp.where(is_a, zero, vt)], axis=0)
                pv = jnp.dot(lhs, pb, preferred_element_type=F32)
                acc_shared = pv if acc_shared is None else acc_shared + pv
        for x, za_ref in enumerate((zaa_ref, zab_ref)):
            tot = acc[x] + acc_shared[x * da:(x + 1) * da]
            y = (tot[:d] * (1.0 / tot[d:d + 1])).T
            za = za_ref[:, hsl(hd)].astype(F32)
            o_ref[0, x, 0, :, hsl(hd)] = (y * (za * _sigmoid(za))).astype(BF16)

    items = []
    for hd in range(heads):
        select_blocks(hd)
        items.append(near_items(hd) + far_items(hd))
        pass1(hd, items[hd])
        if hd:
            pass2(hd - 1, items[hd - 1])
    pass2(heads - 1, items[heads - 1])


def _moba_call(p, bias_own, bias_prev, batch, seq):
    blk, d = MOBA_BLOCK, ATT_HEAD_DIM
    nq = seq // blk
    hg = MOBA_HEAD_GROUP
    w = hg * d
    gpt = PROJ_TN // w
    first = lambda col: (lambda b, h, i: (b * nq + i, col * gpt + h))
    last = lambda col: (lambda b, h, i: (b * nq + nq - 1 - i, col * gpt + h))
    return pl.pallas_call(
        _moba_kernel,
        grid=(batch, ATT_HEADS // hg, nq // 2),
        in_specs=[pl.BlockSpec((blk, w), first(COL_QA)),
                  pl.BlockSpec((blk, w), last(COL_QA)),
                  pl.BlockSpec((seq, w), lambda b, h, i: (b, COL_KA * gpt + h)),
                  pl.BlockSpec((seq, w), lambda b, h, i: (b, COL_VA * gpt + h)),
                  pl.BlockSpec((blk, w), first(COL_ZA)),
                  pl.BlockSpec((blk, w), last(COL_ZA)),
                  pl.BlockSpec((hg, blk, blk), lambda b, h, i: (h, 0, 0)),
                  pl.BlockSpec((hg, blk, blk), lambda b, h, i: (h, 0, 0))],
        out_specs=pl.BlockSpec((1, 2, 1, blk, w), lambda b, h, i: (b, 0, i, 0, h)),
        out_shape=jax.ShapeDtypeStruct((batch, 2, nq // 2, blk, ATT_HEADS * d), BF16),
        scratch_shapes=[pltpu.VMEM((nq, w), F32),
                        pltpu.VMEM((KMEAN_TERMS * nq, w), BF16),
                        pltpu.VMEM((nq, hg * (d + BF16_SUBLANES), blk), BF16),
                        pltpu.VMEM((2, blk, w), BF16),
                        pltpu.VMEM((2, hg, nq, blk), F32),
                        pltpu.VMEM((hg, nq + 1, blk, blk), F32),
                        pltpu.VMEM((2, hg, SUBLANES, blk), F32)],
        compiler_params=pltpu.CompilerParams(
            dimension_semantics=("arbitrary", "arbitrary", "arbitrary"),
            vmem_limit_bytes=VMEM_LIMIT),
        name="moba",
    )(p, p, p, p, p, p, bias_own, bias_prev)


def _ua_tile_row(t, nq):
    half = nq // 2
    return jnp.where(t < half, t, half + (nq - 1 - t))


def _mlstm_kernel(qm_ref, km_ref, vm_ref, zm_ref, om_ref, g_ref, gt_ref, cw_ref, cb_ref, mlw_ref,
                  o_ref, c_scr, n_scr, m_scr, tailq_scr, tailk_scr, shift_scr):
    c_idx = pl.program_id(1)
    L = qm_ref.shape[0]
    W = ML_HEADS * ML_HEAD_DIM
    pad = SUBLANES

    r_i = lax.broadcasted_iota(jnp.int32, (L, L), 0)
    c_i = lax.broadcasted_iota(jnp.int32, (L, L), 1)

    @pl.when(c_idx == 0)
    def _():
        c_scr[...] = jnp.zeros_like(c_scr)
        n_scr[...] = jnp.zeros_like(n_scr)
        m_scr[...] = jnp.zeros_like(m_scr)
        tailq_scr[...] = jnp.zeros_like(tailq_scr)
        tailk_scr[...] = jnp.zeros_like(tailk_scr)
        for k in range(CONV_WIDTH):
            shift_scr[k] = (c_i == r_i - k).astype(F32).astype(BF16)

    row8 = lax.broadcasted_iota(jnp.int32, (pad, W), 0)

    def conv_silu(src_ref, tail_scr, col0):
        x = src_ref[...]
        tail = tail_scr[...]
        y = cb_ref[:, col0:col0 + W]
        fix = jnp.zeros((pad, W), F32)
        for w in range(CONV_WIDTH):
            k = CONV_WIDTH - 1 - w
            cw = cw_ref[w:w + 1, col0:col0 + W]
            y = y + cw * jnp.dot(shift_scr[k], x, preferred_element_type=F32)
            if k:
                fix = fix + cw * jnp.where(row8 < k, pltpu.roll(tail, k, axis=0), 0.0)
        tail_scr[...] = x[L - 2 * pad:, :].astype(F32)[pad:, :]
        y = jnp.concatenate([y[:pad] + fix, y[pad:]], axis=0)
        return y * _sigmoid(y)

    qc = conv_silu(qm_ref, tailq_scr, 0)
    kc = conv_silu(km_ref, tailk_scr, W) * (ML_HEAD_DIM ** -0.5)

    causal = c_i <= r_i
    ltri = causal.astype(F32)
    utri = (r_i <= c_i).astype(F32)
    g_col = g_ref[...]
    g_row = gt_ref[...]
    b_col_all = jnp.dot(ltri, _log_sigmoid(g_col), precision=lax.Precision.HIGHEST,
                        preferred_element_type=F32)
    b_row_all = jnp.dot(_log_sigmoid(g_row), utri, precision=lax.Precision.HIGHEST,
                        preferred_element_type=F32)

    for hh in range(ML_HEADS):
        sl = slice(hh * ML_HEAD_DIM, (hh + 1) * ML_HEAD_DIM)
        q = qc[:, sl]
        k = kc[:, sl]
        v = vm_ref[:, sl]
        b_col = b_col_all[:, ML_HEADS + hh:ML_HEADS + hh + 1]
        a_col = g_col[:, hh:hh + 1] - b_col
        a_row = g_row[hh:hh + 1, :] - b_row_all[ML_HEADS + hh:ML_HEADS + hh + 1, :]
        m_prev = m_scr[hh][0:1, 0:1]

        a_mask = jnp.where(causal, a_row, -jnp.inf)
        gcol = jnp.maximum(m_prev, jnp.max(a_mask, axis=1, keepdims=True))
        d_mat = jnp.exp(a_mask - gcol)
        inter = jnp.exp(m_prev - gcol)
        g_last = jnp.max(gcol, axis=0, keepdims=True)

        qb = q.astype(BF16)
        s = lax.dot_general(qb, k.astype(BF16), _NT, preferred_element_type=F32) * d_mat
        c_old = c_scr[hh]
        n_old = n_scr[hh]
        num = (inter * jnp.dot(qb, c_old.astype(BF16), preferred_element_type=F32)
               + jnp.dot(s.astype(BF16), v, preferred_element_type=F32))
        den = inter * jnp.sum(q * n_old, axis=1, keepdims=True) + jnp.sum(s, axis=1, keepdims=True)
        h = num / jnp.maximum(jnp.abs(den), jnp.exp(-(b_col + gcol)))

        decay = jnp.exp(m_prev - g_last)
        kw = k * jnp.exp(a_col - g_last)
        c_scr[hh] = decay * c_old + lax.dot_general(kw.astype(BF16), v, _TN,
                                                    preferred_element_type=F32)
        n_scr[hh] = decay * n_old + jnp.sum(kw, axis=0, keepdims=True)
        m_scr[hh] = jnp.broadcast_to(b_col[L - 1:L, :] + g_last, m_scr.shape[1:])

        hg = _sigmoid(om_ref[:, sl].astype(F32)) * h
        hn = hg * lax.rsqrt(jnp.mean(hg * hg, axis=-1, keepdims=True) + EPS) * mlw_ref[:, sl]
        z = zm_ref[:, sl].astype(F32)
        o_ref[:, sl] = (hn * (z * _sigmoid(z))).astype(BF16)


def _mlstm_call(p, g, gt, conv_w, conv_b, ml_norm_w, batch, seq):
    T = p.shape[0]
    L = ML_CHUNK
    nc = seq // L
    W = ML_HEADS * ML_HEAD_DIM
    row = lambda b, c: b * nc + c
    pspec = lambda col: pl.BlockSpec((L, W), lambda b, c: (row(b, c), col))
    return pl.pallas_call(
        _mlstm_kernel,
        grid=(batch, nc),
        in_specs=[pspec(COL_QM), pspec(COL_KM), pspec(COL_VM), pspec(COL_ZM), pspec(COL_OM),
                  pl.BlockSpec((L, LANES), lambda b, c: (row(b, c), 0)),
                  pl.BlockSpec((GATE_COLS, L), lambda b, c: (0, row(b, c))),
                  pl.BlockSpec((CONV_WIDTH, 2 * W), lambda b, c: (0, 0)),
                  pl.BlockSpec((1, 2 * W), lambda b, c: (0, 0)),
                  pl.BlockSpec((1, W), lambda b, c: (0, 0))],
        out_specs=pl.BlockSpec((L, W), lambda b, c: (row(b, c), 0)),
        out_shape=jax.ShapeDtypeStruct((T, W), BF16),
        scratch_shapes=[pltpu.VMEM((ML_HEADS, ML_HEAD_DIM, ML_HEAD_DIM), F32),
                        pltpu.VMEM((ML_HEADS, 1, ML_HEAD_DIM), F32),
                        pltpu.VMEM((ML_HEADS, SUBLANES, LANES), F32),
                        pltpu.VMEM((SUBLANES, W), F32),
                        pltpu.VMEM((SUBLANES, W), F32),
                        pltpu.VMEM((CONV_WIDTH, L, L), BF16)],
        compiler_params=pltpu.CompilerParams(dimension_semantics=("arbitrary", "arbitrary"),
                                             vmem_limit_bytes=VMEM_LIMIT),
        name="mlstm",
    )(p, p, p, p, p, g, gt, conv_w, conv_b, ml_norm_w)


def _outp_kernel(*refs):
    ua_refs, (um_ref, ga_ref, gm_ref, x_ref, ada_ref, wa_ref, wm_ref, wo_ref, o_ref) = refs[:-9], refs[-9:]
    ua = jnp.concatenate([r[...] for r in ua_refs], axis=0)
    ya = jnp.dot(ua, wa_ref[...], preferred_element_type=F32)
    ym = jnp.dot(um_ref[...], wm_ref[...], preferred_element_type=F32)
    y = _sigmoid(ga_ref[...].astype(F32)) * ya + _sigmoid(gm_ref[...].astype(F32)) * ym
    gate = ada_ref[0][2:3, :]
    o_ref[...] = x_ref[...] + gate * jnp.dot(y.astype(BF16), wo_ref[...], preferred_element_type=F32)


def _outp_call(ua, um, p, x2, ada3, wa, wm, wo, seq):
    T, D = x2.shape
    tm = OUT_TM
    steps_per_seq = seq // tm
    blk = MOBA_BLOCK
    nq = seq // blk
    tiles_per_step = tm // blk

    def ua_spec(e):
        def index(i):
            t = (i % steps_per_seq) * tiles_per_step + e
            return ((i // steps_per_seq) * nq + _ua_tile_row(t, nq), 0)
        return pl.BlockSpec((blk, D), index)

    act = lambda col: pl.BlockSpec((tm, D), lambda i: (i, col))
    wspec = pl.BlockSpec((D, D), lambda i: (0, 0))
    return pl.pallas_call(
        _outp_kernel,
        grid=(T // tm,),
        in_specs=[ua_spec(e) for e in range(tiles_per_step)]
                 + [act(0), act(COL_GA), act(COL_GM), act(0),
                    pl.BlockSpec((1, 3, D), lambda i: (i // steps_per_seq, 0, 0)),
                    wspec, wspec, wspec],
        out_specs=pl.BlockSpec((tm, D), lambda i: (i, 0)),
        out_shape=jax.ShapeDtypeStruct((T, D), F32),
        compiler_params=pltpu.CompilerParams(dimension_semantics=("arbitrary",),
                                             vmem_limit_bytes=VMEM_LIMIT),
        name="outp",
    )(*([ua] * tiles_per_step), um, p, p, x2, ada3, wa, wm, wo)


def _rel_bias_tables(rel_bias):
    max_exact = REL_BUCKETS // 2
    rb = rel_bias.astype(F32)
    rb = (rb - rb[REL_BUCKETS - 1:, :]) * LOG2E
    key = jnp.arange(MOBA_BLOCK)[:, None]
    qry = jnp.arange(MOBA_BLOCK)[None, :]
    d_own = qry - key

    def tile(dist):
        nf = jnp.maximum(dist, 1).astype(F32)
        large = max_exact + (jnp.log(nf / max_exact) / math.log(REL_MAX_DIST / max_exact)
                             * (REL_BUCKETS - max_exact)).astype(jnp.int32)
        bucket = jnp.where(dist < max_exact, dist, jnp.minimum(large, REL_BUCKETS - 1))
        onehot = (bucket[..., None] == jnp.arange(REL_BUCKETS)).astype(F32)
        return jnp.einsum('kqb,bh->hkq', onehot, rb, precision=lax.Precision.HIGHEST)

    own = jnp.where(d_own >= 0, tile(jnp.maximum(d_own, 0)), NEG_INF)
    prev = tile(d_own + MOBA_BLOCK)
    return own, prev


def _layer(x2, c, w_ada, b_ada, norm_w, w_in_all, layer, q_norm_w, k_norm_w, rel_bias, conv_w, conv_b,
           b_igate, b_fgate, ml_norm_w, w_att_proj, w_ml_proj, w_out, batch, seq):
    D = D_MODEL
    g0 = 4 * D + 5 * D
    ada3 = _ada_call(c, w_ada, b_ada).reshape(batch, 3, D)

    wa = w_in_all[layer, :, :g0].astype(BF16)
    wa = jnp.pad(wa, ((0, 0), (0, -g0 % (PROJ_STEP_TILES * PROJ_TN))))
    wb = w_in_all[layer, :, g0 + GATE_COLS:].astype(BF16)
    wg = jnp.pad(w_in_all[layer, :, g0:g0 + GATE_COLS], ((0, 0), (0, LANES - GATE_COLS))).astype(BF16)
    gb = jnp.pad(jnp.concatenate([b_igate, b_fgate]), (0, LANES - GATE_COLS)).reshape(1, LANES)
    q_scale = ATT_HEAD_DIM ** -0.5 * LOG2E
    hnw = jnp.stack([jnp.tile(q_norm_w.astype(F32), ATT_HEADS) * q_scale,
                     jnp.tile(k_norm_w.astype(F32), ATT_HEADS)])

    p, g, gt = _inproj_call(x2, ada3, norm_w.reshape(1, D), wa, wb, wg, gb, hnw, seq)
    bias_own, bias_prev = _rel_bias_tables(rel_bias)
    ua = _moba_call(p, bias_own, bias_prev, batch, seq).reshape(batch * seq, D)
    um = _mlstm_call(p, g, gt, conv_w, conv_b.reshape(1, -1), ml_norm_w.reshape(1, -1), batch, seq)
    return _outp_call(ua, um, p, x2, ada3, w_att_proj.astype(BF16), w_ml_proj.astype(BF16),
                      w_out.astype(BF16), seq)


def kernel(x, c, w_ada, b_ada, norm_w, w_in, q_norm_w, k_norm_w, rel_bias, conv_w, conv_b, b_igate,
           b_fgate, ml_norm_w, w_att_proj, w_ml_proj, w_out):
    batch, seq, D = x.shape
    assert D == D_MODEL and seq % PROJ_TM == 0 and seq % MOBA_BLOCK == 0 and seq % ML_CHUNK == 0
    x2 = x.reshape(batch * seq, D)
    for l in range(w_in.shape[0]):
        x2 = _layer(x2, c, w_ada[l], b_ada[l], norm_w[l], w_in, l, q_norm_w[l], k_norm_w[l],
                    rel_bias, conv_w[l], conv_b[l], b_igate[l], b_fgate[l], ml_norm_w[l],
                    w_att_proj[l], w_ml_proj[l], w_out[l], batch, seq)
    return x2.reshape(batch, seq, D)
```
